```python
import numpy as np
import jax
import jax.numpy as jnp
from jax import lax

D_MODEL = 4096
BATCH = 4
SEQ = 2048
DEPTH = 2
DEC_BATCH = 8
DEC_SEQ = 8
PAST_LEN = 16384
PAGE_SIZE = 128

RET_HEAD_DIM = 256
RET_HEADS = D_MODEL // RET_HEAD_DIM
RET_V_DIM = 2 * RET_HEAD_DIM
RET_CHUNK = 128
ROPE_BASE = 10000.0
NSA_HEAD_DIM = 128
NSA_Q_HEADS = D_MODEL // NSA_HEAD_DIM
NSA_KV_HEADS = 4
CMP_LEN = 32
CMP_STRIDE = 16
CMP_HIDDEN = 512
SLC_BLOCK = 64
SLC_TOP = 16
SLC_LOCAL = 2
SLC_Q_BLOCK = 32
FORCE_BONUS = 1e4
WINDOW = 512
WIN_BLOCK = 128
N_BRANCH = 3
MOE_GROUPS = 8
MOE_EXPERTS_PER_GROUP = 8
MOE_EXPERTS = MOE_GROUPS * MOE_EXPERTS_PER_GROUP
MOE_TOP = 2
MOE_FF = 512
MOE_BLOCK = 128
PLE_DIM = 256
EPS = 1e-6

kernel_name = 'yoco_retnet_nsa_hmoe_step'


def rms_norm(x, g):
    xf = x.astype(jnp.float32)
    y = xf * lax.rsqrt(jnp.mean(xf * xf, axis=-1, keepdims=True) + EPS)
    return (y * g.astype(jnp.float32)).astype(x.dtype)


def rms_norm_plain(x):
    xf = x.astype(jnp.float32)
    return (xf * lax.rsqrt(jnp.mean(xf * xf, axis=-1, keepdims=True) + EPS)).astype(x.dtype)


def softmax_masked(scores, mask):
    s = jnp.where(mask, scores.astype(jnp.float32), -jnp.inf)
    m = jnp.max(s, axis=-1, keepdims=True)
    e = jnp.exp(s - jnp.where(jnp.isfinite(m), m, 0.0))
    den = jnp.sum(e, axis=-1, keepdims=True)
    return e / jnp.where(den > 0, den, 1.0)


def rotary(x, pos):
    half = x.shape[-1] // 2
    inv = ROPE_BASE ** (-jnp.arange(half, dtype=jnp.float32) / half)
    ang = pos.astype(jnp.float32)[:, None] * inv[None, :]
    cos = jnp.cos(ang)[None, :, None, :]
    sin = jnp.sin(ang)[None, :, None, :]
    xf = x.astype(jnp.float32)
    x1, x2 = xf[..., :half], xf[..., half:]
    return jnp.concatenate([x1 * cos - x2 * sin, x1 * sin + x2 * cos], axis=-1).astype(x.dtype)


def retention_chunkwise(q, k, v, s0):
    B, T, H, dk = q.shape
    dv = v.shape[-1]
    dt = q.dtype
    C = RET_CHUNK if T % RET_CHUNK == 0 else T
    n = T // C
    lg = jnp.log1p(-(2.0 ** (-5.0 - jnp.arange(H, dtype=jnp.float32))))
    idx = jnp.arange(C, dtype=jnp.float32)
    diff = idx[:, None] - idx[None, :]
    mask = jnp.where(diff >= 0, jnp.exp(jnp.maximum(diff, 0.0)[None] * lg[:, None, None]), 0.0).astype(dt)
    q_dec = jnp.exp((idx + 1.0)[None, :] * lg[:, None]).astype(dt)
    k_dec = jnp.exp((C - 1.0 - idx)[None, :] * lg[:, None]).astype(dt)
    c_dec = jnp.exp(C * lg).astype(dt)

    def to_chunks(a):
        return a.reshape(B, n, C, H, a.shape[-1]).transpose(1, 0, 3, 2, 4)

    def step(s, xs):
        qc, kc, vc = xs
        att = jnp.einsum('bhqd,bhkd->bhqk', qc, kc) * mask
        o = (jnp.einsum('bhqk,bhkv->bhqv', att, vc)
             + jnp.einsum('bhqd,bhdv->bhqv', qc * q_dec[..., None], s))
        s = s * c_dec[:, None, None] + jnp.einsum('bhkd,bhkv->bhdv', kc * k_dec[..., None], vc)
        return s, o

    s, o = lax.scan(step, s0.astype(dt), (to_chunks(q), to_chunks(k), to_chunks(v)))
    return o.transpose(1, 0, 3, 2, 4).reshape(B, T, H, dv), s


def retention_mixer(h, pos, s0, w_in, w_out):
    B, T, _ = h.shape
    H, dk, dv = RET_HEADS, RET_HEAD_DIM, RET_V_DIM
    q, k, v, g = jnp.split(h @ w_in, [H * dk, 2 * H * dk, 2 * H * dk + H * dv], axis=-1)
    q = rotary(q.reshape(B, T, H, dk), pos)
    k = rotary(k.reshape(B, T, H, dk), pos) * (dk ** -0.5)
    o, s = retention_chunkwise(q, k, v.reshape(B, T, H, dv), s0)
    o = rms_norm_plain(o).reshape(B, T, H * dv)
    return (jax.nn.silu(g) * o) @ w_out, s


def hier_moe(h, w_rg, b_rg, w_re, b_re, w_up, w_down):
    N, D = h.shape
    E = w_up.shape[0]
    hf = h.astype(jnp.float32)
    rows = jnp.arange(N)
    g_prob = jax.nn.softmax(hf @ w_rg.astype(jnp.float32) + b_rg.astype(jnp.float32), axis=-1)
    g_sel = jnp.argmax(g_prob, axis=-1)
    g_w = g_prob[rows, g_sel]
    e_all = jnp.einsum('nd,gde->nge', hf, w_re.astype(jnp.float32)) + b_re.astype(jnp.float32)
    e_prob = jax.nn.softmax(e_all[rows, g_sel], axis=-1)
    top_p, top_i = lax.top_k(e_prob, MOE_TOP)
    gates = g_w[:, None] * top_p / jnp.sum(top_p, axis=-1, keepdims=True)
    expert_ids = g_sel[:, None] * MOE_EXPERTS_PER_GROUP + top_i

    A = N * MOE_TOP
    M = max(8, min(MOE_BLOCK, A // E))
    n_blocks = -(-A // M) + E
    e_flat = expert_ids.reshape(-1)
    tok_flat = jnp.arange(A) // MOE_TOP
    g_flat = gates.reshape(-1)
    order = jnp.argsort(e_flat)
    e_sorted = e_flat[order]
    counts = jnp.bincount(e_flat, length=E)
    starts = jnp.cumsum(counts) - counts
    blocks_per = (counts + M - 1) // M
    blk_end = jnp.cumsum(blocks_per)
    pad_starts = (blk_end - blocks_per) * M
    dest = pad_starts[e_sorted] + jnp.arange(A) - starts[e_sorted]
    tok_buf = jnp.zeros((n_blocks * M,), jnp.int32).at[dest].set(tok_flat[order].astype(jnp.int32))
    gate_buf = jnp.zeros((n_blocks * M,), h.dtype).at[dest].set(g_flat[order].astype(h.dtype))
    block_expert = jnp.minimum(jnp.searchsorted(blk_end, jnp.arange(n_blocks), side='right'), E - 1)

    def run_block(args):
        e, toks, gw = args
        a, b = jnp.split(h[toks] @ w_up[e], 2, axis=-1)
        return ((jax.nn.silu(a) * b) @ w_down[e]) * gw[:, None]

    y_blocks = lax.map(run_block, (block_expert, tok_buf.reshape(n_blocks, M), gate_buf.reshape(n_blocks, M)))
    return jnp.zeros_like(h).at[tok_buf].add(y_blocks.reshape(n_blocks * M, D))


def shared_kv(x, g_kv, w_kv, g_k_slc, g_k_win):
    B, T, _ = x.shape
    kv = (rms_norm(x, g_kv) @ w_kv).reshape(B, T, 2 * N_BRANCH, NSA_KV_HEADS, NSA_HEAD_DIM)
    cmp_kv = kv[:, :, 0:2]
    slc_kv = jnp.stack([rms_norm(kv[:, :, 2], g_k_slc), kv[:, :, 3]], axis=2)
    win_kv = jnp.stack([rms_norm(kv[:, :, 4], g_k_win), kv[:, :, 5]], axis=2)
    return cmp_kv, slc_kv, win_kv


def compress(rows, pe, w1, w2):
    B, T, G, d = rows.shape
    r = CMP_LEN // CMP_STRIDE
    n_cmp = (T - CMP_LEN) // CMP_STRIDE + 1
    n_chunk = n_cmp + r - 1
    ch = rows[:, :n_chunk * CMP_STRIDE].reshape(B, n_chunk, CMP_STRIDE, G, d)
    ch = ch.transpose(0, 3, 1, 2, 4).reshape(B, G, n_chunk, CMP_STRIDE * d)
    w1r = w1.reshape(r, CMP_STRIDE * d, CMP_HIDDEN)
    hid = pe.reshape(-1) @ w1
    for i in range(r):
        hid = hid + jnp.einsum('bgcf,fh->bgch', ch[:, :, i:i + n_cmp], w1r[i])
    return jax.nn.gelu(hid) @ w2


def slc_map(n_cmp, n_slc):
    a = SLC_BLOCK // CMP_STRIDE
    b = CMP_LEN // CMP_STRIDE
    j = np.arange(n_slc)[:, None, None]
    i = j * a + np.arange(a)[None, :, None] + np.arange(b)[None, None, :] - b + 1
    i, jj = np.broadcast_arrays(i, j)
    ok = (i >= 0) & (i < n_cmp)
    m = np.zeros((n_cmp, n_slc), np.float32)
    np.add.at(m, (i[ok], jj[ok]), 1.0)
    return jnp.asarray(m)


def window_bands(kv):
    B, T = kv.shape[:2]
    WB = WIN_BLOCK if T % WIN_BLOCK == 0 else T
    nq = T // WB
    nprev = -(-WINDOW // WB)
    blocks = kv.reshape((B, nq, WB) + kv.shape[2:])
    padded = jnp.pad(blocks, ((0, 0), (nprev, 0), (0, 0), (0, 0), (0, 0), (0, 0)))
    bands = jnp.concatenate([padded[:, i:i + nq] for i in range(nprev + 1)], axis=2)
    kpos = ((jnp.arange(nq)[:, None] - nprev + jnp.arange(nprev + 1)[None, :])[:, :, None] * WB
            + jnp.arange(WB)[None, None, :]).reshape(nq, (nprev + 1) * WB)
    return bands[:, :, :, 0], bands[:, :, :, 1], kpos


def nsa_mixer(h, pos, k_cmp, v_cmp, k_sb, v_sb, k_wb, v_wb, kw_pos, w_in, g_q, w_out):
    B, T, _ = h.shape
    H, G, d = NSA_Q_HEADS, NSA_KV_HEADS, NSA_HEAD_DIM
    hpg = H // G
    dt = h.dtype
    scale = d ** -0.5
    proj = h @ w_in
    q = rms_norm(proj[..., :H * d].reshape(B, T, G, hpg, d), g_q)
    gates = jax.nn.sigmoid(proj[..., H * d:].reshape(B, T, G, hpg, N_BRANCH))
    qt = q.transpose(0, 2, 3, 1, 4)

    n_cmp = k_cmp.shape[2]
    cmp_end = jnp.arange(n_cmp) * CMP_STRIDE + CMP_LEN - 1
    s_c = jnp.einsum('bghtd,bgcd->bghtc', qt, k_cmp) * scale
    p_c = softmax_masked(s_c, (cmp_end[None, :] <= pos[:, None])[None, None, None])
    o_c = jnp.einsum('bghtc,bgcd->btghd', p_c.astype(dt), v_cmp)

    n_slc = k_sb.shape[2]
    imp = jnp.einsum('bghtc,cs->bgts', p_c, slc_map(n_cmp, n_slc))
    blk = jnp.arange(n_slc)
    cur = pos // SLC_BLOCK
    back = cur[:, None] - blk[None, :]
    valid = blk[None, :] * SLC_BLOCK <= pos[:, None]
    forced = (blk[None, :] == 0) | ((back >= 0) & (back < SLC_LOCAL))
    score = jnp.where(valid, imp + jnp.where(forced, FORCE_BONUS, 0.0), -jnp.inf)
    n_top = min(SLC_TOP, n_slc)
    _, sel = lax.top_k(score, n_top)

    qc = SLC_Q_BLOCK if T % SLC_Q_BLOCK == 0 else T
    nqc = T // qc
    q_ch = qt.reshape(B, G, hpg, nqc, qc, d).transpose(3, 0, 1, 2, 4, 5)
    sel_ch = sel.reshape(B, G, nqc, qc, n_top).transpose(2, 0, 1, 3, 4)
    pos_ch = pos.reshape(nqc, qc)
    gather = jax.vmap(jax.vmap(lambda blocks, ix: blocks[ix]))
    kk = n_top * SLC_BLOCK

    def slc_step(args):
        qb, ib, pb = args
        kb = gather(k_sb, ib).reshape(B, G, qc, kk, d)
        vb = gather(v_sb, ib).reshape(B, G, qc, kk, d)
        s = jnp.einsum('bghqd,bgqkd->bghqk', qb, kb) * scale
        kpos = (ib[..., None] * SLC_BLOCK + jnp.arange(SLC_BLOCK)).reshape(B, G, 1, qc, kk)
        p = softmax_masked(s, kpos <= pb[None, None, None, :, None])
        return jnp.einsum('bghqk,bgqkd->bqghd', p.astype(dt), vb)

    o_s = lax.map(slc_step, (q_ch, sel_ch, pos_ch)).transpose(1, 0, 2, 3, 4, 5).reshape(B, T, G, hpg, d)

    nq = kw_pos.shape[0]
    Qb = T // nq
    qw = q.reshape(B, nq, Qb, G, hpg, d).transpose(1, 0, 2, 3, 4, 5)

    def win_step(args):
        qb, kb, vb, qp, kp = args
        s = jnp.einsum('bqghd,bkgd->bghqk', qb, kb) * scale
        diff = qp[:, None] - kp[None, :]
        p = softmax_masked(s, (diff >= 0) & (diff < WINDOW) & (kp[None, :] >= 0))
        return jnp.einsum('bghqk,bkgd->bqghd', p.astype(dt), vb)

    o_w = lax.map(win_step, (qw, k_wb.transpose(1, 0, 2, 3, 4), v_wb.transpose(1, 0, 2, 3, 4),
                             pos.reshape(nq, Qb), kw_pos))
    o_w = o_w.transpose(1, 0, 2, 3, 4, 5).reshape(B, T, G, hpg, d)

    o = gates[..., 0:1] * o_c + gates[..., 1:2] * o_s + gates[..., 2:3] * o_w
    return o.reshape(B, T, H * d) @ w_out


def setup_inputs(seed: int = 0) -> dict:
    key = jax.random.key(seed)
    ks = iter(jax.random.split(key, 48))
    n_a = DEPTH // 2
    n_b = DEPTH - n_a
    n_pages = PAST_LEN // PAGE_SIZE
    n_pool = (5 * DEC_BATCH * n_pages + 3) // 4
    w_buf = min(WINDOW, PAST_LEN)
    G, d = NSA_KV_HEADS, NSA_HEAD_DIM
    f32 = jnp.float32

    def nrm(shape, scale=1.0):
        return jax.random.normal(next(ks), shape, f32) * scale

    def gain(shape):
        return 1.0 + nrm(shape, 0.05)

    ret_in_w = 2 * RET_HEADS * RET_HEAD_DIM + 2 * RET_HEADS * RET_V_DIM
    return {
        'x_prompt': nrm((BATCH, SEQ, D_MODEL)),
        'x_sample': nrm((DEC_BATCH, DEC_SEQ, D_MODEL)),
        'state_ret': nrm((n_a, DEC_BATCH, RET_HEADS, RET_HEAD_DIM, RET_V_DIM), 0.1),
        'cache_cmp_kv': nrm((n_pool, PAGE_SIZE, 2, G, d)),
        'cache_slc_kv': nrm((n_pool, PAGE_SIZE, 2, G, d)),
        'cache_win_kv': nrm((DEC_BATCH, w_buf, 2, G, d)),
        'page_table': jax.random.permutation(next(ks), n_pool)[:DEC_BATCH * n_pages].reshape(DEC_BATCH, n_pages).astype(jnp.int32),
        'p_prompt': nrm((DEPTH, BATCH, SEQ, PLE_DIM)),
        'p_sample': nrm((DEPTH, DEC_BATCH, DEC_SEQ, PLE_DIM)),
        'g_mix': gain((DEPTH, D_MODEL)),
        'g_ffn': gain((DEPTH, D_MODEL)),
        'w_ret_in': nrm((n_a, D_MODEL, ret_in_w), D_MODEL ** -0.5),
        'w_ret_out': nrm((n_a, RET_HEADS * RET_V_DIM, D_MODEL), (RET_HEADS * RET_V_DIM) ** -0.5),
        'w_nsa_in': nrm((n_b, D_MODEL, NSA_Q_HEADS * d + N_BRANCH * NSA_Q_HEADS), D_MODEL ** -0.5),
        'g_nsa_q': gain((n_b, d)),
        'w_nsa_out': nrm((n_b, NSA_Q_HEADS * d, D_MODEL), (NSA_Q_HEADS * d) ** -0.5),
        'g_kv': gain((D_MODEL,)),
        'w_kv': nrm((D_MODEL, 2 * N_BRANCH * G * d), D_MODEL ** -0.5),
        'g_k_cmp': gain((d,)),
        'g_k_slc': gain((d,)),
        'g_k_win': gain((d,)),
        'pe_cmp_k': nrm((CMP_LEN, d), 0.5),
        'w_cmp_k1': nrm((CMP_LEN * d, CMP_HIDDEN), (CMP_LEN * d) ** -0.5),
        'w_cmp_k2': nrm((CMP_HIDDEN, d), CMP_HIDDEN ** -0.5),
        'pe_cmp_v': nrm((CMP_LEN, d), 0.5),
        'w_cmp_v1': nrm((CMP_LEN * d, CMP_HIDDEN), (CMP_LEN * d) ** -0.5),
        'w_cmp_v2': nrm((CMP_HIDDEN, d), CMP_HIDDEN ** -0.5),
        'w_rg': nrm((DEPTH, D_MODEL, MOE_GROUPS), D_MODEL ** -0.5),
        'b_rg': nrm((DEPTH, MOE_GROUPS), 0.01),
        'w_re': nrm((DEPTH, MOE_GROUPS, D_MODEL, MOE_EXPERTS_PER_GROUP), D_MODEL ** -0.5),
        'b_re': nrm((DEPTH, MOE_GROUPS, MOE_EXPERTS_PER_GROUP), 0.01),
        'w_moe_up': nrm((DEPTH, MOE_EXPERTS, D_MODEL, 2 * MOE_FF), D_MODEL ** -0.5),
        'w_moe_down': nrm((DEPTH, MOE_EXPERTS, MOE_FF, D_MODEL), MOE_FF ** -0.5),
        'w_ple_up': nrm((DEPTH, PLE_DIM, D_MODEL), PLE_DIM ** -0.5),
        'g_ple': gain((DEPTH, D_MODEL)),
        'w_ple_gate': nrm((DEPTH, D_MODEL, D_MODEL), D_MODEL ** -0.5),
    }


def reference(x_prompt, x_sample, state_ret, cache_cmp_kv, cache_slc_kv, cache_win_kv, page_table,
              p_prompt, p_sample, g_mix, g_ffn, w_ret_in, w_ret_out, w_nsa_in, g_nsa_q, w_nsa_out,
              g_kv, w_kv, g_k_cmp, g_k_slc, g_k_win, pe_cmp_k, w_cmp_k1, w_cmp_k2, pe_cmp_v, w_cmp_v1,
              w_cmp_v2, w_rg, b_rg, w_re, b_re, w_moe_up, w_moe_down, w_ple_up, g_ple, w_ple_gate):
    G, d = NSA_KV_HEADS, NSA_HEAD_DIM
    n_a = w_ret_in.shape[0]

    def shared_context(cmp_full, slc_full):
        Bk, Tk = cmp_full.shape[:2]
        k_c = rms_norm(compress(cmp_full[:, :, 0], pe_cmp_k, w_cmp_k1, w_cmp_k2), g_k_cmp)
        v_c = compress(cmp_full[:, :, 1], pe_cmp_v, w_cmp_v1, w_cmp_v2)
        n_slc = -(-Tk // SLC_BLOCK)
        sb = jnp.pad(slc_full, ((0, 0), (0, n_slc * SLC_BLOCK - Tk), (0, 0), (0, 0), (0, 0)))
        sb = sb.reshape(Bk, n_slc, SLC_BLOCK, 2, G, d).transpose(3, 0, 4, 1, 2, 5)
        return k_c, v_c, sb[0], sb[1]

    def layer_stack(x, p, pos, ret_s0, past):
        B, T, _ = x.shape
        ret_states = []
        ctx = None
        for i in range(DEPTH):
            h = rms_norm(x, g_mix[i])
            if i < n_a:
                y, s = retention_mixer(h, pos, ret_s0[i], w_ret_in[i], w_ret_out[i])
                ret_states.append(s)
            else:
                j = i - n_a
                y = nsa_mixer(h, pos, *ctx, w_nsa_in[j], g_nsa_q[j], w_nsa_out[j])
            x = x + y
            hf = rms_norm(x, g_ffn[i]).reshape(B * T, -1)
            x = x + hier_moe(hf, w_rg[i], b_rg[i], w_re[i], b_re[i], w_moe_up[i], w_moe_down[i]).reshape(x.shape)
            x = x + (p[i] @ w_ple_up[i]) * jax.nn.sigmoid(rms_norm(x, g_ple[i]) @ w_ple_gate[i])
            if i == n_a - 1:
                cmp_new, slc_new, win_new = shared_kv(x, g_kv, w_kv, g_k_slc, g_k_win)
                if past is None:
                    cmp_full, slc_full = cmp_new, slc_new
                    k_wb, v_wb, kw_pos = window_bands(win_new)
                    win_state = win_new[:, T - min(WINDOW, T):]
                else:
                    cmp_past, slc_past, win_buf = past
                    past_len = cmp_past.shape[1]
                    cmp_full = jnp.concatenate([cmp_past, cmp_new], axis=1)
                    slc_full = jnp.concatenate([slc_past, slc_new], axis=1)
                    win_all = jnp.concatenate([win_buf, win_new], axis=1)
                    n_all = win_all.shape[1]
                    k_wb = win_all[:, None, :, 0]
                    v_wb = win_all[:, None, :, 1]
                    kw_pos = (past_len - win_buf.shape[1] + jnp.arange(n_all))[None, :]
                    win_state = win_all[:, n_all - min(WINDOW, past_len + T):]
                ctx = shared_context(cmp_full, slc_full) + (k_wb, v_wb, kw_pos)
        return x, jnp.stack(ret_states), cmp_new, slc_new, win_state

    Bp, Tp, _ = x_prompt.shape
    s0_p = jnp.zeros((n_a, Bp, RET_HEADS, RET_HEAD_DIM, RET_V_DIM), x_prompt.dtype)
    y_p, ret_p, cmp_p, slc_p, win_p = layer_stack(x_prompt, p_prompt, jnp.arange(Tp), s0_p, None)

    Bs, Ts, _ = x_sample.shape
    past_len = page_table.shape[1] * cache_cmp_kv.shape[1]
    cmp_past = cache_cmp_kv[page_table].reshape(Bs, past_len, 2, G, d)
    slc_past = cache_slc_kv[page_table].reshape(Bs, past_len, 2, G, d)
    y_s, ret_s, cmp_s, slc_s, win_s = layer_stack(x_sample, p_sample, past_len + jnp.arange(Ts), state_ret,
                                                  (cmp_past, slc_past, cache_win_kv))
    return (y_p, y_s, ret_p, ret_s, cmp_p, cmp_s, slc_p, slc_s, win_p, win_s)
```

```python
import functools

import numpy as np
import jax
import jax.numpy as jnp
from jax import lax
from jax.experimental import pallas as pl
from jax.experimental.pallas import tpu as pltpu

F32 = jnp.float32
BF16 = jnp.bfloat16

RET_HEAD_DIM = 256
RET_V_DIM = 2 * RET_HEAD_DIM
RET_CHUNK = 128
ROPE_BASE = 10000.0
NSA_HEAD_DIM = 128
NSA_KV_HEADS = 4
CMP_LEN = 32
CMP_STRIDE = 16
SLC_BLOCK = 64
SLC_SHIFT = 6
SLC_TOP = 16
SLC_LOCAL = 2
FORCE_BONUS = 1e4
WINDOW = 512
N_BRANCH = 3
MOE_GROUPS = 8
MOE_EXPERTS_PER_GROUP = 8
MOE_TOP = 2
EPS = 1e-6

LANES = 128
SUBLANES = 8
VMEM_LIMIT_BYTES = 56 * 1024 * 1024
MOE_ROWS = 256
NEG_INF = float("-inf")


def _tile(n, pref, align):
    best = None
    for t in range(align, min(n, pref) + 1, align):
        if n % t == 0:
            best = t
    return n if best is None else best


def _params(semantics):
    return pltpu.CompilerParams(dimension_semantics=semantics, vmem_limit_bytes=VMEM_LIMIT_BYTES)


def _rms_kernel(*refs, n_add, want_sum):
    adds, g_ref, outs = refs[:n_add], refs[n_add], refs[n_add + 1:]
    x = adds[0][...]
    for r in adds[1:]:
        x = x + r[...]
    y = (x * lax.rsqrt(jnp.mean(x * x, axis=-1, keepdims=True) + EPS)) * g_ref[...]
    if want_sum:
        outs[0][...] = x
        outs = outs[1:]
    for o in outs:
        o[...] = y.astype(o.dtype)


def _rms(addends, g, out_dtypes, want_sum=False, rows=None, tm=128):
    d = addends[0][0].shape[1]
    rows = addends[0][0].shape[0] if rows is None else rows
    tm = _tile(rows, tm, SUBLANES)
    for _, off in addends:
        assert off % tm == 0
    in_specs = [pl.BlockSpec((tm, d), functools.partial(lambda i, o: (i + o, 0), o=off // tm)) for _, off in addends]
    in_specs.append(pl.BlockSpec((1, d), lambda i: (0, 0)))
    dts = ([F32] if want_sum else []) + list(out_dtypes)
    outs = pl.pallas_call(
        functools.partial(_rms_kernel, n_add=len(addends), want_sum=want_sum),
        grid=(rows // tm,),
        in_specs=in_specs,
        out_specs=[pl.BlockSpec((tm, d), lambda i: (i, 0)) for _ in dts],
        out_shape=[jax.ShapeDtypeStruct((rows, d), dt) for dt in dts],
        compiler_params=_params(("parallel",)),
        name="rms",
    )(*[a for a, _ in addends], g.reshape(1, d).astype(F32))
    return outs


def _mm_kernel(x_ref, w_ref, *rest, n_extra, epilogue, nk):
    extras, o_ref, acc_ref = rest[:n_extra], rest[n_extra], rest[n_extra + 1]
    k = pl.program_id(2)

    @pl.when(k == 0)
    def _():
        acc_ref[...] = jnp.zeros_like(acc_ref)

    acc_ref[...] += jnp.dot(x_ref[...].astype(BF16), w_ref[...].astype(BF16), preferred_element_type=F32)

    @pl.when(k == nk - 1)
    def _():
        acc = acc_ref[...]
        if epilogue is None:
            o_ref[...] = acc.astype(o_ref.dtype)
        else:
            epilogue(acc, extras, o_ref)


def _mm(x, w, *, n_out=None, out_dtype=F32, tm=1024, tn=1024, tk=512, epilogue=None, extras=(),
        out_shape=None, out_spec=None, name="mm"):
    m, kdim = x.shape
    n = w.shape[1] if n_out is None else n_out
    if m <= 256:
        tk = 2048
    tm, tn, tk = _tile(m, tm, SUBLANES), _tile(n, tn, LANES), _tile(kdim, tk, LANES)
    nk = kdim // tk
    if out_shape is None:
        out_shape = jax.ShapeDtypeStruct((m, n), out_dtype)
        out_spec = pl.BlockSpec((tm, tn), lambda i, j, k: (i, j))
    return pl.pallas_call(
        functools.partial(_mm_kernel, n_extra=len(extras), epilogue=epilogue, nk=nk),
        grid=(m // tm, n // tn, nk),
        in_specs=[pl.BlockSpec((tm, tk), lambda i, j, k: (i, k)), pl.BlockSpec((tk, tn), lambda i, j, k: (k, j))]
        + [s for _, s in extras],
        out_specs=out_spec,
        out_shape=out_shape,
        scratch_shapes=[pltpu.VMEM((tm, tn), F32)],
        compiler_params=_params(("parallel", "parallel", "arbitrary")),
        name=name,
    )(x, w, *[a for a, _ in extras]), (tm, tn)


def _tile_spec(tm, tn):
    return pl.BlockSpec((tm, tn), lambda i, j, k: (i, j))


def _mm_resid(x, w, resid, name):
    m, n = resid.shape
    tm, tn = _tile(m, 1024, SUBLANES), _tile(n, 1024, LANES)

    def epi(acc, extras, o_ref):
        o_ref[...] = extras[0][...] + acc

    return _mm(x, w, tm=tm, tn=tn, epilogue=epi, extras=[(resid, _tile_spec(tm, tn))], name=name)[0]


def _mm_ple(h, w_gate, x, pu):
    m, n = x.shape
    tm, tn = _tile(m, 1024, SUBLANES), _tile(n, 1024, LANES)

    def epi(acc, extras, o_ref):
        o_ref[...] = extras[0][...] + extras[1][...] * jax.nn.sigmoid(acc)

    return _mm(h, w_gate, tm=tm, tn=tn, epilogue=epi,
               extras=[(x, _tile_spec(tm, tn)), (pu, _tile_spec(tm, tn))], name="ple_gate")[0]


def _group_rms(a):
    return a * lax.rsqrt(jnp.mean(a * a, axis=-1, keepdims=True) + EPS)


def _mm_kv(h, w_kv, g_k_slc, g_k_win):
    m = h.shape[0]
    gd = NSA_KV_HEADS * NSA_HEAD_DIM
    n = 2 * N_BRANCH * gd
    ones = jnp.ones((gd,), F32)
    gain = jnp.concatenate([ones, ones, jnp.tile(g_k_slc.astype(F32), NSA_KV_HEADS), ones,
                            jnp.tile(g_k_win.astype(F32), NSA_KV_HEADS), ones]).reshape(1, n)
    zeros = jnp.zeros((gd,), F32)
    flag = jnp.concatenate([zeros, zeros, ones, zeros, ones, zeros]).reshape(1, n)
    tm = _tile(m, 1024, SUBLANES)

    def epi(acc, extras, o_ref):
        parts = [_group_rms(acc[:, c * NSA_HEAD_DIM:(c + 1) * NSA_HEAD_DIM]) for c in range(NSA_KV_HEADS)]
        normed = jnp.concatenate(parts, axis=-1) * extras[0][...]
        o_ref[...] = jnp.where(extras[1][...] > 0.5, normed, acc)

    row = pl.BlockSpec((1, gd), lambda i, j, k: (0, j))
    return _mm(h, w_kv, tm=tm, tn=gd, epilogue=epi, extras=[(gain, row), (flag, row)], name="kv_proj")[0]


def _mm_q(h, w_in, g_q, b, t):
    m, d_model = h.shape
    g, d = NSA_KV_HEADS, NSA_HEAD_DIM
    hpg = d_model // d // g
    tn = hpg * d
    tm = _tile(t, 1024, SUBLANES)
    scale = d ** -0.5

    def epi(acc, extras, o_ref):
        for hh in range(hpg):
            o_ref[0, 0, hh] = _group_rms(acc[:, hh * d:(hh + 1) * d]) * extras[0][...] * scale

    gq = g_q.reshape(1, d).astype(F32)
    tiles_per_b = t // tm
    return _mm(h, w_in, n_out=g * tn, tm=tm, tn=tn, epilogue=epi,
               extras=[(gq, pl.BlockSpec((1, d), lambda i, j, k: (0, 0)))],
               out_shape=jax.ShapeDtypeStruct((b, g, hpg, t, d), F32),
               out_spec=pl.BlockSpec((1, 1, hpg, tm, d), lambda i, j, k: (i // tiles_per_b, j, 0, i % tiles_per_b, 0)),
               name="nsa_q")[0]


def _ret_kernel(*refs, c, cp, nc, has_s0):
    if has_s0:
        q_ref, k_ref, v_ref, g_ref, cos_ref, sin_ref, mask_ref, qd_ref, kd_ref, cd_ref, s0_ref, o_ref, so_ref, s_ref = refs
    else:
        q_ref, k_ref, v_ref, g_ref, cos_ref, sin_ref, mask_ref, qd_ref, kd_ref, cd_ref, o_ref, so_ref, s_ref = refs
    ci = pl.program_id(2)

    @pl.when(ci == 0)
    def _():
        if has_s0:
            s_ref[...] = s0_ref[0, 0]
        else:
            s_ref[...] = jnp.zeros_like(s_ref)

    def padded(a):
        if cp == c:
            return a
        return jnp.concatenate([a, jnp.zeros((cp - c, a.shape[1]), a.dtype)], axis=0)

    half = RET_HEAD_DIM // 2
    cos, sin = cos_ref[...], sin_ref[...]

    def rot(a):
        a1, a2 = a[:, :half], a[:, half:]
        return jnp.concatenate([a1 * cos - a2 * sin, a1 * sin + a2 * cos], axis=-1)

    q = rot(padded(q_ref[0]))
    k = rot(padded(k_ref[0])) * (RET_HEAD_DIM ** -0.5)
    v = padded(v_ref[0]).astype(BF16)
    s = s_ref[...]
    att = lax.dot_general(q.astype(BF16), k.astype(BF16), (((1,), (1,)), ((), ())),
                          preferred_element_type=F32) * mask_ref[0]
    o = (jnp.dot(att.astype(BF16), v, preferred_element_type=F32)
         + jnp.dot((q * qd_ref[0]).astype(BF16), s.astype(BF16), preferred_element_type=F32))
    kt = jnp.transpose(k * kd_ref[0]).astype(BF16)
    s_new = s * cd_ref[0] + jnp.dot(kt, v, preferred_element_type=F32)
    s_ref[...] = s_new

    @pl.when(ci == nc - 1)
    def _():
        so_ref[0, 0] = s_new

    o = _group_rms(o[:c])
    gate = g_ref[0]
    o_ref[0] = (gate * jax.nn.sigmoid(gate) * o).astype(o_ref.dtype)


def _retention(qkvg, pos, s0):
    b, t, width = qkvg.shape
    dk, dv = RET_HEAD_DIM, RET_V_DIM
    h = width // (2 * dk + 2 * dv)
    c = RET_CHUNK if t % RET_CHUNK == 0 else t
    nc = t // c
    cp = max(c, LANES)
    half = dk // 2
    lg = jnp.log1p(-(2.0 ** (-5.0 - jnp.arange(h, dtype=F32))))
    idx = jnp.arange(c, dtype=F32)
    diff = idx[:, None] - idx[None, :]
    mask = jnp.where(diff >= 0, jnp.exp(jnp.maximum(diff, 0.0)[None] * lg[:, None, None]), 0.0)
    q_dec = jnp.exp((idx + 1.0)[None, :] * lg[:, None])
    k_dec = jnp.exp((c - 1.0 - idx)[None, :] * lg[:, None])
    c_dec = jnp.exp(c * lg)
    mask = jnp.pad(mask, ((0, 0), (0, cp - c), (0, cp - c)))
    q_dec = jnp.pad(q_dec, ((0, 0), (0, cp - c)))[..., None]
    k_dec = jnp.pad(k_dec, ((0, 0), (0, cp - c)))[..., None]
    c_dec = c_dec.reshape(h, 1, 1)
    inv = ROPE_BASE ** (-jnp.arange(half, dtype=F32) / half)
    ang = pos.astype(F32)[:, None] * inv[None, :]
    cos = jnp.pad(jnp.cos(ang), ((0, nc * cp - t), (0, 0)))
    sin = jnp.pad(jnp.sin(ang), ((0, nc * cp - t), (0, 0)))

    vb = (2 * h * dk) // dv
    in_specs = [
        pl.BlockSpec((1, c, dk), lambda bi, hi, ci: (bi, ci, hi)),
        pl.BlockSpec((1, c, dk), lambda bi, hi, ci: (bi, ci, h + hi)),
        pl.BlockSpec((1, c, dv), lambda bi, hi, ci: (bi, ci, vb + hi)),
        pl.BlockSpec((1, c, dv), lambda bi, hi, ci: (bi, ci, vb + h + hi)),
        pl.BlockSpec((cp, half), lambda bi, hi, ci: (ci, 0)),
        pl.BlockSpec((cp, half), lambda bi, hi, ci: (ci, 0)),
        pl.BlockSpec((1, cp, cp), lambda bi, hi, ci: (hi, 0, 0)),
        pl.BlockSpec((1, cp, 1), lambda bi, hi, ci: (hi, 0, 0)),
        pl.BlockSpec((1, cp, 1), lambda bi, hi, ci: (hi, 0, 0)),
        pl.BlockSpec((1, 1, 1), lambda bi, hi, ci: (hi, 0, 0)),
    ]
    args = [qkvg, qkvg, qkvg, qkvg, cos, sin, mask, q_dec, k_dec, c_dec]
    if s0 is not None:
        in_specs.append(pl.BlockSpec((1, 1, dk, dv), lambda bi, hi, ci: (bi, hi, 0, 0)))
        args.append(s0)
    o, s_out = pl.pallas_call(
        functools.partial(_ret_kernel, c=c, cp=cp, nc=nc, has_s0=s0 is not None),
        grid=(b, h, nc),
        in_specs=in_specs,
        out_specs=[pl.BlockSpec((1, c, dv), lambda bi, hi, ci: (bi, ci, hi)),
                   pl.BlockSpec((1, 1, dk, dv), lambda bi, hi, ci: (bi, hi, 0, 0))],
        out_shape=[jax.ShapeDtypeStruct((b, t, h * dv), BF16), jax.ShapeDtypeStruct((b, h, dk, dv), F32)],
        scratch_shapes=[pltpu.VMEM((dk, dv), F32)],
        compiler_params=_params(("parallel", "parallel", "arbitrary")),
        name="retention",
    )(*args)
    return o, s_out


def _router_kernel(h_ref, w_ref, b_ref, ids_ref, gates_ref):
    logits = jnp.dot(h_ref[...], w_ref[...], precision=lax.Precision.HIGHEST, preferred_element_type=F32) + b_ref[...]
    lane = lax.broadcasted_iota(jnp.int32, logits.shape, 1)
    big = jnp.int32(LANES)
    ng, ne = MOE_GROUPS, MOE_EXPERTS_PER_GROUP
    gl = jnp.where(lane < ng, logits, NEG_INF)
    gmax = jnp.max(gl, axis=-1, keepdims=True)
    gsum = jnp.sum(jnp.exp(gl - gmax), axis=-1, keepdims=True)
    g_sel = jnp.min(jnp.where(gl == gmax, lane, big), axis=-1, keepdims=True)
    g_w = 1.0 / gsum
    lo = ng + g_sel * ne
    in_group = (lane >= lo) & (lane < lo + ne)
    el = jnp.where(in_group, logits, NEG_INF)
    emax = jnp.max(el, axis=-1, keepdims=True)
    ee = jnp.exp(el - emax)
    ep = ee / jnp.sum(ee, axis=-1, keepdims=True)
    ep = jnp.where(in_group, ep, -1.0)
    p1 = jnp.max(ep, axis=-1, keepdims=True)
    i1 = jnp.min(jnp.where(ep == p1, lane, big), axis=-1, keepdims=True)
    ep2 = jnp.where(lane == i1, -1.0, ep)
    p2 = jnp.max(ep2, axis=-1, keepdims=True)
    i2 = jnp.min(jnp.where(ep2 == p2, lane, big), axis=-1, keepdims=True)
    psum = p1 + p2
    ids_ref[...] = jnp.where(lane == 0, i1 - ng, jnp.where(lane == 1, i2 - ng, 0))
    gates_ref[...] = jnp.where(lane == 0, g_w * p1 / psum, jnp.where(lane == 1, g_w * p2 / psum, 0.0))


def _router(hf, w_rg, b_rg, w_re, b_re):
    n, d = hf.shape
    ng, ne = MOE_GROUPS, MOE_EXPERTS_PER_GROUP
    w = jnp.concatenate([w_rg.astype(F32), jnp.transpose(w_re.astype(F32), (1, 0, 2)).reshape(d, ng * ne)], axis=1)
    w = jnp.pad(w, ((0, 0), (0, LANES - w.shape[1])))
    bias = jnp.pad(jnp.concatenate([b_rg.astype(F32), b_re.astype(F32).reshape(-1)]), (0, LANES - ng - ng * ne)).reshape(1, LANES)
    tm = _tile(n, 256, SUBLANES)
    ids, gates = pl.pallas_call(
        _router_kernel,
        grid=(n // tm,),
        in_specs=[pl.BlockSpec((tm, d), lambda i: (i, 0)), pl.BlockSpec((d, LANES), lambda i: (0, 0)),
                  pl.BlockSpec((1, LANES), lambda i: (0, 0))],
        out_specs=[pl.BlockSpec((tm, LANES), lambda i: (i, 0)), pl.BlockSpec((tm, LANES), lambda i: (i, 0))],
        out_shape=[jax.ShapeDtypeStruct((n, LANES), jnp.int32), jax.ShapeDtypeStruct((n, LANES), F32)],
        compiler_params=_params(("parallel",)),
        name="moe_router",
    )(hf, w, bias)
    return ids[:, :MOE_TOP], gates[:, :MOE_TOP]


def _row_copy(src, s_row, dst, d_row, sem):
    return pltpu.make_async_copy(src.at[pl.ds(s_row, 1)], dst.at[pl.ds(d_row, 1)], sem)


def _moe_up_kernel(be_ref, nu_ref, tokc_ref, tokn_ref, x_hbm, w_ref, h_ref, xbuf, sem, *, rows, kc):
    b = pl.program_id(0)
    n_used = nu_ref[0]
    slot = lax.rem(b, 2)

    def gather(tok_ref, s):
        def body(r, carry):
            _row_copy(x_hbm, tok_ref[0, 0, r], xbuf.at[s], r, sem.at[s]).start()
            return carry
        lax.fori_loop(0, rows, body, 0)

    @pl.when(b == 0)
    def _():
        gather(tokc_ref, 0)

    @pl.when(b + 1 < n_used)
    def _():
        gather(tokn_ref, 1 - slot)

    @pl.when(b < n_used)
    def _():
        def body(r, carry):
            _row_copy(x_hbm, tokc_ref[0, 0, r], xbuf.at[slot], r, sem.at[slot]).wait()
            return carry
        lax.fori_loop(0, rows, body, 0)
        d = xbuf.shape[2]
        f = h_ref.shape[1]
        acc = jnp.zeros((rows, 2 * f), F32)
        for c0 in range(0, d, kc):
            acc = acc + jnp.dot(xbuf[slot, :, c0:c0 + kc].astype(BF16), w_ref[0, c0:c0 + kc, :].astype(BF16),
                                preferred_element_type=F32)
        a, g = acc[:, :f], acc[:, f:]
        h_ref[...] = (a * jax.nn.sigmoid(a) * g).astype(h_ref.dtype)

    @pl.when(b >= n_used)
    def _():
        h_ref[...] = jnp.zeros_like(h_ref)


def _moe_down_kernel(be_ref, nu_ref, nv_ref, dstc_ref, dstp_ref, h_ref, g_ref, w_ref, y_hbm, ybuf, sem, *, nb):
    b = pl.program_id(0)
    n_used = nu_ref[0]
    slot = lax.rem(b, 2)

    def scatter(dst_ref, s, count, wait):
        def body(r, carry):
            cp = _row_copy(ybuf.at[s], r, y_hbm, dst_ref[0, 0, r], sem.at[s])
            if wait:
                cp.wait()
            else:
                cp.start()
            return carry
        lax.fori_loop(0, count, body, 0)

    @pl.when(b < n_used)
    def _():
        y = jnp.dot(h_ref[...], w_ref[0].astype(BF16), preferred_element_type=F32) * g_ref[...]
        ybuf[slot] = y
        scatter(dstc_ref, slot, nv_ref[b], False)

    @pl.when((b >= 1) & (b - 1 < n_used))
    def _():
        scatter(dstp_ref, 1 - slot, nv_ref[jnp.maximum(b - 1, 0)], True)

    @pl.when((b == nb - 1) & (b < n_used))
    def _():
        scatter(dstc_ref, slot, nv_ref[b], True)


def _moe(hf, w_rg, b_rg, w_re, b_re, w_up, w_down):
    n, d = hf.shape
    e, _, f2 = w_up.shape
    f = f2 // 2
    rows = MOE_ROWS
    ids, gates = _router(hf, w_rg, b_rg, w_re, b_re)

    a = n * MOE_TOP
    nb = -(-a // rows) + e
    e_flat = ids.reshape(-1)
    order = jnp.argsort(e_flat)
    e_sorted = e_flat[order]
    counts = jnp.bincount(e_flat, length=e)
    starts = jnp.cumsum(counts) - counts
    blocks_per = (counts + rows - 1) // rows
    blk_end = jnp.cumsum(blocks_per)
    pad_starts = (blk_end - blocks_per) * rows
    dest = (pad_starts[e_sorted] + jnp.arange(a) - starts[e_sorted]).astype(jnp.int32)
    tok = (order // MOE_TOP).astype(jnp.int32)
    tok_buf = jnp.zeros((nb * rows,), jnp.int32).at[dest].set(tok)
    gate_buf = jnp.zeros((nb * rows,), F32).at[dest].set(gates.reshape(-1)[order])
    out_row = ((order % MOE_TOP) * n).astype(jnp.int32) + tok
    dst_buf = jnp.zeros((nb * rows,), jnp.int32).at[dest].set(out_row)
    n_valid = jnp.zeros((nb * rows,), jnp.int32).at[dest].set(1).reshape(nb, rows).sum(axis=1).astype(jnp.int32)
    n_used = blk_end[-1].astype(jnp.int32)
    blk = jnp.minimum(jnp.arange(nb), n_used - 1)
    block_expert = jnp.minimum(jnp.searchsorted(blk_end, blk, side="right"), e - 1).astype(jnp.int32)
    n_used_arr = n_used.reshape(1)
    tok3 = tok_buf.reshape(nb, 1, rows)
    dst3 = dst_buf.reshape(nb, 1, rows)

    smem_blk = functools.partial(pl.BlockSpec, (1, 1, rows), memory_space=pltpu.SMEM)
    h_mid = pl.pallas_call(
        functools.partial(_moe_up_kernel, rows=rows, kc=_tile(d, 512, LANES)),
        grid_spec=pltpu.PrefetchScalarGridSpec(
            num_scalar_prefetch=2,
            grid=(nb,),
            in_specs=[smem_blk(lambda b, be, nu: (b, 0, 0)),
                      smem_blk(lambda b, be, nu: (jnp.minimum(b + 1, nb - 1), 0, 0)),
                      pl.BlockSpec(memory_space=pl.ANY),
                      pl.BlockSpec((1, d, f2), lambda b, be, nu: (be[b], 0, 0))],
            out_specs=pl.BlockSpec((rows, f), lambda b, be, nu: (b, 0)),
            scratch_shapes=[pltpu.VMEM((2, rows, d), F32), pltpu.SemaphoreType.DMA((2,))]),
        out_shape=jax.ShapeDtypeStruct((nb * rows, f), BF16),
        compiler_params=_params(("arbitrary",)),
        name="moe_up",
    )(block_expert, n_used_arr, tok3, tok3, hf, w_up)

    y_tok = pl.pallas_call(
        functools.partial(_moe_down_kernel, nb=nb),
        grid_spec=pltpu.PrefetchScalarGridSpec(
            num_scalar_prefetch=3,
            grid=(nb,),
            in_specs=[smem_blk(lambda b, be, nu, nv: (b, 0, 0)),
                      smem_blk(lambda b, be, nu, nv: (jnp.maximum(b - 1, 0), 0, 0)),
                      pl.BlockSpec((rows, f), lambda b, be, nu, nv: (b, 0)),
                      pl.BlockSpec((rows, 1), lambda b, be, nu, nv: (b, 0)),
                      pl.BlockSpec((1, f, d), lambda b, be, nu, nv: (be[b], 0, 0))],
            out_specs=pl.BlockSpec(memory_space=pl.ANY),
            scratch_shapes=[pltpu.VMEM((2, rows, d), F32), pltpu.SemaphoreType.DMA((2,))]),
        out_shape=jax.ShapeDtypeStruct((MOE_TOP * n, d), F32),
        compiler_params=_params(("arbitrary",)),
        name="moe_down",
    )(block_expert, n_used_arr, n_valid, dst3, dst3, h_mid, gate_buf.reshape(nb * rows, 1), w_down)
    return y_tok


def _page_gather_kernel(pt_ref, cache_hbm, out_hbm, sem, *, n_pages, page):
    total = pt_ref.shape[0]

    def copy(i):
        bi = i // n_pages
        j = i - bi * n_pages
        return pltpu.make_async_copy(cache_hbm.at[pt_ref[i]], out_hbm.at[bi, pl.ds(j * page, page)], sem)

    def start(i, carry):
        copy(i).start()
        return carry

    def wait(i, carry):
        copy(i).wait()
        return carry

    lax.fori_loop(0, total, start, 0)
    lax.fori_loop(0, total, wait, 0)


def _page_gather(cache, page_table):
    n_pool, page = cache.shape[:2]
    width = int(np.prod(cache.shape[2:]))
    b, n_pages = page_table.shape
    return pl.pallas_call(
        functools.partial(_page_gather_kernel, n_pages=n_pages, page=page),
        in_specs=[pl.BlockSpec(memory_space=pltpu.SMEM), pl.BlockSpec(memory_space=pl.ANY)],
        out_specs=pl.BlockSpec(memory_space=pl.ANY),
        out_shape=jax.ShapeDtypeStruct((b, n_pages * page, width), cache.dtype),
        scratch_shapes=[pltpu.SemaphoreType.DMA(())],
        name="page_gather",
    )(page_table.reshape(-1).astype(jnp.int32), cache.reshape(n_pool, page, width))


def _cmp_post_kernel(u_ref, pe_ref, w2_ref, g_ref, o_ref, *, n_cmp, norm):
    u = u_ref[...]
    nch, hid2 = u.shape
    hid = hid2 // 2
    nxt = pltpu.roll(u[:, hid:], nch - 1, axis=0)
    x = (pe_ref[0:1, :] + u[:, :hid]) + nxt
    y = 0.5 * x * (1.0 + jnp.tanh(0.7978845608028654 * (x + 0.044715 * (x * x * x))))
    z = jnp.dot(y.astype(BF16), w2_ref[...].astype(BF16), preferred_element_type=F32)
    if norm:
        z = _group_rms(z) * g_ref[...]
    row = lax.broadcasted_iota(jnp.int32, z.shape, 0)
    o_ref[0] = jnp.where(row < n_cmp, z, 0.0)


def _compress(rows, pe, w1, w2, g_k):
    b, t, g, d = rows.shape
    r = CMP_LEN // CMP_STRIDE
    n_cmp = (t - CMP_LEN) // CMP_STRIDE + 1
    nch = n_cmp + r - 1
    assert r == 2
    hid = w1.shape[1]
    ch = rows[:, :nch * CMP_STRIDE].reshape(b, nch, CMP_STRIDE, g, d)
    ch = jnp.transpose(ch, (0, 3, 1, 2, 4)).reshape(b * g * nch, CMP_STRIDE * d).astype(BF16)
    w1r = w1.reshape(r, CMP_STRIDE * d, hid)
    wcat = jnp.concatenate([w1r[0], w1r[1]], axis=1)
    u = _mm(ch, wcat, name="cmp_hidden")[0]
    pe_rows = jnp.pad(pe.reshape(1, -1), ((0, SUBLANES - 1), (0, 0)))
    pe_hid = _mm(pe_rows, w1, name="cmp_pe")[0]
    gain = (jnp.ones((d,), F32) if g_k is None else g_k.astype(F32)).reshape(1, d)
    return pl.pallas_call(
        functools.partial(_cmp_post_kernel, n_cmp=n_cmp, norm=g_k is not None),
        grid=(b * g,),
        in_specs=[pl.BlockSpec((nch, r * hid), lambda i: (i, 0)), pl.BlockSpec((SUBLANES, hid), lambda i: (0, 0)),
                  pl.BlockSpec((hid, d), lambda i: (0, 0)), pl.BlockSpec((1, d), lambda i: (0, 0))],
        out_specs=pl.BlockSpec((1, nch, d), lambda i: (i, 0, 0)),
        out_shape=jax.ShapeDtypeStruct((b * g, nch, d), F32),
        compiler_params=_params(("parallel",)),
        name="cmp_post",
    )(u, pe_hid, w2, gain), n_cmp


def _slc_map(n_cmp, n_slc, rows, cols):
    a = SLC_BLOCK // CMP_STRIDE
    bb = CMP_LEN // CMP_STRIDE
    j = np.arange(n_slc)[:, None, None]
    i = j * a + np.arange(a)[None, :, None] + np.arange(bb)[None, None, :] - bb + 1
    i, jj = np.broadcast_arrays(i, j)
    ok = (i >= 0) & (i < n_cmp)
    m = np.zeros((rows, cols), np.float32)
    np.add.at(m, (i[ok], jj[ok]), 1.0)
    return jnp.asarray(m)


def _masked_softmax_rows(s):
    m = jnp.max(s, axis=-1, keepdims=True)
    e = jnp.exp(s - jnp.where(m > NEG_INF, m, 0.0))
    den = jnp.sum(e, axis=-1, keepdims=True)
    return e / jnp.where(den > 0, den, 1.0)


def _store_gated(o, gl_ref, prev_ref, o_ref, branch, hpg, tq):
    d = NSA_HEAD_DIM
    gate = jax.nn.sigmoid(gl_ref[0])
    for hh in range(hpg):
        c = hh * N_BRANCH + branch
        val = gate[:, c:c + 1] * o[hh * tq:(hh + 1) * tq]
        if prev_ref is not None:
            val = prev_ref[0, :, hh * d:(hh + 1) * d] + val
        o_ref[0, :, hh * d:(hh + 1) * d] = val.astype(o_ref.dtype)


def _nsa_cmp_kernel(q_ref, kc_ref, vc_ref, map_ref, gl_ref, o_ref, sel_ref, *, hpg, tq, pos0, n_cmp, n_slc, n_top):
    i = pl.program_id(2)
    rws = hpg * tq
    q = q_ref[0, 0].reshape(rws, NSA_HEAD_DIM).astype(BF16)
    kc = kc_ref[0].astype(BF16)
    s = lax.dot_general(q, kc, (((1,), (1,)), ((), ())), preferred_element_type=F32)
    ncp = s.shape[1]
    tok = lax.broadcasted_iota(jnp.int32, (rws, 1), 0) & (tq - 1)
    pos = pos0 + i * tq + tok
    cidx = lax.broadcasted_iota(jnp.int32, (1, ncp), 1)
    ok = (cidx * CMP_STRIDE + (CMP_LEN - 1) <= pos) & (cidx < n_cmp)
    p = _masked_softmax_rows(jnp.where(ok, s, NEG_INF))
    o = jnp.dot(p.astype(BF16), vc_ref[0].astype(BF16), preferred_element_type=F32)
    _store_gated(o, gl_ref, None, o_ref, 0, hpg, tq)

    psum = jnp.sum(p.reshape(hpg, tq, ncp), axis=0)
    imp = jnp.dot(psum, map_ref[...], precision=lax.Precision.HIGHEST, preferred_element_type=F32)
    nsp = imp.shape[1]
    posq = pos0 + i * tq + lax.broadcasted_iota(jnp.int32, (tq, 1), 0)
    blk = lax.broadcasted_iota(jnp.int32, (tq, nsp), 1)
    back = (posq >> SLC_SHIFT) - blk
    real = blk < n_slc
    valid = (blk * SLC_BLOCK <= posq) & real
    forced = (blk == 0) | ((back >= 0) & (back < SLC_LOCAL))
    score = jnp.where(valid, imp + jnp.where(forced, FORCE_BONUS, 0.0), NEG_INF)
    avail = real
    sel = jnp.zeros((tq, nsp), F32)
    big = jnp.int32(nsp)
    for _ in range(n_top):
        sc = jnp.where(avail, score, NEG_INF)
        mx = jnp.max(sc, axis=-1, keepdims=True)
        first = jnp.min(jnp.where((sc == mx) & avail, blk, big), axis=-1, keepdims=True)
        pick = blk == first
        sel = jnp.where(pick, 1.0, sel)
        avail = avail & jnp.logical_not(pick)
    sel_ref[0, 0] = sel


def _nsa_cmp(qn, kc, vc, glog, pos0, n_cmp, n_slc):
    b, g, hpg, t, d = qn.shape
    ncp = kc.shape[1]
    nsp = -(-n_slc // LANES) * LANES
    tq = _tile(t, 128, SUBLANES)
    smap = _slc_map(n_cmp, n_slc, ncp, nsp)
    n_top = min(SLC_TOP, n_slc)
    return pl.pallas_call(
        functools.partial(_nsa_cmp_kernel, hpg=hpg, tq=tq, pos0=pos0, n_cmp=n_cmp, n_slc=n_slc, n_top=n_top),
        grid=(b, g, t // tq),
        in_specs=[pl.BlockSpec((1, 1, hpg, tq, d), lambda bi, gi, i: (bi, gi, 0, i, 0)),
                  pl.BlockSpec((1, ncp, d), lambda bi, gi, i: (bi * g + gi, 0, 0)),
                  pl.BlockSpec((1, ncp, d), lambda bi, gi, i: (bi * g + gi, 0, 0)),
                  pl.BlockSpec((ncp, nsp), lambda bi, gi, i: (0, 0)),
                  pl.BlockSpec((1, tq, LANES), lambda bi, gi, i: (bi, i, gi))],
        out_specs=[pl.BlockSpec((1, tq, hpg * d), lambda bi, gi, i: (bi, i, gi)),
                   pl.BlockSpec((1, 1, tq, nsp), lambda bi, gi, i: (bi, gi, i, 0))],
        out_shape=[jax.ShapeDtypeStruct((b, t, g * hpg * d), F32), jax.ShapeDtypeStruct((b, g, t, nsp), F32)],
        compiler_params=_params(("parallel", "parallel", "parallel")),
        name="nsa_cmp",
    )(qn, kc, vc, smap, glog)


def _nsa_attn_kernel(*refs, hpg, tq, tk, n_kt, qpos0, kpos0, window, use_sel, has_tail, tail_pos0, branch):
    refs = list(refs)
    q_ref, k_ref, v_ref = refs[:3]
    refs = refs[3:]
    kt_ref = vt_ref = sel_ref = None
    if has_tail:
        kt_ref, vt_ref = refs[:2]
        refs = refs[2:]
    if use_sel:
        sel_ref = refs[0]
        refs = refs[1:]
    gl_ref, prev_ref, o_ref, m_ref, l_ref, acc_ref = refs
    i = pl.program_id(2)
    d = NSA_HEAD_DIM
    rws = hpg * tq
    q = q_ref[0, 0].reshape(rws, d).astype(BF16)
    posq = qpos0 + i * tq + lax.broadcasted_iota(jnp.int32, (tq, 1), 0)
    sel = sel_ref[0, 0].astype(BF16) if use_sel else None
    m_ref[...] = jnp.full_like(m_ref, NEG_INF)
    l_ref[...] = jnp.zeros_like(l_ref)
    acc_ref[...] = jnp.zeros_like(acc_ref)

    def step(k, v, key0, pos_base):
        n = k.shape[0]
        s = lax.dot_general(q, k.astype(BF16), (((1,), (1,)), ((), ())), preferred_element_type=F32)
        kidx = key0 + lax.broadcasted_iota(jnp.int32, (1, n), 1)
        kpos = pos_base + kidx
        ok = kpos <= posq
        if window is not None:
            ok = ok & (posq - kpos < window)
        if use_sel:
            nsp = sel.shape[1]
            kblk = (key0 + lax.broadcasted_iota(jnp.int32, (nsp, n), 1)) >> SLC_SHIFT
            expand = (kblk == lax.broadcasted_iota(jnp.int32, (nsp, n), 0)).astype(BF16)
            ok = ok & (jnp.dot(sel, expand, preferred_element_type=F32) > 0.5)
        s = jnp.where(ok[None], s.reshape(hpg, tq, n), NEG_INF).reshape(rws, n)
        m_old = m_ref[...]
        m_new = jnp.maximum(m_old, jnp.max(s, axis=-1, keepdims=True))
        m_safe = jnp.where(m_new > NEG_INF, m_new, 0.0)
        p = jnp.exp(s - m_safe)
        alpha = jnp.exp(m_old - m_safe)
        l_ref[...] = alpha * l_ref[...] + jnp.sum(p, axis=-1, keepdims=True)
        acc_ref[...] = alpha * acc_ref[...] + jnp.dot(p.astype(BF16), v.astype(BF16), preferred_element_type=F32)
        m_ref[...] = m_new

    q_lo = qpos0 + i * tq
    q_hi = q_lo + tq - 1
    hi = jnp.clip((q_hi - kpos0) // tk + 1, 0, n_kt)
    if window is None:
        lo = 0
    else:
        lo = jnp.clip((q_lo - (window - 1) - kpos0) // tk, 0, n_kt)

    def body(kt, carry):
        off = pl.multiple_of(kt * tk, tk)
        step(k_ref[0, pl.ds(off, tk), :], v_ref[0, pl.ds(off, tk), :], kt * tk, kpos0)
        return carry

    lax.fori_loop(lo, hi, body, 0)
    if has_tail:
        step(kt_ref[0], vt_ref[0], tail_pos0 - kpos0, kpos0)
    l = l_ref[...]
    o = acc_ref[...] / jnp.where(l > 0, l, 1.0)
    _store_gated(o, gl_ref, prev_ref, o_ref, branch, hpg, tq)


def _nsa_attn(qn, kv, glog, prev, *, qpos0, kpos0, branch, out_dtype, window=None, sel=None, tail=None, tail_pos0=0):
    b, g, hpg, t, d = qn.shape
    tk_total = kv.shape[1]
    tq = _tile(t, 128, SUBLANES)
    tk = _tile(tk_total, 512, LANES)
    n_kt = tk_total // tk
    in_specs = [pl.BlockSpec((1, 1, hpg, tq, d), lambda bi, gi, i: (bi, gi, 0, i, 0)),
                pl.BlockSpec((1, tk_total, d), lambda bi, gi, i: (bi, 0, gi)),
                pl.BlockSpec((1, tk_total, d), lambda bi, gi, i: (bi, 0, g + gi))]
    args = [qn, kv, kv]
    if tail is not None:
        nt = tail.shape[1]
        in_specs += [pl.BlockSpec((1, nt, d), lambda bi, gi, i: (bi, 0, gi)),
                     pl.BlockSpec((1, nt, d), lambda bi, gi, i: (bi, 0, g + gi))]
        args += [tail, tail]
    if sel is not None:
        nsp = sel.shape[-1]
        in_specs.append(pl.BlockSpec((1, 1, tq, nsp), lambda bi, gi, i: (bi, gi, i, 0)))
        args.append(sel)
    in_specs += [pl.BlockSpec((1, tq, LANES), lambda bi, gi, i: (bi, i, gi)),
                 pl.BlockSpec((1, tq, hpg * d), lambda bi, gi, i: (bi, i, gi))]
    args += [glog, prev]
    rws = hpg * tq
    return pl.pallas_call(
        functools.partial(_nsa_attn_kernel, hpg=hpg, tq=tq, tk=tk, n_kt=n_kt, qpos0=qpos0, kpos0=kpos0,
                          window=window, use_sel=sel is not None, has_tail=tail is not None,
                          tail_pos0=tail_pos0, branch=branch),
        grid=(b, g, t // tq),
        in_specs=in_specs,
        out_specs=pl.BlockSpec((1, tq, hpg * d), lambda bi, gi, i: (bi, i, gi)),
        out_shape=jax.ShapeDtypeStruct((b, t, g * hpg * d), out_dtype),
        scratch_shapes=[pltpu.VMEM((rws, 1), F32), pltpu.VMEM((rws, 1), F32), pltpu.VMEM((rws, d), F32)],
        compiler_params=_params(("parallel", "parallel", "parallel")),
        name="nsa_attn_%d" % branch,
    )(*args)


def _gate_weight(w_in, d_model):
    g = NSA_KV_HEADS
    hpg = d_model // NSA_HEAD_DIM // g
    wg = w_in[:, d_model:].reshape(d_model, g, hpg * N_BRANCH)
    wg = jnp.pad(wg, ((0, 0), (0, 0), (0, LANES - hpg * N_BRANCH)))
    return wg.reshape(d_model, g * LANES)


def kernel(x_prompt, x_sample, state_ret, cache_cmp_kv, cache_slc_kv, cache_win_kv, page_table, p_prompt, p_sample, g_mix, g_ffn, w_ret_in, w_ret_out, w_nsa_in, g_nsa_q, w_nsa_out, g_kv, w_kv, g_k_cmp, g_k_slc, g_k_win, pe_cmp_k, w_cmp_k1, w_cmp_k2, pe_cmp_v, w_cmp_v1, w_cmp_v2, w_rg, b_rg, w_re, b_re, w_moe_up, w_moe_down, w_ple_up, g_ple, w_ple_gate):
    depth = g_mix.shape[0]
    n_a = w_ret_in.shape[0]
    g, d = NSA_KV_HEADS, NSA_HEAD_DIM
    gd = g * d
    d_model = x_prompt.shape[-1]
    page = cache_cmp_kv.shape[1]
    past_len = page_table.shape[1] * page

    groups = [
        dict(x=x_prompt.reshape(-1, d_model), p=p_prompt, b=x_prompt.shape[0], t=x_prompt.shape[1], pos0=0, s0=None),
        dict(x=x_sample.reshape(-1, d_model), p=p_sample, b=x_sample.shape[0], t=x_sample.shape[1], pos0=past_len, s0=state_ret),
    ]
    n_rows = [gr["x"].shape[0] for gr in groups]
    n_tok = sum(n_rows)
    offs = [0, n_rows[0]]
    for gr in groups:
        gr["ret"] = []

    for i in range(depth):
        for gr in groups:
            b, t = gr["b"], gr["t"]
            h = _rms([(gr["x"], 0)], g_mix[i], [BF16])[0]
            if i < n_a:
                qkvg = _mm(h, w_ret_in[i], tm=2048, name="ret_in")[0]
                pos = gr["pos0"] + jnp.arange(t)
                s0 = None if gr["s0"] is None else gr["s0"][i]
                o, s_new = _retention(qkvg.reshape(b, t, -1), pos, s0)
                gr["ret"].append(s_new)
                gr["x"] = _mm_resid(o.reshape(b * t, -1), w_ret_out[i], gr["x"], "ret_out")
            else:
                j = i - n_a
                qn = _mm_q(h, w_nsa_in[j], g_nsa_q[j], b, t)
                glog = _mm(h, _gate_weight(w_nsa_in[j], d_model), name="nsa_gate")[0].reshape(b, t, g * LANES)
                ctx = gr["ctx"]
                o1, sel = _nsa_cmp(qn, ctx["k_c"], ctx["v_c"], glog, gr["pos0"], ctx["n_cmp"], ctx["n_slc"])
                o2 = _nsa_attn(qn, ctx["slc"], glog, o1, qpos0=gr["pos0"], kpos0=0, branch=1, out_dtype=F32,
                               sel=sel, tail=ctx["slc_tail"], tail_pos0=ctx["slc_tail_pos0"])
                o3 = _nsa_attn(qn, ctx["win"], glog, o2, qpos0=gr["pos0"], kpos0=ctx["win_pos0"], branch=2,
                               out_dtype=BF16, window=WINDOW)
                gr["x"] = _mm_resid(o3.reshape(b * t, -1), w_nsa_out[j], gr["x"], "nsa_out")

        hf = jnp.concatenate([_rms([(gr["x"], 0)], g_ffn[i], [F32])[0] for gr in groups], axis=0)
        y_tok = _moe(hf, w_rg[i], b_rg[i], w_re[i], b_re[i], w_moe_up[i], w_moe_down[i])

        for gi, gr in enumerate(groups):
            rows = n_rows[gi]
            x_new, hp = _rms([(gr["x"], 0), (y_tok, offs[gi]), (y_tok, n_tok + offs[gi])], g_ple[i], [BF16],
                             want_sum=True, rows=rows, tm=64)
            pu = _mm(gr["p"][i].reshape(rows, -1), w_ple_up[i], name="ple_up")[0]
            gr["x"] = _mm_ple(hp, w_ple_gate[i], x_new, pu)

        if i == n_a - 1:
            for gi, gr in enumerate(groups):
                b, t = gr["b"], gr["t"]
                hk = _rms([(gr["x"], 0)], g_kv, [BF16])[0]
                kv = _mm_kv(hk, w_kv, g_k_slc, g_k_win).reshape(b, t, 2 * N_BRANCH * gd)
                cmp_new, slc_new, win_new = kv[..., :2 * gd], kv[..., 2 * gd:4 * gd], kv[..., 4 * gd:]
                gr["cmp_new"], gr["slc_new"] = cmp_new, slc_new
                if gi == 0:
                    cmp_full, slc_full, slc_tail, tail_pos0 = cmp_new, slc_new, None, 0
                    win_keys, win_pos0 = win_new, 0
                    gr["win_state"] = win_new[:, t - min(WINDOW, t):]
                    n_keys = t
                else:
                    cmp_full = _page_gather(cache_cmp_kv, page_table)
                    slc_full = _page_gather(cache_slc_kv, page_table)
                    slc_tail = jnp.pad(slc_new, ((0, 0), (0, LANES - t), (0, 0)))
                    tail_pos0 = past_len
                    w_buf = cache_win_kv.shape[1]
                    win_all = jnp.concatenate([cache_win_kv.reshape(b, w_buf, 2 * gd), win_new], axis=1)
                    n_all = w_buf + t
                    gr["win_state"] = win_all[:, n_all - min(WINDOW, past_len + t):]
                    win_keys = jnp.pad(win_all, ((0, 0), (0, -n_all % LANES), (0, 0)))
                    win_pos0 = past_len - w_buf
                    n_keys = past_len + t
                cmp_rows = cmp_full.reshape(b, cmp_full.shape[1], 2, g, d) if gi == 0 else None
                if gi == 0:
                    k_rows, v_rows = cmp_rows[:, :, 0], cmp_rows[:, :, 1]
                else:
                    assert (n_keys - CMP_LEN) // CMP_STRIDE + 1 == (past_len - CMP_LEN) // CMP_STRIDE + 1
                    cr = cmp_full.reshape(b, past_len, 2, g, d)
                    k_rows, v_rows = cr[:, :, 0], cr[:, :, 1]
                k_c, n_cmp = _compress(k_rows, pe_cmp_k, w_cmp_k1, w_cmp_k2, g_k_cmp)
                v_c, _ = _compress(v_rows, pe_cmp_v, w_cmp_v1, w_cmp_v2, None)
                gr["ctx"] = dict(k_c=k_c, v_c=v_c, n_cmp=n_cmp, n_slc=-(-n_keys // SLC_BLOCK), slc=slc_full,
                                 slc_tail=slc_tail, slc_tail_pos0=tail_pos0, win=win_keys, win_pos0=win_pos0)

    outs = []
    for gr in groups:
        outs.append(gr["x"].reshape(gr["b"], gr["t"], d_model))
    rets = [jnp.stack(gr["ret"]) for gr in groups]
    kvs = []
    for name in ("cmp_new", "slc_new", "win_state"):
        for gr in groups:
            a = gr[name]
            kvs.append(a.reshape(a.shape[0], a.shape[1], 2, g, d))
    return (outs[0], outs[1], rets[0], rets[1], kvs[0], kvs[1], kvs[2], kvs[3], kvs[4], kvs[5])
```

```python
import functools

import numpy as np
import jax
import jax.numpy as jnp
from jax import lax
from jax.experimental import pallas as pl
from jax.experimental.pallas import tpu as pltpu

F32 = jnp.float32
BF16 = jnp.bfloat16

RET_HEAD_DIM = 256
RET_V_DIM = 2 * RET_HEAD_DIM
RET_CHUNK = 128
ROPE_BASE = 10000.0
NSA_HEAD_DIM = 128
NSA_KV_HEADS = 4
NSA_SCALE = NSA_HEAD_DIM ** -0.5
CMP_LEN = 32
CMP_STRIDE = 16
SLC_BLOCK = 64
SLC_SHIFT = 6
SLC_TOP = 16
SLC_LOCAL = 2
FORCE_BONUS = 1e4
WINDOW = 512
N_BRANCH = 3
MOE_GROUPS = 8
MOE_EXPERTS_PER_GROUP = 8
MOE_TOP = 2
EPS = 1e-6

LANES = 128
SUBLANES = 8
VMEM_LIMIT_BYTES = 56 * 1024 * 1024
MOE_ROWS = 256
NEG_INF = float("-inf")


def _tile(n, pref, align):
    best = None
    for t in range(align, min(n, pref) + 1, align):
        if n % t == 0:
            best = t
    return n if best is None else best


def _params(semantics):
    return pltpu.CompilerParams(dimension_semantics=semantics, vmem_limit_bytes=VMEM_LIMIT_BYTES)


def _rms_kernel(*refs, n_add, want_sum):
    adds, g_ref, outs = refs[:n_add], refs[n_add], refs[n_add + 1:]
    x = adds[0][...]
    for r in adds[1:]:
        x = x + r[...]
    y = (x * lax.rsqrt(jnp.mean(x * x, axis=-1, keepdims=True) + EPS)) * g_ref[...]
    if want_sum:
        outs[0][...] = x
        outs = outs[1:]
    for o in outs:
        o[...] = y.astype(o.dtype)


def _rms(addends, g, out_dtypes, want_sum=False, rows=None, tm=128):
    d = addends[0][0].shape[1]
    rows = addends[0][0].shape[0] if rows is None else rows
    tm = _tile(rows, tm, SUBLANES)
    for _, off in addends:
        assert off % tm == 0
    in_specs = [pl.BlockSpec((tm, d), functools.partial(lambda i, o: (i + o, 0), o=off // tm)) for _, off in addends]
    in_specs.append(pl.BlockSpec((1, d), lambda i: (0, 0)))
    dts = ([F32] if want_sum else []) + list(out_dtypes)
    outs = pl.pallas_call(
        functools.partial(_rms_kernel, n_add=len(addends), want_sum=want_sum),
        grid=(rows // tm,),
        in_specs=in_specs,
        out_specs=[pl.BlockSpec((tm, d), lambda i: (i, 0)) for _ in dts],
        out_shape=[jax.ShapeDtypeStruct((rows, d), dt) for dt in dts],
        compiler_params=_params(("parallel",)),
        name="rms",
    )(*[a for a, _ in addends], g.reshape(1, d).astype(F32))
    return outs


def _mm_kernel(x_ref, w_ref, *rest, n_extra, epilogue, nk):
    extras, o_ref, acc_ref = rest[:n_extra], rest[n_extra], rest[n_extra + 1]
    k = pl.program_id(2)

    @pl.when(k == 0)
    def _():
        acc_ref[...] = jnp.zeros_like(acc_ref)

    acc_ref[...] += jnp.dot(x_ref[...].astype(BF16), w_ref[...].astype(BF16), preferred_element_type=F32)

    @pl.when(k == nk - 1)
    def _():
        acc = acc_ref[...]
        if epilogue is None:
            o_ref[...] = acc.astype(o_ref.dtype)
        else:
            epilogue(acc, extras, o_ref)


def _mm(x, w, *, n_out=None, out_dtype=F32, tm=1024, tn=1024, tk=512, epilogue=None, extras=(),
        out_shape=None, out_spec=None, name="mm"):
    m, kdim = x.shape
    n = w.shape[1] if n_out is None else n_out
    if m <= 256:
        tk = 2048
    tm, tn, tk = _tile(m, tm, SUBLANES), _tile(n, tn, LANES), _tile(kdim, tk, LANES)
    nk = kdim // tk
    if out_shape is None:
        out_shape = jax.ShapeDtypeStruct((m, n), out_dtype)
        out_spec = pl.BlockSpec((tm, tn), lambda i, j, k: (i, j))
    return pl.pallas_call(
        functools.partial(_mm_kernel, n_extra=len(extras), epilogue=epilogue, nk=nk),
        grid=(m // tm, n // tn, nk),
        in_specs=[pl.BlockSpec((tm, tk), lambda i, j, k: (i, k)), pl.BlockSpec((tk, tn), lambda i, j, k: (k, j))]
        + [s for _, s in extras],
        out_specs=out_spec,
        out_shape=out_shape,
        scratch_shapes=[pltpu.VMEM((tm, tn), F32)],
        compiler_params=_params(("parallel", "parallel", "arbitrary")),
        name=name,
    )(x, w, *[a for a, _ in extras]), (tm, tn)


def _tile_spec(tm, tn):
    return pl.BlockSpec((tm, tn), lambda i, j, k: (i, j))


def _mm_resid(x, w, resid, name):
    m, n = resid.shape
    tm, tn = _tile(m, 1024, SUBLANES), _tile(n, 1024, LANES)

    def epi(acc, extras, o_ref):
        o_ref[...] = extras[0][...] + acc

    return _mm(x, w, tm=tm, tn=tn, epilogue=epi, extras=[(resid, _tile_spec(tm, tn))], name=name)[0]


def _mm_ple(h, w_gate, x, pu):
    m, n = x.shape
    tm, tn = _tile(m, 1024, SUBLANES), _tile(n, 1024, LANES)

    def epi(acc, extras, o_ref):
        o_ref[...] = extras[0][...] + extras[1][...] * jax.nn.sigmoid(acc)

    return _mm(h, w_gate, tm=tm, tn=tn, epilogue=epi,
               extras=[(x, _tile_spec(tm, tn)), (pu, _tile_spec(tm, tn))], name="ple_gate")[0]


def _group_rms(a):
    return a * lax.rsqrt(jnp.mean(a * a, axis=-1, keepdims=True) + EPS)


def _mm_kv(h, w_kv, g_k_slc, g_k_win):
    m = h.shape[0]
    gd = NSA_KV_HEADS * NSA_HEAD_DIM
    n = 2 * N_BRANCH * gd
    ones = jnp.ones((gd,), F32)
    gain = jnp.concatenate([ones, ones, jnp.tile(g_k_slc.astype(F32), NSA_KV_HEADS), ones,
                            jnp.tile(g_k_win.astype(F32), NSA_KV_HEADS), ones]).reshape(1, n)
    zeros = jnp.zeros((gd,), F32)
    flag = jnp.concatenate([zeros, zeros, ones, zeros, ones, zeros]).reshape(1, n)
    tm = _tile(m, 1024, SUBLANES)

    def epi(acc, extras, o_ref):
        parts = [_group_rms(acc[:, c * NSA_HEAD_DIM:(c + 1) * NSA_HEAD_DIM]) for c in range(NSA_KV_HEADS)]
        normed = jnp.concatenate(parts, axis=-1) * extras[0][...]
        o_ref[...] = jnp.where(extras[1][...] > 0.5, normed, acc)

    row = pl.BlockSpec((1, gd), lambda i, j, k: (0, j))
    return _mm(h, w_kv, tm=tm, tn=gd, epilogue=epi, extras=[(gain, row), (flag, row)], name="kv_proj")[0]


def _mm_q(h, w_in, g_q, b, t):
    m, d_model = h.shape
    g, d = NSA_KV_HEADS, NSA_HEAD_DIM
    hpg = d_model // d // g
    tn = hpg * d
    tm = _tile(t, 1024, SUBLANES)

    def epi(acc, extras, o_ref):
        for hh in range(hpg):
            o_ref[0, 0, hh] = _group_rms(acc[:, hh * d:(hh + 1) * d]) * extras[0][...]

    gq = g_q.reshape(1, d).astype(F32)
    tiles_per_b = t // tm
    return _mm(h, w_in, n_out=g * tn, tm=tm, tn=tn, epilogue=epi,
               extras=[(gq, pl.BlockSpec((1, d), lambda i, j, k: (0, 0)))],
               out_shape=jax.ShapeDtypeStruct((b, g, hpg, t, d), F32),
               out_spec=pl.BlockSpec((1, 1, hpg, tm, d), lambda i, j, k: (i // tiles_per_b, j, 0, i % tiles_per_b, 0)),
               name="nsa_q")[0]


def _ret_kernel(*refs, c, cp, nc, has_s0):
    if has_s0:
        q_ref, k_ref, v_ref, g_ref, cos_ref, sin_ref, mask_ref, qd_ref, kd_ref, cd_ref, s0_ref, o_ref, so_ref, s_ref = refs
    else:
        q_ref, k_ref, v_ref, g_ref, cos_ref, sin_ref, mask_ref, qd_ref, kd_ref, cd_ref, o_ref, so_ref, s_ref = refs
    ci = pl.program_id(2)

    @pl.when(ci == 0)
    def _():
        if has_s0:
            s_ref[...] = s0_ref[0, 0]
        else:
            s_ref[...] = jnp.zeros_like(s_ref)

    def padded(a):
        if cp == c:
            return a
        return jnp.concatenate([a, jnp.zeros((cp - c, a.shape[1]), a.dtype)], axis=0)

    half = RET_HEAD_DIM // 2
    cos, sin = cos_ref[...], sin_ref[...]

    def rot(a):
        a1, a2 = a[:, :half], a[:, half:]
        return jnp.concatenate([a1 * cos - a2 * sin, a1 * sin + a2 * cos], axis=-1)

    q = rot(padded(q_ref[0]))
    k = rot(padded(k_ref[0])) * (RET_HEAD_DIM ** -0.5)
    v = padded(v_ref[0]).astype(BF16)
    s = s_ref[...]
    att = lax.dot_general(q.astype(BF16), k.astype(BF16), (((1,), (1,)), ((), ())),
                          preferred_element_type=F32) * mask_ref[0]
    o = (jnp.dot(att.astype(BF16), v, preferred_element_type=F32)
         + jnp.dot((q * qd_ref[0]).astype(BF16), s.astype(BF16), preferred_element_type=F32))
    kt = jnp.transpose(k * kd_ref[0]).astype(BF16)
    s_new = s * cd_ref[0] + jnp.dot(kt, v, preferred_element_type=F32)
    s_ref[...] = s_new

    @pl.when(ci == nc - 1)
    def _():
        so_ref[0, 0] = s_new

    o = _group_rms(o[:c])
    gate = g_ref[0]
    o_ref[0] = (gate * jax.nn.sigmoid(gate) * o).astype(o_ref.dtype)


def _retention(qkvg, pos, s0):
    b, t, width = qkvg.shape
    dk, dv = RET_HEAD_DIM, RET_V_DIM
    h = width // (2 * dk + 2 * dv)
    c = RET_CHUNK if t % RET_CHUNK == 0 else t
    nc = t // c
    cp = max(c, LANES)
    half = dk // 2
    lg = jnp.log1p(-(2.0 ** (-5.0 - jnp.arange(h, dtype=F32))))
    idx = jnp.arange(c, dtype=F32)
    diff = idx[:, None] - idx[None, :]
    mask = jnp.where(diff >= 0, jnp.exp(jnp.maximum(diff, 0.0)[None] * lg[:, None, None]), 0.0)
    q_dec = jnp.exp((idx + 1.0)[None, :] * lg[:, None])
    k_dec = jnp.exp((c - 1.0 - idx)[None, :] * lg[:, None])
    c_dec = jnp.exp(c * lg)
    mask = jnp.pad(mask, ((0, 0), (0, cp - c), (0, cp - c)))
    q_dec = jnp.pad(q_dec, ((0, 0), (0, cp - c)))[..., None]
    k_dec = jnp.pad(k_dec, ((0, 0), (0, cp - c)))[..., None]
    c_dec = c_dec.reshape(h, 1, 1)
    inv = ROPE_BASE ** (-jnp.arange(half, dtype=F32) / half)
    ang = pos.astype(F32)[:, None] * inv[None, :]
    cos = jnp.pad(jnp.cos(ang), ((0, nc * cp - t), (0, 0)))
    sin = jnp.pad(jnp.sin(ang), ((0, nc * cp - t), (0, 0)))

    vb = (2 * h * dk) // dv
    in_specs = [
        pl.BlockSpec((1, c, dk), lambda bi, hi, ci: (bi, ci, hi)),
        pl.BlockSpec((1, c, dk), lambda bi, hi, ci: (bi, ci, h + hi)),
        pl.BlockSpec((1, c, dv), lambda bi, hi, ci: (bi, ci, vb + hi)),
        pl.BlockSpec((1, c, dv), lambda bi, hi, ci: (bi, ci, vb + h + hi)),
        pl.BlockSpec((cp, half), lambda bi, hi, ci: (ci, 0)),
        pl.BlockSpec((cp, half), lambda bi, hi, ci: (ci, 0)),
        pl.BlockSpec((1, cp, cp), lambda bi, hi, ci: (hi, 0, 0)),
        pl.BlockSpec((1, cp, 1), lambda bi, hi, ci: (hi, 0, 0)),
        pl.BlockSpec((1, cp, 1), lambda bi, hi, ci: (hi, 0, 0)),
        pl.BlockSpec((1, 1, 1), lambda bi, hi, ci: (hi, 0, 0)),
    ]
    args = [qkvg, qkvg, qkvg, qkvg, cos, sin, mask, q_dec, k_dec, c_dec]
    if s0 is not None:
        in_specs.append(pl.BlockSpec((1, 1, dk, dv), lambda bi, hi, ci: (bi, hi, 0, 0)))
        args.append(s0)
    o, s_out = pl.pallas_call(
        functools.partial(_ret_kernel, c=c, cp=cp, nc=nc, has_s0=s0 is not None),
        grid=(b, h, nc),
        in_specs=in_specs,
        out_specs=[pl.BlockSpec((1, c, dv), lambda bi, hi, ci: (bi, ci, hi)),
                   pl.BlockSpec((1, 1, dk, dv), lambda bi, hi, ci: (bi, hi, 0, 0))],
        out_shape=[jax.ShapeDtypeStruct((b, t, h * dv), BF16), jax.ShapeDtypeStruct((b, h, dk, dv), F32)],
        scratch_shapes=[pltpu.VMEM((dk, dv), F32)],
        compiler_params=_params(("parallel", "parallel", "arbitrary")),
        name="retention",
    )(*args)
    return o, s_out


def _router_kernel(h_ref, w_ref, b_ref, ids_ref, gates_ref):
    logits = jnp.dot(h_ref[...].astype(BF16), w_ref[...].astype(BF16), preferred_element_type=F32) + b_ref[...]
    lane = lax.broadcasted_iota(jnp.int32, logits.shape, 1)
    big = jnp.int32(LANES)
    ng, ne = MOE_GROUPS, MOE_EXPERTS_PER_GROUP
    gl = jnp.where(lane < ng, logits, NEG_INF)
    gmax = jnp.max(gl, axis=-1, keepdims=True)
    gsum = jnp.sum(jnp.exp(gl - gmax), axis=-1, keepdims=True)
    g_sel = jnp.min(jnp.where(gl == gmax, lane, big), axis=-1, keepdims=True)
    g_w = 1.0 / gsum
    lo = ng + g_sel * ne
    in_group = (lane >= lo) & (lane < lo + ne)
    el = jnp.where(in_group, logits, NEG_INF)
    emax = jnp.max(el, axis=-1, keepdims=True)
    ee = jnp.exp(el - emax)
    ep = ee / jnp.sum(ee, axis=-1, keepdims=True)
    ep = jnp.where(in_group, ep, -1.0)
    p1 = jnp.max(ep, axis=-1, keepdims=True)
    i1 = jnp.min(jnp.where(ep == p1, lane, big), axis=-1, keepdims=True)
    ep2 = jnp.where(lane == i1, -1.0, ep)
    p2 = jnp.max(ep2, axis=-1, keepdims=True)
    i2 = jnp.min(jnp.where(ep2 == p2, lane, big), axis=-1, keepdims=True)
    psum = p1 + p2
    ids_ref[...] = jnp.where(lane == 0, i1 - ng, jnp.where(lane == 1, i2 - ng, 0))
    gates_ref[...] = jnp.where(lane == 0, g_w * p1 / psum, jnp.where(lane == 1, g_w * p2 / psum, 0.0))


def _router(hf, w_rg, b_rg, w_re, b_re):
    n, d = hf.shape
    ng, ne = MOE_GROUPS, MOE_EXPERTS_PER_GROUP
    w = jnp.concatenate([w_rg.astype(F32), jnp.transpose(w_re.astype(F32), (1, 0, 2)).reshape(d, ng * ne)], axis=1)
    w = jnp.pad(w, ((0, 0), (0, LANES - w.shape[1])))
    bias = jnp.pad(jnp.concatenate([b_rg.astype(F32), b_re.astype(F32).reshape(-1)]), (0, LANES - ng - ng * ne)).reshape(1, LANES)
    tm = _tile(n, 256, SUBLANES)
    ids, gates = pl.pallas_call(
        _router_kernel,
        grid=(n // tm,),
        in_specs=[pl.BlockSpec((tm, d), lambda i: (i, 0)), pl.BlockSpec((d, LANES), lambda i: (0, 0)),
                  pl.BlockSpec((1, LANES), lambda i: (0, 0))],
        out_specs=[pl.BlockSpec((tm, LANES), lambda i: (i, 0)), pl.BlockSpec((tm, LANES), lambda i: (i, 0))],
        out_shape=[jax.ShapeDtypeStruct((n, LANES), jnp.int32), jax.ShapeDtypeStruct((n, LANES), F32)],
        compiler_params=_params(("parallel",)),
        name="moe_router",
    )(hf, w, bias)
    return ids[:, :MOE_TOP], gates[:, :MOE_TOP]


def _row_copy(src, s_row, dst, d_row, sem):
    return pltpu.make_async_copy(src.at[pl.ds(s_row, 1)], dst.at[pl.ds(d_row, 1)], sem)


def _moe_up_kernel(be_ref, nu_ref, tokc_ref, tokn_ref, x_hbm, w_ref, h_ref, xbuf, sem, *, rows, kc):
    b = pl.program_id(0)
    n_used = nu_ref[0]
    slot = lax.rem(b, 2)

    def gather(tok_ref, s):
        def body(r, carry):
            _row_copy(x_hbm, tok_ref[0, 0, r], xbuf.at[s], r, sem.at[s]).start()
            return carry
        lax.fori_loop(0, rows, body, 0)

    @pl.when(b == 0)
    def _():
        gather(tokc_ref, 0)

    @pl.when(b + 1 < n_used)
    def _():
        gather(tokn_ref, 1 - slot)

    @pl.when(b < n_used)
    def _():
        def body(r, carry):
            _row_copy(x_hbm, tokc_ref[0, 0, r], xbuf.at[slot], r, sem.at[slot]).wait()
            return carry
        lax.fori_loop(0, rows, body, 0)
        d = xbuf.shape[2]
        f = h_ref.shape[1]
        acc = jnp.zeros((rows, 2 * f), F32)
        for c0 in range(0, d, kc):
            acc = acc + jnp.dot(xbuf[slot, :, c0:c0 + kc].astype(BF16), w_ref[0, c0:c0 + kc, :].astype(BF16),
                                preferred_element_type=F32)
        a, g = acc[:, :f], acc[:, f:]
        h_ref[...] = (a * jax.nn.sigmoid(a) * g).astype(h_ref.dtype)

    @pl.when(b >= n_used)
    def _():
        h_ref[...] = jnp.zeros_like(h_ref)


def _moe_down_kernel(be_ref, nu_ref, nv_ref, dstc_ref, dstp_ref, h_ref, g_ref, w_ref, y_hbm, ybuf, sem, *, nb):
    b = pl.program_id(0)
    n_used = nu_ref[0]
    slot = lax.rem(b, 2)

    def scatter(dst_ref, s, count, wait):
        def body(r, carry):
            cp = _row_copy(ybuf.at[s], r, y_hbm, dst_ref[0, 0, r], sem.at[s])
            if wait:
                cp.wait()
            else:
                cp.start()
            return carry
        lax.fori_loop(0, count, body, 0)

    @pl.when(b < n_used)
    def _():
        y = jnp.dot(h_ref[...], w_ref[0].astype(BF16), preferred_element_type=F32) * g_ref[...]
        ybuf[slot] = y
        scatter(dstc_ref, slot, nv_ref[b], False)

    @pl.when((b >= 1) & (b - 1 < n_used))
    def _():
        scatter(dstp_ref, 1 - slot, nv_ref[jnp.maximum(b - 1, 0)], True)

    @pl.when((b == nb - 1) & (b < n_used))
    def _():
        scatter(dstc_ref, slot, nv_ref[b], True)


def _moe(hf, w_rg, b_rg, w_re, b_re, w_up, w_down):
    n, d = hf.shape
    e, _, f2 = w_up.shape
    f = f2 // 2
    rows = MOE_ROWS
    ids, gates = _router(hf, w_rg, b_rg, w_re, b_re)

    a = n * MOE_TOP
    nb = -(-a // rows) + e
    e_flat = ids.reshape(-1)
    order = jnp.argsort(e_flat).astype(jnp.int32)
    counts = jnp.bincount(e_flat, length=e).astype(jnp.int32)
    starts = jnp.cumsum(counts) - counts
    blocks_per = (counts + rows - 1) // rows
    blk_end = jnp.cumsum(blocks_per)
    first_blk = blk_end - blocks_per
    n_used = blk_end[-1].astype(jnp.int32)
    blk_ids = jnp.arange(nb, dtype=jnp.int32)
    owner = jnp.minimum(jnp.searchsorted(blk_end, blk_ids, side="right"), e - 1).astype(jnp.int32)
    in_e0 = (blk_ids - first_blk[owner]) * rows
    n_valid = jnp.where(blk_ids < n_used, jnp.clip(counts[owner] - in_e0, 0, rows), 0).astype(jnp.int32)
    within = jnp.arange(rows, dtype=jnp.int32)[None, :]
    live = within < n_valid[:, None]
    src = jnp.clip((starts[owner] + in_e0)[:, None] + within, 0, a - 1)
    assign = order[src]
    tok = assign // MOE_TOP
    tok3 = jnp.where(live, tok, 0).reshape(nb, 1, rows)
    dst3 = jnp.where(live, (assign % MOE_TOP) * n + tok, 0).reshape(nb, 1, rows)
    gate_buf = jnp.where(live, gates.reshape(-1)[assign], 0.0)
    block_expert = owner[jnp.minimum(blk_ids, n_used - 1)]
    n_used_arr = n_used.reshape(1)
    smem_blk = functools.partial(pl.BlockSpec, (1, 1, rows), memory_space=pltpu.SMEM)
    h_mid = pl.pallas_call(
        functools.partial(_moe_up_kernel, rows=rows, kc=_tile(d, 512, LANES)),
        grid_spec=pltpu.PrefetchScalarGridSpec(
            num_scalar_prefetch=2,
            grid=(nb,),
            in_specs=[smem_blk(lambda b, be, nu: (b, 0, 0)),
                      smem_blk(lambda b, be, nu: (jnp.minimum(b + 1, nb - 1), 0, 0)),
                      pl.BlockSpec(memory_space=pl.ANY),
                      pl.BlockSpec((1, d, f2), lambda b, be, nu: (be[b], 0, 0))],
            out_specs=pl.BlockSpec((rows, f), lambda b, be, nu: (b, 0)),
            scratch_shapes=[pltpu.VMEM((2, rows, d), F32), pltpu.SemaphoreType.DMA((2,))]),
        out_shape=jax.ShapeDtypeStruct((nb * rows, f), BF16),
        compiler_params=_params(("arbitrary",)),
        name="moe_up",
    )(block_expert, n_used_arr, tok3, tok3, hf, w_up)

    y_tok = pl.pallas_call(
        functools.partial(_moe_down_kernel, nb=nb),
        grid_spec=pltpu.PrefetchScalarGridSpec(
            num_scalar_prefetch=3,
            grid=(nb,),
            in_specs=[smem_blk(lambda b, be, nu, nv: (b, 0, 0)),
                      smem_blk(lambda b, be, nu, nv: (jnp.maximum(b - 1, 0), 0, 0)),
                      pl.BlockSpec((rows, f), lambda b, be, nu, nv: (b, 0)),
                      pl.BlockSpec((rows, 1), lambda b, be, nu, nv: (b, 0)),
                      pl.BlockSpec((1, f, d), lambda b, be, nu, nv: (be[b], 0, 0))],
            out_specs=pl.BlockSpec(memory_space=pl.ANY),
            scratch_shapes=[pltpu.VMEM((2, rows, d), F32), pltpu.SemaphoreType.DMA((2,))]),
        out_shape=jax.ShapeDtypeStruct((MOE_TOP * n, d), F32),
        compiler_params=_params(("arbitrary",)),
        name="moe_down",
    )(block_expert, n_used_arr, n_valid, dst3, dst3, h_mid, gate_buf.reshape(nb * rows, 1), w_down)
    return y_tok


def _page_specs(n_pool_shape, n_pages, pgs):
    _, page, width = n_pool_shape
    return [pl.BlockSpec((1, page, width),
                         functools.partial(lambda bi, j, pt, r: (pt[bi * n_pages + j * pgs + r], 0, 0), r=r))
            for r in range(pgs)]


def _cmp_paged_kernel(pt_ref, *refs, pgs, g, d):
    page_refs = refs[:pgs]
    perm_ref, wk_ref, wv_ref, uk_ref, uv_ref = refs[pgs:]
    cs = CMP_STRIDE
    page = page_refs[0].shape[1]
    cpp = page // cs
    xs = [jnp.dot(perm_ref[...], pr[0].astype(BF16), preferred_element_type=F32) for pr in page_refs]
    m = g * pgs * cpp
    accs = [jnp.zeros((m, wk_ref.shape[1]), F32), jnp.zeros((m, wv_ref.shape[1]), F32)]
    for pp in range(0, cs, 2):
        for kv, w_ref in enumerate((wk_ref, wv_ref)):
            halves = []
            for p in (pp, pp + 1):
                pieces = [xs[r][p * cpp:(p + 1) * cpp, (kv * g + gi) * d:(kv * g + gi + 1) * d]
                          for gi in range(g) for r in range(pgs)]
                halves.append(jnp.concatenate(pieces, axis=0))
            lhs = jnp.concatenate(halves, axis=1).astype(BF16)
            accs[kv] = accs[kv] + jnp.dot(lhs, w_ref[pp * d:(pp + 2) * d, :], preferred_element_type=F32)
    uk_ref[0] = accs[0].reshape(g, pgs * cpp, wk_ref.shape[1])
    uv_ref[0] = accs[1].reshape(g, pgs * cpp, wv_ref.shape[1])


def _cmp_hidden_paged(cache, page_table, wcat_k, wcat_v):
    n_pool, page = cache.shape[:2]
    g, d = cache.shape[3], cache.shape[4]
    width = 2 * g * d
    b, n_pages = page_table.shape
    pgs = _tile(n_pages, 8, 1)
    cpp = page // CMP_STRIDE
    nch = n_pages * cpp
    perm = np.zeros((page, page), np.float32)
    for c in range(cpp):
        for p in range(CMP_STRIDE):
            perm[p * cpp + c, c * CMP_STRIDE + p] = 1.0
    cache3 = cache.reshape(n_pool, page, width)
    hid2 = wcat_k.shape[1]
    const = lambda shape: pl.BlockSpec(shape, lambda bi, j, pt: (0, 0))
    uk, uv = pl.pallas_call(
        functools.partial(_cmp_paged_kernel, pgs=pgs, g=g, d=d),
        grid_spec=pltpu.PrefetchScalarGridSpec(
            num_scalar_prefetch=1,
            grid=(b, n_pages // pgs),
            in_specs=_page_specs(cache3.shape, n_pages, pgs)
            + [const((page, page)), const(wcat_k.shape), const(wcat_v.shape)],
            out_specs=[pl.BlockSpec((1, g, pgs * cpp, hid2), lambda bi, j, pt: (bi, 0, j, 0))] * 2),
        out_shape=[jax.ShapeDtypeStruct((b, g, nch, hid2), F32)] * 2,
        compiler_params=_params(("parallel", "parallel")),
        name="cmp_hidden_paged",
    )(page_table.reshape(-1).astype(jnp.int32), *([cache3] * pgs), jnp.asarray(perm, BF16),
      wcat_k.astype(BF16), wcat_v.astype(BF16))
    return uk.reshape(b * g, nch, hid2), uv.reshape(b * g, nch, hid2)
def _cmp_post_kernel(u_ref, pe_ref, w2_ref, g_ref, o_ref, *, n_cmp, norm):
    u = u_ref[0]
    nch, hid2 = u.shape
    hid = hid2 // 2
    nxt = pltpu.roll(u[:, hid:], nch - 1, axis=0)
    x = (pe_ref[0:1, :] + u[:, :hid]) + nxt
    y = 0.5 * x * (1.0 + jnp.tanh(0.7978845608028654 * (x + 0.044715 * (x * x * x))))
    z = jnp.dot(y.astype(BF16), w2_ref[...].astype(BF16), preferred_element_type=F32)
    if norm:
        z = _group_rms(z) * g_ref[...]
    row = lax.broadcasted_iota(jnp.int32, z.shape, 0)
    o_ref[0] = jnp.where(row < n_cmp, z, 0.0)


def _cmp_wcat(w1):
    r = CMP_LEN // CMP_STRIDE
    assert r == 2
    w1r = w1.reshape(r, w1.shape[0] // r, w1.shape[1])
    return jnp.concatenate([w1r[0], w1r[1]], axis=1)


def _cmp_hidden_dense(rows, nch, wcat):
    b, _, g, d = rows.shape
    ch = rows[:, :nch * CMP_STRIDE].reshape(b, nch, CMP_STRIDE, g, d)
    ch = jnp.transpose(ch, (0, 3, 1, 2, 4)).reshape(b * g * nch, CMP_STRIDE * d).astype(BF16)
    return _mm(ch, wcat, name="cmp_hidden")[0].reshape(b * g, nch, wcat.shape[1])


def _cmp_finish(u, n_cmp, pe, w1, w2, g_k):
    bg, nch, hid2 = u.shape
    hid = hid2 // 2
    d = w2.shape[1]
    pe_rows = jnp.pad(pe.reshape(1, -1), ((0, SUBLANES - 1), (0, 0)))
    pe_hid = _mm(pe_rows, w1, name="cmp_pe")[0]
    gain = (jnp.ones((d,), F32) if g_k is None else g_k.astype(F32)).reshape(1, d)
    return pl.pallas_call(
        functools.partial(_cmp_post_kernel, n_cmp=n_cmp, norm=g_k is not None),
        grid=(bg,),
        in_specs=[pl.BlockSpec((1, nch, hid2), lambda i: (i, 0, 0)), pl.BlockSpec((SUBLANES, hid), lambda i: (0, 0)),
                  pl.BlockSpec((hid, d), lambda i: (0, 0)), pl.BlockSpec((1, d), lambda i: (0, 0))],
        out_specs=pl.BlockSpec((1, nch, d), lambda i: (i, 0, 0)),
        out_shape=jax.ShapeDtypeStruct((bg, nch, d), F32),
        compiler_params=_params(("parallel",)),
        name="cmp_post",
    )(u, pe_hid, w2, gain)


def _slc_map(n_cmp, n_slc, rows, cols):
    a = SLC_BLOCK // CMP_STRIDE
    bb = CMP_LEN // CMP_STRIDE
    j = np.arange(n_slc)[:, None, None]
    i = j * a + np.arange(a)[None, :, None] + np.arange(bb)[None, None, :] - bb + 1
    i, jj = np.broadcast_arrays(i, j)
    ok = (i >= 0) & (i < n_cmp)
    m = np.zeros((rows, cols), np.float32)
    np.add.at(m, (i[ok], jj[ok]), 1.0)
    return jnp.asarray(m)


def _masked_softmax_rows(s):
    m = jnp.max(s, axis=-1, keepdims=True)
    e = jnp.exp(s - jnp.where(m > NEG_INF, m, 0.0))
    den = jnp.sum(e, axis=-1, keepdims=True)
    return e / jnp.where(den > 0, den, 1.0)


def _store_gated(o, gl_ref, prev_ref, o_ref, branch, hpg, tq):
    d = NSA_HEAD_DIM
    gate = jax.nn.sigmoid(gl_ref[0])
    for hh in range(hpg):
        c = hh * N_BRANCH + branch
        val = gate[:, c:c + 1] * o[hh * tq:(hh + 1) * tq]
        if prev_ref is not None:
            val = prev_ref[0, :, hh * d:(hh + 1) * d] + val
        o_ref[0, :, hh * d:(hh + 1) * d] = val.astype(o_ref.dtype)


def _nsa_cmp_kernel(q_ref, kc_ref, vc_ref, map_ref, gl_ref, o_ref, sel_ref, *, hpg, tq, pos0, n_cmp, n_slc, n_top):
    i = pl.program_id(2)
    rws = hpg * tq
    q = q_ref[0, 0].reshape(rws, NSA_HEAD_DIM).astype(BF16)
    kc = kc_ref[0].astype(BF16)
    s = lax.dot_general(q, kc, (((1,), (1,)), ((), ())), preferred_element_type=F32) * NSA_SCALE
    ncp = s.shape[1]
    tok = lax.broadcasted_iota(jnp.int32, (rws, 1), 0) & (tq - 1)
    pos = pos0 + i * tq + tok
    cidx = lax.broadcasted_iota(jnp.int32, (1, ncp), 1)
    ok = (cidx * CMP_STRIDE + (CMP_LEN - 1) <= pos) & (cidx < n_cmp)
    p = _masked_softmax_rows(jnp.where(ok, s, NEG_INF))
    o = jnp.dot(p.astype(BF16), vc_ref[0].astype(BF16), preferred_element_type=F32)
    _store_gated(o, gl_ref, None, o_ref, 0, hpg, tq)

    psum = jnp.sum(p.astype(BF16).astype(F32).reshape(hpg, tq, ncp), axis=0)
    imp = jnp.dot(psum, map_ref[...], precision=lax.Precision.HIGHEST, preferred_element_type=F32)
    nsp = imp.shape[1]
    posq = pos0 + i * tq + lax.broadcasted_iota(jnp.int32, (tq, 1), 0)
    blk = lax.broadcasted_iota(jnp.int32, (tq, nsp), 1)
    back = (posq >> SLC_SHIFT) - blk
    real = blk < n_slc
    valid = (blk * SLC_BLOCK <= posq) & real
    forced = (blk == 0) | ((back >= 0) & (back < SLC_LOCAL))
    score = jnp.where(valid, imp + jnp.where(forced, FORCE_BONUS, 0.0), NEG_INF)
    rank = jnp.zeros((tq, nsp), jnp.int32)
    for kb in range(n_slc):
        col = score[:, kb:kb + 1]
        ahead = (col > score) | ((col == score) & (blk > kb))
        rank = rank + ahead.astype(jnp.int32)
    sel_ref[0, 0] = jnp.where((rank < n_top) & real, 1.0, 0.0)


def _nsa_cmp(qn, kc, vc, glog, pos0, n_cmp, n_slc):
    b, g, hpg, t, d = qn.shape
    ncp = kc.shape[1]
    nsp = -(-n_slc // LANES) * LANES
    tq = _tile(t, 128, SUBLANES)
    smap = _slc_map(n_cmp, n_slc, ncp, nsp)
    n_top = min(SLC_TOP, n_slc)
    return pl.pallas_call(
        functools.partial(_nsa_cmp_kernel, hpg=hpg, tq=tq, pos0=pos0, n_cmp=n_cmp, n_slc=n_slc, n_top=n_top),
        grid=(b, g, t // tq),
        in_specs=[pl.BlockSpec((1, 1, hpg, tq, d), lambda bi, gi, i: (bi, gi, 0, i, 0)),
                  pl.BlockSpec((1, ncp, d), lambda bi, gi, i: (bi * g + gi, 0, 0)),
                  pl.BlockSpec((1, ncp, d), lambda bi, gi, i: (bi * g + gi, 0, 0)),
                  pl.BlockSpec((ncp, nsp), lambda bi, gi, i: (0, 0)),
                  pl.BlockSpec((1, tq, LANES), lambda bi, gi, i: (bi, i, gi))],
        out_specs=[pl.BlockSpec((1, tq, hpg * d), lambda bi, gi, i: (bi, i, gi)),
                   pl.BlockSpec((1, 1, tq, nsp), lambda bi, gi, i: (bi, gi, i, 0))],
        out_shape=[jax.ShapeDtypeStruct((b, t, g * hpg * d), F32), jax.ShapeDtypeStruct((b, g, t, nsp), F32)],
        compiler_params=_params(("parallel", "parallel", "parallel")),
        name="nsa_cmp",
    )(qn, kc, vc, smap, glog)


def _nsa_attn_kernel(*refs, hpg, tq, tk, n_kt, qpos0, kpos0, window, use_sel, branch):
    refs = list(refs)
    q_ref, k_ref, v_ref = refs[:3]
    refs = refs[3:]
    sel_ref = None
    if use_sel:
        sel_ref = refs[0]
        refs = refs[1:]
    gl_ref, prev_ref, o_ref, m_ref, l_ref, acc_ref = refs
    i = pl.program_id(2)
    d = NSA_HEAD_DIM
    rws = hpg * tq
    q = q_ref[0, 0].reshape(rws, d).astype(BF16)
    posq = qpos0 + i * tq + lax.broadcasted_iota(jnp.int32, (tq, 1), 0)
    sel = sel_ref[0, 0].astype(BF16) if use_sel else None
    m_ref[...] = jnp.full_like(m_ref, NEG_INF)
    l_ref[...] = jnp.zeros_like(l_ref)
    acc_ref[...] = jnp.zeros_like(acc_ref)

    def step(k, v, key0, pos_base):
        n = k.shape[0]
        s = lax.dot_general(q, k.astype(BF16), (((1,), (1,)), ((), ())), preferred_element_type=F32) * NSA_SCALE
        kidx = key0 + lax.broadcasted_iota(jnp.int32, (1, n), 1)
        kpos = pos_base + kidx
        ok = kpos <= posq
        if window is not None:
            ok = ok & (posq - kpos < window)
        if use_sel:
            nsp = sel.shape[1]
            kblk = (key0 + lax.broadcasted_iota(jnp.int32, (nsp, n), 1)) >> SLC_SHIFT
            expand = (kblk == lax.broadcasted_iota(jnp.int32, (nsp, n), 0)).astype(BF16)
            ok = ok & (jnp.dot(sel, expand, preferred_element_type=F32) > 0.5)
        s = jnp.where(ok[None], s.reshape(hpg, tq, n), NEG_INF).reshape(rws, n)
        m_old = m_ref[...]
        m_new = jnp.maximum(m_old, jnp.max(s, axis=-1, keepdims=True))
        m_safe = jnp.where(m_new > NEG_INF, m_new, 0.0)
        p = jnp.exp(s - m_safe)
        alpha = jnp.exp(m_old - m_safe)
        l_ref[...] = alpha * l_ref[...] + jnp.sum(p, axis=-1, keepdims=True)
        acc_ref[...] = alpha * acc_ref[...] + jnp.dot(p.astype(BF16), v.astype(BF16), preferred_element_type=F32)
        m_ref[...] = m_new

    q_lo = qpos0 + i * tq
    q_hi = q_lo + tq - 1
    hi = jnp.clip((q_hi - kpos0) // tk + 1, 0, n_kt)
    if window is None:
        lo = 0
    else:
        lo = jnp.clip((q_lo - (window - 1) - kpos0) // tk, 0, n_kt)

    def body(kt, carry):
        off = pl.multiple_of(kt * tk, tk)
        step(k_ref[0, pl.ds(off, tk), :], v_ref[0, pl.ds(off, tk), :], kt * tk, kpos0)
        return carry

    lax.fori_loop(lo, hi, body, 0)
    l = l_ref[...]
    o = acc_ref[...] / jnp.where(l > 0, l, 1.0)
    _store_gated(o, gl_ref, prev_ref, o_ref, branch, hpg, tq)


def _nsa_attn(qn, kv, glog, prev, *, qpos0, kpos0, branch, out_dtype, window=None, sel=None, tk=512):
    b, g, hpg, t, d = qn.shape
    tk_total = kv.shape[1]
    tq = _tile(t, 128, SUBLANES)
    tk = _tile(tk_total, tk, LANES)
    n_kt = tk_total // tk
    in_specs = [pl.BlockSpec((1, 1, hpg, tq, d), lambda bi, gi, i: (bi, gi, 0, i, 0)),
                pl.BlockSpec((1, tk_total, d), lambda bi, gi, i: (bi, 0, gi)),
                pl.BlockSpec((1, tk_total, d), lambda bi, gi, i: (bi, 0, g + gi))]
    args = [qn, kv, kv]
    if sel is not None:
        nsp = sel.shape[-1]
        in_specs.append(pl.BlockSpec((1, 1, tq, nsp), lambda bi, gi, i: (bi, gi, i, 0)))
        args.append(sel)
    in_specs += [pl.BlockSpec((1, tq, LANES), lambda bi, gi, i: (bi, i, gi)),
                 pl.BlockSpec((1, tq, hpg * d), lambda bi, gi, i: (bi, i, gi))]
    args += [glog, prev]
    rws = hpg * tq
    return pl.pallas_call(
        functools.partial(_nsa_attn_kernel, hpg=hpg, tq=tq, tk=tk, n_kt=n_kt, qpos0=qpos0, kpos0=kpos0,
                          window=window, use_sel=sel is not None, branch=branch),
        grid=(b, g, t // tq),
        in_specs=in_specs,
        out_specs=pl.BlockSpec((1, tq, hpg * d), lambda bi, gi, i: (bi, i, gi)),
        out_shape=jax.ShapeDtypeStruct((b, t, g * hpg * d), out_dtype),
        scratch_shapes=[pltpu.VMEM((rws, 1), F32), pltpu.VMEM((rws, 1), F32), pltpu.VMEM((rws, d), F32)],
        compiler_params=_params(("parallel", "parallel", "parallel")),
        name="nsa_attn_%d" % branch,
    )(*args)


def _nsa_slc_paged_kernel(pt_ref, *refs, pgs, g, hpg, tq, qpos0, past_len, n_steps):
    page_refs = refs[:pgs]
    q_ref, sel_ref, tail_ref, gl_ref, prev_ref, o_ref, m_ref, l_ref, acc_ref = refs[pgs:]
    j = pl.program_id(1)
    d = NSA_HEAD_DIM
    rws = hpg * tq
    page = page_refs[0].shape[1]
    nsp = sel_ref.shape[-1]
    posq = qpos0 + lax.broadcasted_iota(jnp.int32, (tq, 1), 0)
    sel_all = sel_ref[0].reshape(g * tq, nsp).astype(BF16)

    @pl.when(j == 0)
    def _():
        m_ref[...] = jnp.full_like(m_ref, NEG_INF)
        l_ref[...] = jnp.zeros_like(l_ref)
        acc_ref[...] = jnp.zeros_like(acc_ref)

    def update(tiles, key0):
        n = sum(t.shape[0] for t in tiles)
        ok_pos = key0 + lax.broadcasted_iota(jnp.int32, (1, n), 1) <= posq
        kblk = (key0 + lax.broadcasted_iota(jnp.int32, (nsp, n), 1)) >> SLC_SHIFT
        expand = (kblk == lax.broadcasted_iota(jnp.int32, (nsp, n), 0)).astype(BF16)
        picked = jnp.dot(sel_all, expand, preferred_element_type=F32) > 0.5
        for gi in range(g):
            k = jnp.concatenate([t[:, gi * d:(gi + 1) * d] for t in tiles], axis=0).astype(BF16)
            v = jnp.concatenate([t[:, (g + gi) * d:(g + gi + 1) * d] for t in tiles], axis=0).astype(BF16)
            q = q_ref[0, gi].reshape(rws, d).astype(BF16)
            s = lax.dot_general(q, k, (((1,), (1,)), ((), ())), preferred_element_type=F32) * NSA_SCALE
            ok = ok_pos & picked[gi * tq:(gi + 1) * tq]
            s = jnp.where(ok[None], s.reshape(hpg, tq, n), NEG_INF).reshape(rws, n)
            m_old = m_ref[gi]
            m_new = jnp.maximum(m_old, jnp.max(s, axis=-1, keepdims=True))
            m_safe = jnp.where(m_new > NEG_INF, m_new, 0.0)
            p = jnp.exp(s - m_safe)
            alpha = jnp.exp(m_old - m_safe)
            l_ref[gi] = alpha * l_ref[gi] + jnp.sum(p, axis=-1, keepdims=True)
            acc_ref[gi] = alpha * acc_ref[gi] + jnp.dot(p.astype(BF16), v, preferred_element_type=F32)
            m_ref[gi] = m_new

    update([pr[0] for pr in page_refs], j * (pgs * page))

    @pl.when(j == n_steps - 1)
    def _():
        update([tail_ref[0]], past_len)
        gate = jax.nn.sigmoid(gl_ref[0])
        for gi in range(g):
            l = l_ref[gi]
            o = acc_ref[gi] / jnp.where(l > 0, l, 1.0)
            for hh in range(hpg):
                c = gi * LANES + hh * N_BRANCH + 1
                col = (gi * hpg + hh) * d
                o_ref[0, :, col:col + d] = prev_ref[0, :, col:col + d] + gate[:, c:c + 1] * o[hh * tq:(hh + 1) * tq]


def _nsa_slc_paged(qn, cache, page_table, sel, tail, glog, prev, *, qpos0):
    b, g, hpg, t, d = qn.shape
    n_pool, page = cache.shape[:2]
    width = 2 * g * d
    n_pages = page_table.shape[1]
    pgs = _tile(n_pages, 8, 1)
    n_steps = n_pages // pgs
    nsp = sel.shape[-1]
    nt = tail.shape[1]
    rws = hpg * t
    cache3 = cache.reshape(n_pool, page, width)
    return pl.pallas_call(
        functools.partial(_nsa_slc_paged_kernel, pgs=pgs, g=g, hpg=hpg, tq=t, qpos0=qpos0,
                          past_len=n_pages * page, n_steps=n_steps),
        grid_spec=pltpu.PrefetchScalarGridSpec(
            num_scalar_prefetch=1,
            grid=(b, n_steps),
            in_specs=_page_specs(cache3.shape, n_pages, pgs)
            + [pl.BlockSpec((1, g, hpg, t, d), lambda bi, j, pt: (bi, 0, 0, 0, 0)),
               pl.BlockSpec((1, g, t, nsp), lambda bi, j, pt: (bi, 0, 0, 0)),
               pl.BlockSpec((1, nt, width), lambda bi, j, pt: (bi, 0, 0)),
               pl.BlockSpec((1, t, g * LANES), lambda bi, j, pt: (bi, 0, 0)),
               pl.BlockSpec((1, t, g * hpg * d), lambda bi, j, pt: (bi, 0, 0))],
            out_specs=pl.BlockSpec((1, t, g * hpg * d), lambda bi, j, pt: (bi, 0, 0)),
            scratch_shapes=[pltpu.VMEM((g, rws, 1), F32), pltpu.VMEM((g, rws, 1), F32), pltpu.VMEM((g, rws, d), F32)]),
        out_shape=jax.ShapeDtypeStruct((b, t, g * hpg * d), F32),
        compiler_params=_params(("parallel", "arbitrary")),
        name="nsa_slc_paged",
    )(page_table.reshape(-1).astype(jnp.int32), *([cache3] * pgs), qn, sel, tail, glog, prev)


def _gate_weight(w_in, d_model):
    g = NSA_KV_HEADS
    hpg = d_model // NSA_HEAD_DIM // g
    wg = w_in[:, d_model:].reshape(d_model, g, hpg * N_BRANCH)
    wg = jnp.pad(wg, ((0, 0), (0, 0), (0, LANES - hpg * N_BRANCH)))
    return wg.reshape(d_model, g * LANES)


def kernel(x_prompt, x_sample, state_ret, cache_cmp_kv, cache_slc_kv, cache_win_kv, page_table, p_prompt, p_sample, g_mix, g_ffn, w_ret_in, w_ret_out, w_nsa_in, g_nsa_q, w_nsa_out, g_kv, w_kv, g_k_cmp, g_k_slc, g_k_win, pe_cmp_k, w_cmp_k1, w_cmp_k2, pe_cmp_v, w_cmp_v1, w_cmp_v2, w_rg, b_rg, w_re, b_re, w_moe_up, w_moe_down, w_ple_up, g_ple, w_ple_gate):
    depth = g_mix.shape[0]
    n_a = w_ret_in.shape[0]
    g, d = NSA_KV_HEADS, NSA_HEAD_DIM
    gd = g * d
    d_model = x_prompt.shape[-1]
    page = cache_cmp_kv.shape[1]
    past_len = page_table.shape[1] * page

    groups = [
        dict(x=x_prompt.reshape(-1, d_model), p=p_prompt, b=x_prompt.shape[0], t=x_prompt.shape[1], pos0=0, s0=None),
        dict(x=x_sample.reshape(-1, d_model), p=p_sample, b=x_sample.shape[0], t=x_sample.shape[1], pos0=past_len, s0=state_ret),
    ]
    n_rows = [gr["x"].shape[0] for gr in groups]
    n_tok = sum(n_rows)
    offs = [0, n_rows[0]]
    for gr in groups:
        gr["ret"] = []

    for i in range(depth):
        for gr in groups:
            b, t = gr["b"], gr["t"]
            h = _rms([(gr["x"], 0)], g_mix[i], [BF16])[0]
            if i < n_a:
                qkvg = _mm(h, w_ret_in[i], tm=2048, name="ret_in")[0]
                pos = gr["pos0"] + jnp.arange(t)
                s0 = None if gr["s0"] is None else gr["s0"][i]
                o, s_new = _retention(qkvg.reshape(b, t, -1), pos, s0)
                gr["ret"].append(s_new)
                gr["x"] = _mm_resid(o.reshape(b * t, -1), w_ret_out[i], gr["x"], "ret_out")
            else:
                j = i - n_a
                qn = _mm_q(h, w_nsa_in[j], g_nsa_q[j], b, t)
                glog = _mm(h, _gate_weight(w_nsa_in[j], d_model), name="nsa_gate")[0].reshape(b, t, g * LANES)
                ctx = gr["ctx"]
                o1, sel = _nsa_cmp(qn, ctx["k_c"], ctx["v_c"], glog, gr["pos0"], ctx["n_cmp"], ctx["n_slc"])
                if ctx["slc_tail"] is None:
                    o2 = _nsa_attn(qn, ctx["slc"], glog, o1, qpos0=gr["pos0"], kpos0=0, branch=1, out_dtype=F32, sel=sel)
                else:
                    o2 = _nsa_slc_paged(qn, cache_slc_kv, page_table, sel, ctx["slc_tail"], glog, o1, qpos0=gr["pos0"])
                o3 = _nsa_attn(qn, ctx["win"], glog, o2, qpos0=gr["pos0"], kpos0=ctx["win_pos0"], branch=2,
                               out_dtype=BF16, window=WINDOW)
                gr["x"] = _mm_resid(o3.reshape(b * t, -1), w_nsa_out[j], gr["x"], "nsa_out")

        hf = jnp.concatenate([_rms([(gr["x"], 0)], g_ffn[i], [F32])[0] for gr in groups], axis=0)
        y_tok = _moe(hf, w_rg[i], b_rg[i], w_re[i], b_re[i], w_moe_up[i], w_moe_down[i])

        for gi, gr in enumerate(groups):
            rows = n_rows[gi]
            x_new, hp = _rms([(gr["x"], 0), (y_tok, offs[gi]), (y_tok, n_tok + offs[gi])], g_ple[i], [BF16],
                             want_sum=True, rows=rows, tm=64)
            pu = _mm(gr["p"][i].reshape(rows, -1), w_ple_up[i], name="ple_up")[0]
            gr["x"] = _mm_ple(hp, w_ple_gate[i], x_new, pu)

        if i == n_a - 1:
            for gi, gr in enumerate(groups):
                b, t = gr["b"], gr["t"]
                hk = _rms([(gr["x"], 0)], g_kv, [BF16])[0]
                kv = _mm_kv(hk, w_kv, g_k_slc, g_k_win).reshape(b, t, 2 * N_BRANCH * gd)
                cmp_new, slc_new, win_new = kv[..., :2 * gd], kv[..., 2 * gd:4 * gd], kv[..., 4 * gd:]
                gr["cmp_new"], gr["slc_new"] = cmp_new, slc_new
                wcat_k, wcat_v = _cmp_wcat(w_cmp_k1), _cmp_wcat(w_cmp_v1)
                n_keys = gr["pos0"] + t
                n_cmp = (n_keys - CMP_LEN) // CMP_STRIDE + 1
                nch = n_cmp + CMP_LEN // CMP_STRIDE - 1
                if gi == 0:
                    slc_tail = None
                    win_keys, win_pos0 = win_new, 0
                    gr["win_state"] = win_new[:, t - min(WINDOW, t):]
                    cmp_rows = cmp_new.reshape(b, t, 2, g, d)
                    u_k = _cmp_hidden_dense(cmp_rows[:, :, 0], nch, wcat_k)
                    u_v = _cmp_hidden_dense(cmp_rows[:, :, 1], nch, wcat_v)
                else:
                    slc_tail = jnp.pad(slc_new, ((0, 0), (0, LANES - t), (0, 0)))
                    w_buf = cache_win_kv.shape[1]
                    win_all = jnp.concatenate([cache_win_kv.reshape(b, w_buf, 2 * gd), win_new], axis=1)
                    n_all = w_buf + t
                    gr["win_state"] = win_all[:, n_all - min(WINDOW, past_len + t):]
                    win_keys = jnp.pad(win_all, ((0, 0), (0, -n_all % LANES), (0, 0)))
                    win_pos0 = past_len - w_buf
                    assert nch * CMP_STRIDE == past_len
                    u_k, u_v = _cmp_hidden_paged(cache_cmp_kv, page_table, wcat_k, wcat_v)
                k_c = _cmp_finish(u_k, n_cmp, pe_cmp_k, w_cmp_k1, w_cmp_k2, g_k_cmp)
                v_c = _cmp_finish(u_v, n_cmp, pe_cmp_v, w_cmp_v1, w_cmp_v2, None)
                gr["ctx"] = dict(k_c=k_c, v_c=v_c, n_cmp=n_cmp, n_slc=-(-n_keys // SLC_BLOCK), slc=slc_new,
                                 slc_tail=slc_tail, win=win_keys, win_pos0=win_pos0)

    outs = []
    for gr in groups:
        outs.append(gr["x"].reshape(gr["b"], gr["t"], d_model))
    rets = [jnp.stack(gr["ret"]) for gr in groups]
    kvs = []
    for name in ("cmp_new", "slc_new", "win_state"):
        for gr in groups:
            a = gr[name]
            kvs.append(a.reshape(a.shape[0], a.shape[1], 2, g, d))
    return (outs[0], outs[1], rets[0], rets[1], kvs[0], kvs[1], kvs[2], kvs[3], kvs[4], kvs[5])
```

```python
import functools

import numpy as np
import jax
import jax.numpy as jnp
from jax import lax
from jax.experimental import pallas as pl
from jax.experimental.pallas import tpu as pltpu

F32 = jnp.float32
BF16 = jnp.bfloat16

RET_HEAD_DIM = 256
RET_V_DIM = 2 * RET_HEAD_DIM
RET_CHUNK = 128
ROPE_BASE = 10000.0
NSA_HEAD_DIM = 128
NSA_KV_HEADS = 4
NSA_SCALE = NSA_HEAD_DIM ** -0.5
CMP_LEN = 32
CMP_STRIDE = 16
SLC_BLOCK = 64
SLC_SHIFT = 6
SLC_TOP = 16
SLC_LOCAL = 2
FORCE_BONUS = 1e4
WINDOW = 512
N_BRANCH = 3
MOE_GROUPS = 8
MOE_EXPERTS_PER_GROUP = 8
MOE_TOP = 2
EPS = 1e-6

LANES = 128
SUBLANES = 8
VMEM_LIMIT_BYTES = 56 * 1024 * 1024
MOE_ROWS = 256
NEG_INF = float("-inf")


def _tile(n, pref, align):
    best = None
    for t in range(align, min(n, pref) + 1, align):
        if n % t == 0:
            best = t
    return n if best is None else best


def _params(semantics):
    return pltpu.CompilerParams(dimension_semantics=semantics, vmem_limit_bytes=VMEM_LIMIT_BYTES)


def _rms_kernel(*refs, n_add, want_sum):
    adds, g_ref, outs = refs[:n_add], refs[n_add], refs[n_add + 1:]
    x = adds[0][...]
    for r in adds[1:]:
        x = x + r[...]
    y = (x * lax.rsqrt(jnp.mean(x * x, axis=-1, keepdims=True) + EPS)) * g_ref[...]
    if want_sum:
        outs[0][...] = x
        outs = outs[1:]
    for o in outs:
        o[...] = y.astype(o.dtype)


def _rms(addends, g, out_dtypes, want_sum=False, rows=None, tm=128):
    d = addends[0][0].shape[1]
    rows = addends[0][0].shape[0] if rows is None else rows
    tm = _tile(rows, tm, SUBLANES)
    for _, off in addends:
        assert off % tm == 0
    in_specs = [pl.BlockSpec((tm, d), functools.partial(lambda i, o: (i + o, 0), o=off // tm)) for _, off in addends]
    in_specs.append(pl.BlockSpec((1, d), lambda i: (0, 0)))
    dts = ([F32] if want_sum else []) + list(out_dtypes)
    outs = pl.pallas_call(
        functools.partial(_rms_kernel, n_add=len(addends), want_sum=want_sum),
        grid=(rows // tm,),
        in_specs=in_specs,
        out_specs=[pl.BlockSpec((tm, d), lambda i: (i, 0)) for _ in dts],
        out_shape=[jax.ShapeDtypeStruct((rows, d), dt) for dt in dts],
        compiler_params=_params(("parallel",)),
        name="rms",
    )(*[a for a, _ in addends], g.reshape(1, d).astype(F32))
    return outs


def _mm_kernel(x_ref, w_ref, *rest, n_extra, epilogue, nk):
    extras, o_ref, acc_ref = rest[:n_extra], rest[n_extra], rest[n_extra + 1]
    k = pl.program_id(2)

    @pl.when(k == 0)
    def _():
        acc_ref[...] = jnp.zeros_like(acc_ref)

    w = w_ref[0] if len(w_ref.shape) == 3 else w_ref[...]
    acc_ref[...] += jnp.dot(x_ref[...].astype(BF16), w.astype(BF16), preferred_element_type=F32)

    @pl.when(k == nk - 1)
    def _():
        acc = acc_ref[...]
        if epilogue is None:
            o_ref[...] = acc.astype(o_ref.dtype)
        else:
            epilogue(acc, extras, o_ref)


def _mm(x, w, *, layer=None, n_out=None, out_dtype=F32, tm=1024, tn=1024, tk=512, epilogue=None, extras=(),
        out_shape=None, out_spec=None, name="mm"):
    m, kdim = x.shape
    n = w.shape[-1] if n_out is None else n_out
    if m <= 256:
        tk = 2048
    tm, tn, tk = _tile(m, tm, SUBLANES), _tile(n, tn, LANES), _tile(kdim, tk, LANES)
    nk = kdim // tk
    if out_shape is None:
        out_shape = jax.ShapeDtypeStruct((m, n), out_dtype)
        out_spec = pl.BlockSpec((tm, tn), lambda i, j, k: (i, j))
    return pl.pallas_call(
        functools.partial(_mm_kernel, n_extra=len(extras), epilogue=epilogue, nk=nk),
        grid=(m // tm, n // tn, nk),
        in_specs=[pl.BlockSpec((tm, tk), lambda i, j, k: (i, k)),
                  pl.BlockSpec((tk, tn), lambda i, j, k: (k, j)) if layer is None
                  else pl.BlockSpec((1, tk, tn), lambda i, j, k: (layer, k, j))]
        + [s for _, s in extras],
        out_specs=out_spec,
        out_shape=out_shape,
        scratch_shapes=[pltpu.VMEM((tm, tn), F32)],
        compiler_params=_params(("parallel", "parallel", "arbitrary")),
        name=name,
    )(x, w, *[a for a, _ in extras]), (tm, tn)


def _tile_spec(tm, tn):
    return pl.BlockSpec((tm, tn), lambda i, j, k: (i, j))


def _mm_resid_kernel(x_ref, w_ref, r_ref, o_ref):
    @pl.when(pl.program_id(2) == 0)
    def _():
        o_ref[...] = r_ref[...]

    o_ref[...] += jnp.dot(x_ref[...].astype(BF16), w_ref[...].astype(BF16), preferred_element_type=F32)


def _mm_resid(x, w, resid, name):
    m, n = resid.shape
    kdim = x.shape[1]
    tm, tn = _tile(m, 2048, SUBLANES), _tile(n, 1024, LANES)
    tk = _tile(kdim, 2048 if m <= 256 else 512, LANES)
    return pl.pallas_call(
        _mm_resid_kernel,
        grid=(m // tm, n // tn, kdim // tk),
        in_specs=[pl.BlockSpec((tm, tk), lambda i, j, k: (i, k)), pl.BlockSpec((tk, tn), lambda i, j, k: (k, j)),
                  _tile_spec(tm, tn)],
        out_specs=_tile_spec(tm, tn),
        out_shape=jax.ShapeDtypeStruct((m, n), F32),
        compiler_params=_params(("parallel", "parallel", "arbitrary")),
        name=name,
    )(x, w, resid)


def _mm_ple(h, w_gate, layer, x, p, w_up):
    m, n = x.shape
    pdim = p.shape[1]
    tm, tn = _tile(m, 1024, SUBLANES), _tile(n, 1024, LANES)

    def epi(acc, extras, o_ref):
        pu = jnp.dot(extras[1][...].astype(BF16), extras[2][0].astype(BF16), preferred_element_type=F32)
        o_ref[...] = extras[0][...] + pu * jax.nn.sigmoid(acc)

    return _mm(h, w_gate, layer=layer, tm=tm, tn=tn, epilogue=epi,
               extras=[(x, _tile_spec(tm, tn)),
                       (p, pl.BlockSpec((tm, pdim), lambda i, j, k: (i, 0))),
                       (w_up, pl.BlockSpec((1, pdim, tn), lambda i, j, k: (layer, 0, j)))], name="ple_gate")[0]


def _group_rms(a):
    return a * lax.rsqrt(jnp.mean(a * a, axis=-1, keepdims=True) + EPS)


def _mm_kv(h, w_kv, g_k_slc, g_k_win):
    m = h.shape[0]
    gd = NSA_KV_HEADS * NSA_HEAD_DIM
    n = 2 * N_BRANCH * gd
    ones = jnp.ones((gd,), F32)
    gain = jnp.concatenate([ones, ones, jnp.tile(g_k_slc.astype(F32), NSA_KV_HEADS), ones,
                            jnp.tile(g_k_win.astype(F32), NSA_KV_HEADS), ones]).reshape(1, n)
    zeros = jnp.zeros((gd,), F32)
    flag = jnp.concatenate([zeros, zeros, ones, zeros, ones, zeros]).reshape(1, n)
    tm = _tile(m, 1024, SUBLANES)

    tn = 2 * gd

    def epi(acc, extras, o_ref):
        parts = [_group_rms(acc[:, c * NSA_HEAD_DIM:(c + 1) * NSA_HEAD_DIM]) for c in range(tn // NSA_HEAD_DIM)]
        normed = jnp.concatenate(parts, axis=-1) * extras[0][...]
        o_ref[...] = jnp.where(extras[1][...] > 0.5, normed, acc)

    row = pl.BlockSpec((1, tn), lambda i, j, k: (0, j))
    return _mm(h, w_kv, tm=tm, tn=tn, epilogue=epi, extras=[(gain, row), (flag, row)], name="kv_proj")[0]


def _mm_q(h, w_in, g_q, b, t):
    m, d_model = h.shape
    g, d = NSA_KV_HEADS, NSA_HEAD_DIM
    hpg = d_model // d // g
    tn = hpg * d
    tm = _tile(t, 1024, SUBLANES)

    def epi(acc, extras, o_ref):
        for hh in range(hpg):
            o_ref[0, 0, hh] = _group_rms(acc[:, hh * d:(hh + 1) * d]) * extras[0][...]

    gq = g_q.reshape(1, d).astype(F32)
    tiles_per_b = t // tm
    return _mm(h, w_in, n_out=g * tn, tm=tm, tn=tn, epilogue=epi,
               extras=[(gq, pl.BlockSpec((1, d), lambda i, j, k: (0, 0)))],
               out_shape=jax.ShapeDtypeStruct((b, g, hpg, t, d), F32),
               out_spec=pl.BlockSpec((1, 1, hpg, tm, d), lambda i, j, k: (i // tiles_per_b, j, 0, i % tiles_per_b, 0)),
               name="nsa_q")[0]


def _ret_kernel(*refs, c, cp, nc, has_s0):
    if has_s0:
        q_ref, k_ref, v_ref, g_ref, cos_ref, sin_ref, mask_ref, qd_ref, kd_ref, cd_ref, s0_ref, o_ref, so_ref, s_ref = refs
    else:
        q_ref, k_ref, v_ref, g_ref, cos_ref, sin_ref, mask_ref, qd_ref, kd_ref, cd_ref, o_ref, so_ref, s_ref = refs
    ci = pl.program_id(2)

    @pl.when(ci == 0)
    def _():
        if has_s0:
            s_ref[...] = s0_ref[0, 0]
        else:
            s_ref[...] = jnp.zeros_like(s_ref)

    def padded(a):
        if cp == c:
            return a
        return jnp.concatenate([a, jnp.zeros((cp - c, a.shape[1]), a.dtype)], axis=0)

    half = RET_HEAD_DIM // 2
    cos, sin = cos_ref[...], sin_ref[...]

    def rot(a):
        a1, a2 = a[:, :half], a[:, half:]
        return jnp.concatenate([a1 * cos - a2 * sin, a1 * sin + a2 * cos], axis=-1)

    q = rot(padded(q_ref[0]))
    k = rot(padded(k_ref[0])) * (RET_HEAD_DIM ** -0.5)
    v = padded(v_ref[0]).astype(BF16)
    s = s_ref[...]
    att = lax.dot_general(q.astype(BF16), k.astype(BF16), (((1,), (1,)), ((), ())),
                          preferred_element_type=F32) * mask_ref[0]
    o = (jnp.dot(att.astype(BF16), v, preferred_element_type=F32)
         + jnp.dot((q * qd_ref[0]).astype(BF16), s.astype(BF16), preferred_element_type=F32))
    kt = jnp.transpose(k * kd_ref[0]).astype(BF16)
    s_new = s * cd_ref[0] + jnp.dot(kt, v, preferred_element_type=F32)
    s_ref[...] = s_new

    @pl.when(ci == nc - 1)
    def _():
        so_ref[0, 0] = s_new

    o = _group_rms(o[:c])
    gate = g_ref[0]
    o_ref[0] = (gate * jax.nn.sigmoid(gate) * o).astype(o_ref.dtype)


def _retention(qkvg, pos, s0):
    b, t, width = qkvg.shape
    dk, dv = RET_HEAD_DIM, RET_V_DIM
    h = width // (2 * dk + 2 * dv)
    c = RET_CHUNK if t % RET_CHUNK == 0 else t
    nc = t // c
    cp = max(c, LANES)
    half = dk // 2
    lg = jnp.log1p(-(2.0 ** (-5.0 - jnp.arange(h, dtype=F32))))
    idx = jnp.arange(c, dtype=F32)
    diff = idx[:, None] - idx[None, :]
    mask = jnp.where(diff >= 0, jnp.exp(jnp.maximum(diff, 0.0)[None] * lg[:, None, None]), 0.0)
    q_dec = jnp.exp((idx + 1.0)[None, :] * lg[:, None])
    k_dec = jnp.exp((c - 1.0 - idx)[None, :] * lg[:, None])
    c_dec = jnp.exp(c * lg)
    mask = jnp.pad(mask, ((0, 0), (0, cp - c), (0, cp - c)))
    q_dec = jnp.pad(q_dec, ((0, 0), (0, cp - c)))[..., None]
    k_dec = jnp.pad(k_dec, ((0, 0), (0, cp - c)))[..., None]
    c_dec = c_dec.reshape(h, 1, 1)
    inv = ROPE_BASE ** (-jnp.arange(half, dtype=F32) / half)
    ang = pos.astype(F32)[:, None] * inv[None, :]
    cos = jnp.pad(jnp.cos(ang), ((0, nc * cp - t), (0, 0)))
    sin = jnp.pad(jnp.sin(ang), ((0, nc * cp - t), (0, 0)))

    vb = (2 * h * dk) // dv
    in_specs = [
        pl.BlockSpec((1, c, dk), lambda bi, hi, ci: (bi, ci, hi)),
        pl.BlockSpec((1, c, dk), lambda bi, hi, ci: (bi, ci, h + hi)),
        pl.BlockSpec((1, c, dv), lambda bi, hi, ci: (bi, ci, vb + hi)),
        pl.BlockSpec((1, c, dv), lambda bi, hi, ci: (bi, ci, vb + h + hi)),
        pl.BlockSpec((cp, half), lambda bi, hi, ci: (ci, 0)),
        pl.BlockSpec((cp, half), lambda bi, hi, ci: (ci, 0)),
        pl.BlockSpec((1, cp, cp), lambda bi, hi, ci: (hi, 0, 0)),
        pl.BlockSpec((1, cp, 1), lambda bi, hi, ci: (hi, 0, 0)),
        pl.BlockSpec((1, cp, 1), lambda bi, hi, ci: (hi, 0, 0)),
        pl.BlockSpec((1, 1, 1), lambda bi, hi, ci: (hi, 0, 0)),
    ]
    args = [qkvg, qkvg, qkvg, qkvg, cos, sin, mask, q_dec, k_dec, c_dec]
    if s0 is not None:
        in_specs.append(pl.BlockSpec((1, 1, dk, dv), lambda bi, hi, ci: (bi, hi, 0, 0)))
        args.append(s0)
    o, s_out = pl.pallas_call(
        functools.partial(_ret_kernel, c=c, cp=cp, nc=nc, has_s0=s0 is not None),
        grid=(b, h, nc),
        in_specs=in_specs,
        out_specs=[pl.BlockSpec((1, c, dv), lambda bi, hi, ci: (bi, ci, hi)),
                   pl.BlockSpec((1, 1, dk, dv), lambda bi, hi, ci: (bi, hi, 0, 0))],
        out_shape=[jax.ShapeDtypeStruct((b, t, h * dv), BF16), jax.ShapeDtypeStruct((b, h, dk, dv), F32)],
        scratch_shapes=[pltpu.VMEM((dk, dv), F32)],
        compiler_params=_params(("parallel", "parallel", "arbitrary")),
        name="retention",
    )(*args)
    return o, s_out


def _router_kernel(h_ref, w_ref, b_ref, ids_ref, gates_ref):
    logits = jnp.dot(h_ref[...].astype(BF16), w_ref[...].astype(BF16), preferred_element_type=F32) + b_ref[...]
    lane = lax.broadcasted_iota(jnp.int32, logits.shape, 1)
    big = jnp.int32(LANES)
    ng, ne = MOE_GROUPS, MOE_EXPERTS_PER_GROUP
    gl = jnp.where(lane < ng, logits, NEG_INF)
    gmax = jnp.max(gl, axis=-1, keepdims=True)
    gsum = jnp.sum(jnp.exp(gl - gmax), axis=-1, keepdims=True)
    g_sel = jnp.min(jnp.where(gl == gmax, lane, big), axis=-1, keepdims=True)
    g_w = 1.0 / gsum
    lo = ng + g_sel * ne
    in_group = (lane >= lo) & (lane < lo + ne)
    el = jnp.where(in_group, logits, NEG_INF)
    emax = jnp.max(el, axis=-1, keepdims=True)
    ee = jnp.exp(el - emax)
    ep = ee / jnp.sum(ee, axis=-1, keepdims=True)
    ep = jnp.where(in_group, ep, -1.0)
    p1 = jnp.max(ep, axis=-1, keepdims=True)
    i1 = jnp.min(jnp.where(ep == p1, lane, big), axis=-1, keepdims=True)
    ep2 = jnp.where(lane == i1, -1.0, ep)
    p2 = jnp.max(ep2, axis=-1, keepdims=True)
    i2 = jnp.min(jnp.where(ep2 == p2, lane, big), axis=-1, keepdims=True)
    psum = p1 + p2
    ids_ref[...] = jnp.where(lane == 0, i1 - ng, jnp.where(lane == 1, i2 - ng, 0))
    gates_ref[...] = jnp.where(lane == 0, g_w * p1 / psum, jnp.where(lane == 1, g_w * p2 / psum, 0.0))


def _router(hf, w_rg, b_rg, w_re, b_re):
    n, d = hf.shape
    ng, ne = MOE_GROUPS, MOE_EXPERTS_PER_GROUP
    w = jnp.concatenate([w_rg.astype(F32), jnp.transpose(w_re.astype(F32), (1, 0, 2)).reshape(d, ng * ne)], axis=1)
    w = jnp.pad(w, ((0, 0), (0, LANES - w.shape[1])))
    bias = jnp.pad(jnp.concatenate([b_rg.astype(F32), b_re.astype(F32).reshape(-1)]), (0, LANES - ng - ng * ne)).reshape(1, LANES)
    tm = _tile(n, 256, SUBLANES)
    ids, gates = pl.pallas_call(
        _router_kernel,
        grid=(n // tm,),
        in_specs=[pl.BlockSpec((tm, d), lambda i: (i, 0)), pl.BlockSpec((d, LANES), lambda i: (0, 0)),
                  pl.BlockSpec((1, LANES), lambda i: (0, 0))],
        out_specs=[pl.BlockSpec((tm, LANES), lambda i: (i, 0)), pl.BlockSpec((tm, LANES), lambda i: (i, 0))],
        out_shape=[jax.ShapeDtypeStruct((n, LANES), jnp.int32), jax.ShapeDtypeStruct((n, LANES), F32)],
        compiler_params=_params(("parallel",)),
        name="moe_router",
    )(hf, w, bias)
    return ids[:, :MOE_TOP], gates[:, :MOE_TOP]


def _row_copy(src, s_row, dst, d_row, sem):
    return pltpu.make_async_copy(src.at[pl.ds(s_row, 1)], dst.at[pl.ds(d_row, 1)], sem)


def _moe_up_kernel(be_ref, nu_ref, tokc_ref, tokn_ref, x_hbm, w_ref, h_ref, xbuf, sem, *, rows, kc):
    b = pl.program_id(0)
    n_used = nu_ref[0]
    slot = lax.rem(b, 2)

    def gather(tok_ref, s):
        def body(r, carry):
            _row_copy(x_hbm, tok_ref[0, 0, r], xbuf.at[s], r, sem.at[s]).start()
            return carry
        lax.fori_loop(0, rows, body, 0, unroll=8)

    @pl.when(b == 0)
    def _():
        gather(tokc_ref, 0)

    @pl.when(b + 1 < n_used)
    def _():
        gather(tokn_ref, 1 - slot)

    @pl.when(b < n_used)
    def _():
        pltpu.make_async_copy(x_hbm.at[pl.ds(0, rows)], xbuf.at[slot], sem.at[slot]).wait()
        d = xbuf.shape[2]
        f = h_ref.shape[1]
        acc = jnp.zeros((rows, 2 * f), F32)
        for c0 in range(0, d, kc):
            acc = acc + jnp.dot(xbuf[slot, :, c0:c0 + kc].astype(BF16), w_ref[0, 0, c0:c0 + kc, :].astype(BF16),
                                preferred_element_type=F32)
        a, g = acc[:, :f], acc[:, f:]
        h_ref[...] = (a * jax.nn.sigmoid(a) * g).astype(h_ref.dtype)

    @pl.when(b >= n_used)
    def _():
        h_ref[...] = jnp.zeros_like(h_ref)


def _moe_down_kernel(be_ref, nu_ref, nv_ref, dstc_ref, dstp_ref, h_ref, g_ref, w_ref, y_hbm, ybuf, sem, *, nb):
    b = pl.program_id(0)
    n_used = nu_ref[0]
    slot = lax.rem(b, 2)

    def scatter(dst_ref, s, count, wait):
        def body(r, carry):
            cp = _row_copy(ybuf.at[s], r, y_hbm, dst_ref[0, 0, r], sem.at[s])
            if wait:
                cp.wait()
            else:
                cp.start()
            return carry
        lax.fori_loop(0, count, body, 0)

    @pl.when(b < n_used)
    def _():
        y = jnp.dot(h_ref[...], w_ref[0, 0].astype(BF16), preferred_element_type=F32) * g_ref[...]
        ybuf[slot] = y
        scatter(dstc_ref, slot, nv_ref[b], False)

    @pl.when((b >= 1) & (b - 1 < n_used))
    def _():
        scatter(dstp_ref, 1 - slot, nv_ref[jnp.maximum(b - 1, 0)], True)

    @pl.when((b == nb - 1) & (b < n_used))
    def _():
        scatter(dstc_ref, slot, nv_ref[b], True)


def _moe(hf, layer, w_rg, b_rg, w_re, b_re, w_up, w_down):
    n, d = hf.shape
    e, f2 = w_up.shape[1], w_up.shape[3]
    f = f2 // 2
    rows = MOE_ROWS
    ids, gates = _router(hf, w_rg, b_rg, w_re, b_re)

    a = n * MOE_TOP
    nb = -(-a // rows) + e
    e_flat = ids.reshape(-1)
    a_idx = jnp.arange(a, dtype=jnp.int32)
    tok_flat = a_idx // MOE_TOP
    _, tok_s, dst_s, gate_s = lax.sort(
        (e_flat, tok_flat, (a_idx % MOE_TOP) * n + tok_flat, lax.bitcast_convert_type(gates.reshape(-1), jnp.int32)),
        num_keys=1, is_stable=True)
    packed = jnp.stack([tok_s, dst_s, gate_s, jnp.zeros_like(tok_s)], axis=1)
    counts = jnp.bincount(e_flat, length=e).astype(jnp.int32)
    starts = jnp.cumsum(counts) - counts
    blocks_per = (counts + rows - 1) // rows
    blk_end = jnp.cumsum(blocks_per)
    first_blk = blk_end - blocks_per
    n_used = blk_end[-1].astype(jnp.int32)
    blk_ids = jnp.arange(nb, dtype=jnp.int32)
    owner = jnp.minimum(jnp.searchsorted(blk_end, blk_ids, side="right"), e - 1).astype(jnp.int32)
    in_e0 = (blk_ids - first_blk[owner]) * rows
    n_valid = jnp.where(blk_ids < n_used, jnp.clip(counts[owner] - in_e0, 0, rows), 0).astype(jnp.int32)
    within = jnp.arange(rows, dtype=jnp.int32)[None, :]
    live = within < n_valid[:, None]
    src = jnp.clip((starts[owner] + in_e0)[:, None] + within, 0, a - 1)
    picked = packed[src]
    tok3 = jnp.where(live, picked[..., 0], 0).reshape(nb, 1, rows)
    dst3 = jnp.where(live, picked[..., 1], 0).reshape(nb, 1, rows)
    gate_buf = jnp.where(live, lax.bitcast_convert_type(picked[..., 2], F32), 0.0)
    block_expert = owner[jnp.minimum(blk_ids, n_used - 1)]
    n_used_arr = n_used.reshape(1)
    smem_blk = functools.partial(pl.BlockSpec, (1, 1, rows), memory_space=pltpu.SMEM)
    h_mid = pl.pallas_call(
        functools.partial(_moe_up_kernel, rows=rows, kc=_tile(d, 512, LANES)),
        grid_spec=pltpu.PrefetchScalarGridSpec(
            num_scalar_prefetch=2,
            grid=(nb,),
            in_specs=[smem_blk(lambda b, be, nu: (b, 0, 0)),
                      smem_blk(lambda b, be, nu: (jnp.minimum(b + 1, nb - 1), 0, 0)),
                      pl.BlockSpec(memory_space=pl.ANY),
                      pl.BlockSpec((1, 1, d, f2), lambda b, be, nu: (layer, be[b], 0, 0))],
            out_specs=pl.BlockSpec((rows, f), lambda b, be, nu: (b, 0)),
            scratch_shapes=[pltpu.VMEM((2, rows, d), F32), pltpu.SemaphoreType.DMA((2,))]),
        out_shape=jax.ShapeDtypeStruct((nb * rows, f), BF16),
        compiler_params=_params(("arbitrary",)),
        name="moe_up",
    )(block_expert, n_used_arr, tok3, tok3, hf, w_up)

    y_tok = pl.pallas_call(
        functools.partial(_moe_down_kernel, nb=nb),
        grid_spec=pltpu.PrefetchScalarGridSpec(
            num_scalar_prefetch=3,
            grid=(nb,),
            in_specs=[smem_blk(lambda b, be, nu, nv: (b, 0, 0)),
                      smem_blk(lambda b, be, nu, nv: (jnp.maximum(b - 1, 0), 0, 0)),
                      pl.BlockSpec((rows, f), lambda b, be, nu, nv: (b, 0)),
                      pl.BlockSpec((rows, 1), lambda b, be, nu, nv: (b, 0)),
                      pl.BlockSpec((1, 1, f, d), lambda b, be, nu, nv: (layer, be[b], 0, 0))],
            out_specs=pl.BlockSpec(memory_space=pl.ANY),
            scratch_shapes=[pltpu.VMEM((2, rows, d), F32), pltpu.SemaphoreType.DMA((2,))]),
        out_shape=jax.ShapeDtypeStruct((MOE_TOP * n, d), F32),
        compiler_params=_params(("arbitrary",)),
        name="moe_down",
    )(block_expert, n_used_arr, n_valid, dst3, dst3, h_mid, gate_buf.reshape(nb * rows, 1), w_down)
    return y_tok


def _page_specs(cache_shape, n_pages, pgs):
    return [pl.BlockSpec((1,) + tuple(cache_shape[1:]),
                         functools.partial(lambda bi, j, pt, r: (pt[bi * n_pages + j * pgs + r], 0, 0, 0, 0), r=r))
            for r in range(pgs)]


def _cmp_paged_kernel(pt_ref, *refs, pgs, g, d):
    page_refs = refs[:pgs]
    perm_ref, wk_ref, wv_ref, uk_ref, uv_ref = refs[pgs:]
    cs = CMP_STRIDE
    page = page_refs[0].shape[1]
    cpp = page // cs
    xs = [[[jnp.dot(perm_ref[...], pr[0, :, kv, gi, :].astype(BF16), preferred_element_type=F32)
            for gi in range(g)] for kv in range(2)] for pr in page_refs]
    m = g * pgs * cpp
    accs = [jnp.zeros((m, wk_ref.shape[1]), F32), jnp.zeros((m, wv_ref.shape[1]), F32)]
    for pp in range(0, cs, 2):
        for kv, w_ref in enumerate((wk_ref, wv_ref)):
            halves = []
            for p in (pp, pp + 1):
                pieces = [xs[r][kv][gi][p * cpp:(p + 1) * cpp, :] for gi in range(g) for r in range(pgs)]
                halves.append(jnp.concatenate(pieces, axis=0))
            lhs = jnp.concatenate(halves, axis=1).astype(BF16)
            accs[kv] = accs[kv] + jnp.dot(lhs, w_ref[pp * d:(pp + 2) * d, :], preferred_element_type=F32)
    uk_ref[0] = accs[0].reshape(g, pgs * cpp, wk_ref.shape[1])
    uv_ref[0] = accs[1].reshape(g, pgs * cpp, wv_ref.shape[1])


def _cmp_hidden_paged(cache, page_table, wcat_k, wcat_v):
    page = cache.shape[1]
    g, d = cache.shape[3], cache.shape[4]
    b, n_pages = page_table.shape
    pgs = _tile(n_pages, 8, 1)
    cpp = page // CMP_STRIDE
    nch = n_pages * cpp
    perm = np.zeros((page, page), np.float32)
    for c in range(cpp):
        for p in range(CMP_STRIDE):
            perm[p * cpp + c, c * CMP_STRIDE + p] = 1.0
    hid2 = wcat_k.shape[1]
    const = lambda shape: pl.BlockSpec(shape, lambda bi, j, pt: (0, 0))
    uk, uv = pl.pallas_call(
        functools.partial(_cmp_paged_kernel, pgs=pgs, g=g, d=d),
        grid_spec=pltpu.PrefetchScalarGridSpec(
            num_scalar_prefetch=1,
            grid=(b, n_pages // pgs),
            in_specs=_page_specs(cache.shape, n_pages, pgs)
            + [const((page, page)), const(wcat_k.shape), const(wcat_v.shape)],
            out_specs=[pl.BlockSpec((1, g, pgs * cpp, hid2), lambda bi, j, pt: (bi, 0, j, 0))] * 2),
        out_shape=[jax.ShapeDtypeStruct((b, g, nch, hid2), F32)] * 2,
        compiler_params=_params(("parallel", "parallel")),
        name="cmp_hidden_paged",
    )(page_table.reshape(-1).astype(jnp.int32), *([cache] * pgs), jnp.asarray(perm, BF16),
      wcat_k.astype(BF16), wcat_v.astype(BF16))
    return uk.reshape(b * g, nch, hid2), uv.reshape(b * g, nch, hid2)
def _cmp_post_kernel(u_ref, pe_ref, w2_ref, g_ref, o_ref, *, n_cmp, norm):
    u = u_ref[0]
    nch, hid2 = u.shape
    hid = hid2 // 2
    nxt = pltpu.roll(u[:, hid:], nch - 1, axis=0)
    x = (pe_ref[0:1, :] + u[:, :hid]) + nxt
    y = 0.5 * x * (1.0 + jnp.tanh(0.7978845608028654 * (x + 0.044715 * (x * x * x))))
    z = jnp.dot(y.astype(BF16), w2_ref[...].astype(BF16), preferred_element_type=F32)
    if norm:
        z = _group_rms(z) * g_ref[...]
    row = lax.broadcasted_iota(jnp.int32, z.shape, 0)
    o_ref[0] = jnp.where(row < n_cmp, z, 0.0)


def _cmp_wcat(w1):
    r = CMP_LEN // CMP_STRIDE
    assert r == 2
    w1r = w1.reshape(r, w1.shape[0] // r, w1.shape[1])
    return jnp.concatenate([w1r[0], w1r[1]], axis=1)


def _cmp_hidden_dense(rows, nch, wcat):
    b, _, g, d = rows.shape
    ch = rows[:, :nch * CMP_STRIDE].reshape(b, nch, CMP_STRIDE, g, d)
    ch = jnp.transpose(ch, (0, 3, 1, 2, 4)).reshape(b * g * nch, CMP_STRIDE * d).astype(BF16)
    return _mm(ch, wcat, name="cmp_hidden")[0].reshape(b * g, nch, wcat.shape[1])


def _cmp_finish(u, n_cmp, pe, w1, w2, g_k):
    bg, nch, hid2 = u.shape
    hid = hid2 // 2
    d = w2.shape[1]
    pe_rows = jnp.pad(pe.reshape(1, -1), ((0, SUBLANES - 1), (0, 0)))
    pe_hid = _mm(pe_rows, w1, name="cmp_pe")[0]
    gain = (jnp.ones((d,), F32) if g_k is None else g_k.astype(F32)).reshape(1, d)
    return pl.pallas_call(
        functools.partial(_cmp_post_kernel, n_cmp=n_cmp, norm=g_k is not None),
        grid=(bg,),
        in_specs=[pl.BlockSpec((1, nch, hid2), lambda i: (i, 0, 0)), pl.BlockSpec((SUBLANES, hid), lambda i: (0, 0)),
                  pl.BlockSpec((hid, d), lambda i: (0, 0)), pl.BlockSpec((1, d), lambda i: (0, 0))],
        out_specs=pl.BlockSpec((1, nch, d), lambda i: (i, 0, 0)),
        out_shape=jax.ShapeDtypeStruct((bg, nch, d), F32),
        compiler_params=_params(("parallel",)),
        name="cmp_post",
    )(u, pe_hid, w2, gain)


def _slc_map(n_cmp, n_slc, rows, cols):
    a = SLC_BLOCK // CMP_STRIDE
    bb = CMP_LEN // CMP_STRIDE
    j = np.arange(n_slc)[:, None, None]
    i = j * a + np.arange(a)[None, :, None] + np.arange(bb)[None, None, :] - bb + 1
    i, jj = np.broadcast_arrays(i, j)
    ok = (i >= 0) & (i < n_cmp)
    m = np.zeros((rows, cols), np.float32)
    np.add.at(m, (i[ok], jj[ok]), 1.0)
    return jnp.asarray(m)


def _masked_softmax_rows(s):
    m = jnp.max(s, axis=-1, keepdims=True)
    e = jnp.exp(s - jnp.where(m > NEG_INF, m, 0.0))
    den = jnp.sum(e, axis=-1, keepdims=True)
    return e / jnp.where(den > 0, den, 1.0)


def _store_gated(o, gl_ref, prev_ref, o_ref, branch, hpg, tq):
    d = NSA_HEAD_DIM
    gate = jax.nn.sigmoid(gl_ref[0])
    for hh in range(hpg):
        c = hh * N_BRANCH + branch
        val = gate[:, c:c + 1] * o[hh * tq:(hh + 1) * tq]
        if prev_ref is not None:
            val = prev_ref[0, :, hh * d:(hh + 1) * d] + val
        o_ref[0, :, hh * d:(hh + 1) * d] = val.astype(o_ref.dtype)


def _nsa_cmp_kernel(q_ref, kc_ref, vc_ref, map_ref, gl_ref, o_ref, sel_ref, *, hpg, tq, pos0, n_cmp, n_slc, n_top):
    i = pl.program_id(2)
    rws = hpg * tq
    q = q_ref[0, 0].reshape(rws, NSA_HEAD_DIM).astype(BF16)
    kc = kc_ref[0].astype(BF16)
    s = lax.dot_general(q, kc, (((1,), (1,)), ((), ())), preferred_element_type=F32) * NSA_SCALE
    ncp = s.shape[1]
    tok = lax.broadcasted_iota(jnp.int32, (rws, 1), 0) & (tq - 1)
    pos = pos0 + i * tq + tok
    cidx = lax.broadcasted_iota(jnp.int32, (1, ncp), 1)
    ok = (cidx * CMP_STRIDE + (CMP_LEN - 1) <= pos) & (cidx < n_cmp)
    p = _masked_softmax_rows(jnp.where(ok, s, NEG_INF))
    o = jnp.dot(p.astype(BF16), vc_ref[0].astype(BF16), preferred_element_type=F32)
    _store_gated(o, gl_ref, None, o_ref, 0, hpg, tq)

    psum = jnp.sum(p.astype(BF16).astype(F32).reshape(hpg, tq, ncp), axis=0)
    imp = jnp.dot(psum, map_ref[...], precision=lax.Precision.HIGHEST, preferred_element_type=F32)
    nsp = imp.shape[1]
    posq = pos0 + i * tq + lax.broadcasted_iota(jnp.int32, (tq, 1), 0)
    blk = lax.broadcasted_iota(jnp.int32, (tq, nsp), 1)
    back = (posq >> SLC_SHIFT) - blk
    real = blk < n_slc
    valid = (blk * SLC_BLOCK <= posq) & real
    forced = (blk == 0) | ((back >= 0) & (back < SLC_LOCAL))
    score = jnp.where(valid, imp + jnp.where(forced, FORCE_BONUS, 0.0), NEG_INF)
    rank = jnp.zeros((tq, nsp), jnp.int32)
    for kb in range(n_slc):
        col = score[:, kb:kb + 1]
        ahead = (col > score) | ((col == score) & (blk > kb))
        rank = rank + ahead.astype(jnp.int32)
    sel_ref[0, 0] = jnp.where((rank < n_top) & real, 1.0, 0.0)


def _nsa_cmp(qn, kc, vc, glog, pos0, n_cmp, n_slc):
    b, g, hpg, t, d = qn.shape
    ncp = kc.shape[1]
    nsp = -(-n_slc // LANES) * LANES
    tq = _tile(t, 128, SUBLANES)
    smap = _slc_map(n_cmp, n_slc, ncp, nsp)
    n_top = min(SLC_TOP, n_slc)
    return pl.pallas_call(
        functools.partial(_nsa_cmp_kernel, hpg=hpg, tq=tq, pos0=pos0, n_cmp=n_cmp, n_slc=n_slc, n_top=n_top),
        grid=(b, g, t // tq),
        in_specs=[pl.BlockSpec((1, 1, hpg, tq, d), lambda bi, gi, i: (bi, gi, 0, i, 0)),
                  pl.BlockSpec((1, ncp, d), lambda bi, gi, i: (bi * g + gi, 0, 0)),
                  pl.BlockSpec((1, ncp, d), lambda bi, gi, i: (bi * g + gi, 0, 0)),
                  pl.BlockSpec((ncp, nsp), lambda bi, gi, i: (0, 0)),
                  pl.BlockSpec((1, tq, LANES), lambda bi, gi, i: (bi, i, gi))],
        out_specs=[pl.BlockSpec((1, tq, hpg * d), lambda bi, gi, i: (bi, i, gi)),
                   pl.BlockSpec((1, 1, tq, nsp), lambda bi, gi, i: (bi, gi, i, 0))],
        out_shape=[jax.ShapeDtypeStruct((b, t, g * hpg * d), F32), jax.ShapeDtypeStruct((b, g, t, nsp), F32)],
        compiler_params=_params(("parallel", "parallel", "parallel")),
        name="nsa_cmp",
    )(qn, kc, vc, smap, glog)


def _nsa_attn_kernel(*refs, hpg, tq, tk, n_kt, qpos0, kpos0, window, use_sel, branch):
    refs = list(refs)
    q_ref, k_ref, v_ref = refs[:3]
    refs = refs[3:]
    sel_ref = None
    if use_sel:
        sel_ref = refs[0]
        refs = refs[1:]
    gl_ref, prev_ref, o_ref, qs_ref, s_ref, p_ref, bias_ref, m_ref, a_ref, acc_ref = refs
    i = pl.program_id(2)
    d = NSA_HEAD_DIM
    rws = hpg * tq
    rb = min(tq, 64)
    per_head = tq // rb
    pvb = min(rws, 512)
    qs_ref[...] = (q_ref[0, 0].reshape(rws, d) * NSA_SCALE).astype(BF16)
    posq = qpos0 + i * tq + lax.broadcasted_iota(jnp.int32, (tq, 1), 0)
    sel = sel_ref[0, 0].astype(BF16) if use_sel else None
    m_ref[...] = jnp.full_like(m_ref, NEG_INF)
    acc_ref[...] = jnp.zeros_like(acc_ref)
    ones = jnp.ones((tk, d), BF16)

    def body(kt, carry):
        key0 = kt * tk
        off = pl.multiple_of(key0, tk)
        k = k_ref[0, pl.ds(off, tk), :].astype(BF16)
        v1 = jnp.concatenate([v_ref[0, pl.ds(off, tk), :].astype(BF16), ones], axis=1)
        s_ref[...] = lax.dot_general(qs_ref[...], k, (((1,), (1,)), ((), ())), preferred_element_type=F32)
        kpos = kpos0 + key0 + lax.broadcasted_iota(jnp.int32, (1, tk), 1)
        ok = kpos <= posq
        if window is not None:
            ok = ok & (posq - kpos < window)
        if use_sel:
            nsp = sel.shape[1]
            kblk = (key0 + lax.broadcasted_iota(jnp.int32, (nsp, tk), 1)) >> SLC_SHIFT
            expand = (kblk == lax.broadcasted_iota(jnp.int32, (nsp, tk), 0)).astype(BF16)
            ok = ok & (jnp.dot(sel, expand, preferred_element_type=F32) > 0.5)
        bias_ref[...] = jnp.where(ok, 0.0, NEG_INF)
        for blk in range(rws // rb):
            rows = slice(blk * rb, (blk + 1) * rb)
            part = blk % per_head
            s = s_ref[rows, :] + bias_ref[part * rb:(part + 1) * rb, :]
            m_old = m_ref[rows, :]
            m_new = jnp.maximum(m_old, jnp.max(s, axis=-1, keepdims=True))
            m_safe = jnp.where(m_new > NEG_INF, m_new, 0.0)
            p_ref[rows, :] = jnp.exp(s - jnp.tile(m_safe, (1, tk // LANES))).astype(BF16)
            a_ref[rows, :] = jnp.exp(m_old - m_safe)
            m_ref[rows, :] = m_new
        for r0 in range(0, rws, pvb):
            rows = slice(r0, r0 + pvb)
            pv = jnp.dot(p_ref[rows, :], v1, preferred_element_type=F32)
            acc_ref[rows, :] = jnp.tile(a_ref[rows, :], (1, 2)) * acc_ref[rows, :] + pv
        return carry

    q_lo = qpos0 + i * tq
    q_hi = q_lo + tq - 1
    hi = jnp.clip((q_hi - kpos0) // tk + 1, 0, n_kt)
    if window is None:
        lo = 0
    else:
        lo = jnp.clip((q_lo - (window - 1) - kpos0) // tk, 0, n_kt)
    lax.fori_loop(lo, hi, body, 0)
    l = acc_ref[:, d:]
    o = acc_ref[:, :d] / jnp.where(l > 0, l, 1.0)
    _store_gated(o, gl_ref, prev_ref, o_ref, branch, hpg, tq)


def _nsa_attn(qn, kv, glog, prev, *, qpos0, kpos0, branch, out_dtype, window=None, sel=None, tk=512):
    b, g, hpg, t, d = qn.shape
    tk_total = kv.shape[1]
    tq = _tile(t, 256, SUBLANES)
    tk = _tile(tk_total, tk, LANES)
    n_kt = tk_total // tk
    in_specs = [pl.BlockSpec((1, 1, hpg, tq, d), lambda bi, gi, i: (bi, gi, 0, i, 0)),
                pl.BlockSpec((1, tk_total, d), lambda bi, gi, i: (bi, 0, gi)),
                pl.BlockSpec((1, tk_total, d), lambda bi, gi, i: (bi, 0, g + gi))]
    args = [qn, kv, kv]
    if sel is not None:
        nsp = sel.shape[-1]
        in_specs.append(pl.BlockSpec((1, 1, tq, nsp), lambda bi, gi, i: (bi, gi, i, 0)))
        args.append(sel)
    in_specs += [pl.BlockSpec((1, tq, LANES), lambda bi, gi, i: (bi, i, gi)),
                 pl.BlockSpec((1, tq, hpg * d), lambda bi, gi, i: (bi, i, gi))]
    args += [glog, prev]
    rws = hpg * tq
    return pl.pallas_call(
        functools.partial(_nsa_attn_kernel, hpg=hpg, tq=tq, tk=tk, n_kt=n_kt, qpos0=qpos0, kpos0=kpos0,
                          window=window, use_sel=sel is not None, branch=branch),
        grid=(b, g, t // tq),
        in_specs=in_specs,
        out_specs=pl.BlockSpec((1, tq, hpg * d), lambda bi, gi, i: (bi, i, gi)),
        out_shape=jax.ShapeDtypeStruct((b, t, g * hpg * d), out_dtype),
        scratch_shapes=[pltpu.VMEM((rws, d), BF16), pltpu.VMEM((rws, tk), F32), pltpu.VMEM((rws, tk), BF16),
                        pltpu.VMEM((tq, tk), F32), pltpu.VMEM((rws, LANES), F32), pltpu.VMEM((rws, LANES), F32),
                        pltpu.VMEM((rws, 2 * d), F32)],
        compiler_params=_params(("parallel", "parallel", "parallel")),
        name="nsa_attn_%d" % branch,
    )(*args)


def _nsa_slc_paged_kernel(pt_ref, *refs, pgs, g, hpg, tq, qpos0, past_len, n_steps):
    page_refs = refs[:pgs]
    q_ref, sel_ref, tail_ref, gl_ref, prev_ref, o_ref, m_ref, l_ref, acc_ref = refs[pgs:]
    j = pl.program_id(1)
    d = NSA_HEAD_DIM
    rws = hpg * tq
    page = page_refs[0].shape[1]
    nsp = sel_ref.shape[-1]
    posq = qpos0 + lax.broadcasted_iota(jnp.int32, (tq, 1), 0)
    sel_all = sel_ref[0].reshape(g * tq, nsp).astype(BF16)

    @pl.when(j == 0)
    def _():
        m_ref[...] = jnp.full_like(m_ref, NEG_INF)
        l_ref[...] = jnp.zeros_like(l_ref)
        acc_ref[...] = jnp.zeros_like(acc_ref)

    def update(plane, n, key0):
        ok_pos = key0 + lax.broadcasted_iota(jnp.int32, (1, n), 1) <= posq
        kblk = (key0 + lax.broadcasted_iota(jnp.int32, (nsp, n), 1)) >> SLC_SHIFT
        expand = (kblk == lax.broadcasted_iota(jnp.int32, (nsp, n), 0)).astype(BF16)
        picked = jnp.dot(sel_all, expand, preferred_element_type=F32) > 0.5
        for gi in range(g):
            k = plane(0, gi).astype(BF16)
            v = plane(1, gi).astype(BF16)
            q = q_ref[0, gi].reshape(rws, d).astype(BF16)
            s = lax.dot_general(q, k, (((1,), (1,)), ((), ())), preferred_element_type=F32) * NSA_SCALE
            ok = ok_pos & picked[gi * tq:(gi + 1) * tq]
            s = jnp.where(ok[None], s.reshape(hpg, tq, n), NEG_INF).reshape(rws, n)
            m_old = m_ref[gi]
            m_new = jnp.maximum(m_old, jnp.max(s, axis=-1, keepdims=True))
            m_safe = jnp.where(m_new > NEG_INF, m_new, 0.0)
            p = jnp.exp(s - m_safe)
            alpha = jnp.exp(m_old - m_safe)
            l_ref[gi] = alpha * l_ref[gi] + jnp.sum(p, axis=-1, keepdims=True)
            acc_ref[gi] = alpha * acc_ref[gi] + jnp.dot(p.astype(BF16), v, preferred_element_type=F32)
            m_ref[gi] = m_new

    update(lambda kv, gi: jnp.concatenate([pr[0, :, kv, gi, :] for pr in page_refs], axis=0),
           pgs * page, j * (pgs * page))

    @pl.when(j == n_steps - 1)
    def _():
        update(lambda kv, gi: tail_ref[0, :, (kv * g + gi) * d:(kv * g + gi + 1) * d], tail_ref.shape[1], past_len)
        gate = jax.nn.sigmoid(gl_ref[0])
        for gi in range(g):
            l = l_ref[gi]
            o = acc_ref[gi] / jnp.where(l > 0, l, 1.0)
            for hh in range(hpg):
                c = gi * LANES + hh * N_BRANCH + 1
                col = (gi * hpg + hh) * d
                o_ref[0, :, col:col + d] = prev_ref[0, :, col:col + d] + gate[:, c:c + 1] * o[hh * tq:(hh + 1) * tq]


def _nsa_slc_paged(qn, cache, page_table, sel, tail, glog, prev, *, qpos0):
    b, g, hpg, t, d = qn.shape
    page = cache.shape[1]
    width = 2 * g * d
    n_pages = page_table.shape[1]
    pgs = _tile(n_pages, 8, 1)
    n_steps = n_pages // pgs
    nsp = sel.shape[-1]
    nt = tail.shape[1]
    rws = hpg * t
    return pl.pallas_call(
        functools.partial(_nsa_slc_paged_kernel, pgs=pgs, g=g, hpg=hpg, tq=t, qpos0=qpos0,
                          past_len=n_pages * page, n_steps=n_steps),
        grid_spec=pltpu.PrefetchScalarGridSpec(
            num_scalar_prefetch=1,
            grid=(b, n_steps),
            in_specs=_page_specs(cache.shape, n_pages, pgs)
            + [pl.BlockSpec((1, g, hpg, t, d), lambda bi, j, pt: (bi, 0, 0, 0, 0)),
               pl.BlockSpec((1, g, t, nsp), lambda bi, j, pt: (bi, 0, 0, 0)),
               pl.BlockSpec((1, nt, width), lambda bi, j, pt: (bi, 0, 0)),
               pl.BlockSpec((1, t, g * LANES), lambda bi, j, pt: (bi, 0, 0)),
               pl.BlockSpec((1, t, g * hpg * d), lambda bi, j, pt: (bi, 0, 0))],
            out_specs=pl.BlockSpec((1, t, g * hpg * d), lambda bi, j, pt: (bi, 0, 0)),
            scratch_shapes=[pltpu.VMEM((g, rws, 1), F32), pltpu.VMEM((g, rws, 1), F32), pltpu.VMEM((g, rws, d), F32)]),
        out_shape=jax.ShapeDtypeStruct((b, t, g * hpg * d), F32),
        compiler_params=_params(("parallel", "arbitrary")),
        name="nsa_slc_paged",
    )(page_table.reshape(-1).astype(jnp.int32), *([cache] * pgs), qn, sel, tail, glog, prev)


def _gate_weight(w_in, d_model):
    g = NSA_KV_HEADS
    hpg = d_model // NSA_HEAD_DIM // g
    wg = w_in[:, d_model:].reshape(d_model, g, hpg * N_BRANCH)
    wg = jnp.pad(wg, ((0, 0), (0, 0), (0, LANES - hpg * N_BRANCH)))
    return wg.reshape(d_model, g * LANES)


def kernel(x_prompt, x_sample, state_ret, cache_cmp_kv, cache_slc_kv, cache_win_kv, page_table, p_prompt, p_sample, g_mix, g_ffn, w_ret_in, w_ret_out, w_nsa_in, g_nsa_q, w_nsa_out, g_kv, w_kv, g_k_cmp, g_k_slc, g_k_win, pe_cmp_k, w_cmp_k1, w_cmp_k2, pe_cmp_v, w_cmp_v1, w_cmp_v2, w_rg, b_rg, w_re, b_re, w_moe_up, w_moe_down, w_ple_up, g_ple, w_ple_gate):
    depth = g_mix.shape[0]
    n_a = w_ret_in.shape[0]
    g, d = NSA_KV_HEADS, NSA_HEAD_DIM
    gd = g * d
    d_model = x_prompt.shape[-1]
    page = cache_cmp_kv.shape[1]
    past_len = page_table.shape[1] * page

    groups = [
        dict(x=x_prompt.reshape(-1, d_model), p=p_prompt, b=x_prompt.shape[0], t=x_prompt.shape[1], pos0=0, s0=None),
        dict(x=x_sample.reshape(-1, d_model), p=p_sample, b=x_sample.shape[0], t=x_sample.shape[1], pos0=past_len, s0=state_ret),
    ]
    n_rows = [gr["x"].shape[0] for gr in groups]
    n_tok = sum(n_rows)
    offs = [0, n_rows[0]]
    for gr in groups:
        gr["ret"] = []

    for i in range(depth):
        for gr in groups:
            b, t = gr["b"], gr["t"]
            h = _rms([(gr["x"], 0)], g_mix[i], [BF16])[0]
            if i < n_a:
                qkvg = _mm(h, w_ret_in[i], tm=2048, name="ret_in")[0]
                pos = gr["pos0"] + jnp.arange(t)
                s0 = None if gr["s0"] is None else gr["s0"][i]
                o, s_new = _retention(qkvg.reshape(b, t, -1), pos, s0)
                gr["ret"].append(s_new)
                gr["x"] = _mm_resid(o.reshape(b * t, -1), w_ret_out[i], gr["x"], "ret_out")
            else:
                j = i - n_a
                qn = _mm_q(h, w_nsa_in[j], g_nsa_q[j], b, t)
                glog = _mm(h, _gate_weight(w_nsa_in[j], d_model), name="nsa_gate")[0].reshape(b, t, g * LANES)
                ctx = gr["ctx"]
                o1, sel = _nsa_cmp(qn, ctx["k_c"], ctx["v_c"], glog, gr["pos0"], ctx["n_cmp"], ctx["n_slc"])
                if ctx["slc_tail"] is None:
                    o2 = _nsa_attn(qn, ctx["slc"], glog, o1, qpos0=gr["pos0"], kpos0=0, branch=1, out_dtype=F32, sel=sel)
                else:
                    o2 = _nsa_slc_paged(qn, cache_slc_kv, page_table, sel, ctx["slc_tail"], glog, o1, qpos0=gr["pos0"])
                o3 = _nsa_attn(qn, ctx["win"], glog, o2, qpos0=gr["pos0"], kpos0=ctx["win_pos0"], branch=2,
                               out_dtype=BF16, window=WINDOW)
                gr["x"] = _mm_resid(o3.reshape(b * t, -1), w_nsa_out[j], gr["x"], "nsa_out")

        hf = jnp.concatenate([_rms([(gr["x"], 0)], g_ffn[i], [F32])[0] for gr in groups], axis=0)
        y_tok = _moe(hf, i, w_rg[i], b_rg[i], w_re[i], b_re[i], w_moe_up, w_moe_down)

        for gi, gr in enumerate(groups):
            rows = n_rows[gi]
            x_new, hp = _rms([(gr["x"], 0), (y_tok, offs[gi]), (y_tok, n_tok + offs[gi])], g_ple[i], [BF16],
                             want_sum=True, rows=rows, tm=64)
            gr["x"] = _mm_ple(hp, w_ple_gate, i, x_new, gr["p"][i].reshape(rows, -1), w_ple_up)

        if i == n_a - 1:
            for gi, gr in enumerate(groups):
                b, t = gr["b"], gr["t"]
                hk = _rms([(gr["x"], 0)], g_kv, [BF16])[0]
                kv = _mm_kv(hk, w_kv, g_k_slc, g_k_win).reshape(b, t, 2 * N_BRANCH * gd)
                cmp_new, slc_new, win_new = kv[..., :2 * gd], kv[..., 2 * gd:4 * gd], kv[..., 4 * gd:]
                gr["cmp_new"], gr["slc_new"] = cmp_new, slc_new
                wcat_k, wcat_v = _cmp_wcat(w_cmp_k1), _cmp_wcat(w_cmp_v1)
                n_keys = gr["pos0"] + t
                n_cmp = (n_keys - CMP_LEN) // CMP_STRIDE + 1
                nch = n_cmp + CMP_LEN // CMP_STRIDE - 1
                if gi == 0:
                    slc_tail = None
                    win_keys, win_pos0 = win_new, 0
                    gr["win_state"] = win_new[:, t - min(WINDOW, t):]
                    cmp_rows = cmp_new.reshape(b, t, 2, g, d)
                    u_k = _cmp_hidden_dense(cmp_rows[:, :, 0], nch, wcat_k)
                    u_v = _cmp_hidden_dense(cmp_rows[:, :, 1], nch, wcat_v)
                else:
                    slc_tail = jnp.pad(slc_new, ((0, 0), (0, LANES - t), (0, 0)))
                    w_buf = cache_win_kv.shape[1]
                    win_all = jnp.concatenate([cache_win_kv.reshape(b, w_buf, 2 * gd), win_new], axis=1)
                    n_all = w_buf + t
                    gr["win_state"] = win_all[:, n_all - min(WINDOW, past_len + t):]
                    win_keys = jnp.pad(win_all, ((0, 0), (0, -n_all % LANES), (0, 0)))
                    win_pos0 = past_len - w_buf
                    assert nch * CMP_STRIDE == past_len
                    u_k, u_v = _cmp_hidden_paged(cache_cmp_kv, page_table, wcat_k, wcat_v)
                k_c = _cmp_finish(u_k, n_cmp, pe_cmp_k, w_cmp_k1, w_cmp_k2, g_k_cmp)
                v_c = _cmp_finish(u_v, n_cmp, pe_cmp_v, w_cmp_v1, w_cmp_v2, None)
                gr["ctx"] = dict(k_c=k_c, v_c=v_c, n_cmp=n_cmp, n_slc=-(-n_keys // SLC_BLOCK), slc=slc_new,
                                 slc_tail=slc_tail, win=win_keys, win_pos0=win_pos0)

    outs = []
    for gr in groups:
        outs.append(gr["x"].reshape(gr["b"], gr["t"], d_model))
    rets = [jnp.stack(gr["ret"]) for gr in groups]
    kvs = []
    for name in ("cmp_new", "slc_new", "win_state"):
        for gr in groups:
            a = gr[name]
            kvs.append(a.reshape(a.shape[0], a.shape[1], 2, g, d))
    return (outs[0], outs[1], rets[0], rets[1], kvs[0], kvs[1], kvs[2], kvs[3], kvs[4], kvs[5])
```

```python
import functools

import numpy as np
import jax
import jax.numpy as jnp
from jax import lax
from jax.experimental import pallas as pl
from jax.experimental.pallas import tpu as pltpu

F32 = jnp.float32
BF16 = jnp.bfloat16

RET_HEAD_DIM = 256
RET_V_DIM = 2 * RET_HEAD_DIM
RET_CHUNK = 128
ROPE_BASE = 10000.0
NSA_HEAD_DIM = 128
NSA_KV_HEADS = 4
NSA_SCALE = NSA_HEAD_DIM ** -0.5
CMP_LEN = 32
CMP_STRIDE = 16
SLC_BLOCK = 64
SLC_SHIFT = 6
SLC_TOP = 16
SLC_LOCAL = 2
FORCE_BONUS = 1e4
WINDOW = 512
N_BRANCH = 3
MOE_GROUPS = 8
MOE_EXPERTS_PER_GROUP = 8
MOE_TOP = 2
EPS = 1e-6

LANES = 128
SUBLANES = 8
VMEM_LIMIT_BYTES = 56 * 1024 * 1024
MOE_ROWS = 256
NEG_INF = float("-inf")


def _tile(n, pref, align):
    best = None
    for t in range(align, min(n, pref) + 1, align):
        if n % t == 0:
            best = t
    return n if best is None else best


def _params(semantics):
    return pltpu.CompilerParams(dimension_semantics=semantics, vmem_limit_bytes=VMEM_LIMIT_BYTES)


def _rms_kernel(*refs, n_add, want_sum, planes):
    adds, g_ref, outs = refs[:n_add], refs[n_add], refs[n_add + 1:]
    x = adds[0][...]
    for r in adds[1:]:
        x = x + r[...]
    y = (x * lax.rsqrt(jnp.mean(x * x, axis=-1, keepdims=True) + EPS)) * g_ref[...]
    if want_sum:
        outs[0][...] = x
        outs = outs[1:]
    if planes:
        for j in range(y.shape[1] // LANES):
            outs[-1][:, j, :] = y[:, j * LANES:(j + 1) * LANES]
        outs = outs[:-1]
    for o in outs:
        o[...] = y.astype(o.dtype)


def _rms(addends, g, out_dtypes, want_sum=False, rows=None, tm=128, planes=False):
    d = addends[0][0].shape[1]
    rows = addends[0][0].shape[0] if rows is None else rows
    tm = _tile(rows, tm, SUBLANES)
    for _, off in addends:
        assert off % tm == 0
    in_specs = [pl.BlockSpec((tm, d), functools.partial(lambda i, o: (i + o, 0), o=off // tm)) for _, off in addends]
    in_specs.append(pl.BlockSpec((1, d), lambda i: (0, 0)))
    dts = ([F32] if want_sum else []) + list(out_dtypes)
    out_specs = [pl.BlockSpec((tm, d), lambda i: (i, 0)) for _ in dts]
    out_shape = [jax.ShapeDtypeStruct((rows, d), dt) for dt in dts]
    if planes:
        out_specs.append(pl.BlockSpec((tm, d // LANES, LANES), lambda i: (i, 0, 0)))
        out_shape.append(jax.ShapeDtypeStruct((rows, d // LANES, LANES), F32))
    outs = pl.pallas_call(
        functools.partial(_rms_kernel, n_add=len(addends), want_sum=want_sum, planes=planes),
        grid=(rows // tm,),
        in_specs=in_specs,
        out_specs=out_specs,
        out_shape=out_shape,
        compiler_params=_params(("parallel",)),
        name="rms",
    )(*[a for a, _ in addends], g.reshape(1, d).astype(F32))
    return outs


def _mm_kernel(x_ref, w_ref, *rest, n_extra, epilogue, nk):
    extras, o_ref, acc_ref = rest[:n_extra], rest[n_extra], rest[n_extra + 1]
    k = pl.program_id(2)

    @pl.when(k == 0)
    def _():
        acc_ref[...] = jnp.zeros_like(acc_ref)

    w = w_ref[0] if len(w_ref.shape) == 3 else w_ref[...]
    acc_ref[...] += jnp.dot(x_ref[...].astype(BF16), w.astype(BF16), preferred_element_type=F32)

    @pl.when(k == nk - 1)
    def _():
        acc = acc_ref[...]
        if epilogue is None:
            o_ref[...] = acc.astype(o_ref.dtype)
        else:
            epilogue(acc, extras, o_ref)


def _mm(x, w, *, layer=None, n_out=None, out_dtype=F32, tm=1024, tn=1024, tk=512, epilogue=None, extras=(),
        out_shape=None, out_spec=None, name="mm"):
    m, kdim = x.shape
    n = w.shape[-1] if n_out is None else n_out
    if m <= 256:
        tk = 2048
    tm, tn, tk = _tile(m, tm, SUBLANES), _tile(n, tn, LANES), _tile(kdim, tk, LANES)
    nk = kdim // tk
    if out_shape is None:
        out_shape = jax.ShapeDtypeStruct((m, n), out_dtype)
        out_spec = pl.BlockSpec((tm, tn), lambda i, j, k: (i, j))
    return pl.pallas_call(
        functools.partial(_mm_kernel, n_extra=len(extras), epilogue=epilogue, nk=nk),
        grid=(m // tm, n // tn, nk),
        in_specs=[pl.BlockSpec((tm, tk), lambda i, j, k: (i, k)),
                  pl.BlockSpec((tk, tn), lambda i, j, k: (k, j)) if layer is None
                  else pl.BlockSpec((1, tk, tn), lambda i, j, k: (layer, k, j))]
        + [s for _, s in extras],
        out_specs=out_spec,
        out_shape=out_shape,
        scratch_shapes=[pltpu.VMEM((tm, tn), F32)],
        compiler_params=_params(("parallel", "parallel", "arbitrary")),
        name=name,
    )(x, w, *[a for a, _ in extras]), (tm, tn)


def _tile_spec(tm, tn):
    return pl.BlockSpec((tm, tn), lambda i, j, k: (i, j))


def _mm_resid_kernel(x_ref, w_ref, r_ref, o_ref):
    @pl.when(pl.program_id(2) == 0)
    def _():
        o_ref[...] = r_ref[...]

    o_ref[...] += jnp.dot(x_ref[...].astype(BF16), w_ref[...].astype(BF16), preferred_element_type=F32)


def _mm_resid(x, w, resid, name):
    m, n = resid.shape
    kdim = x.shape[1]
    tm, tn = _tile(m, 2048, SUBLANES), _tile(n, 1024, LANES)
    tk = _tile(kdim, 2048 if m <= 256 else 512, LANES)
    return pl.pallas_call(
        _mm_resid_kernel,
        grid=(m // tm, n // tn, kdim // tk),
        in_specs=[pl.BlockSpec((tm, tk), lambda i, j, k: (i, k)), pl.BlockSpec((tk, tn), lambda i, j, k: (k, j)),
                  _tile_spec(tm, tn)],
        out_specs=_tile_spec(tm, tn),
        out_shape=jax.ShapeDtypeStruct((m, n), F32),
        compiler_params=_params(("parallel", "parallel", "arbitrary")),
        name=name,
    )(x, w, resid)


def _mm_ple(h, w_gate, layer, x, p, w_up):
    m, n = x.shape
    pdim = p.shape[1]
    tm, tn = _tile(m, 1024, SUBLANES), _tile(n, 1024, LANES)

    def epi(acc, extras, o_ref):
        pu = jnp.dot(extras[1][...].astype(BF16), extras[2][0].astype(BF16), preferred_element_type=F32)
        o_ref[...] = extras[0][...] + pu * jax.nn.sigmoid(acc)

    return _mm(h, w_gate, layer=layer, tm=tm, tn=tn, epilogue=epi,
               extras=[(x, _tile_spec(tm, tn)),
                       (p, pl.BlockSpec((tm, pdim), lambda i, j, k: (i, 0))),
                       (w_up, pl.BlockSpec((1, pdim, tn), lambda i, j, k: (layer, 0, j)))], name="ple_gate")[0]


def _group_rms(a):
    return a * lax.rsqrt(jnp.mean(a * a, axis=-1, keepdims=True) + EPS)


def _mm_kv(h, w_kv, g_k_slc, g_k_win):
    m = h.shape[0]
    gd = NSA_KV_HEADS * NSA_HEAD_DIM
    n = 2 * N_BRANCH * gd
    ones = jnp.ones((gd,), F32)
    gain = jnp.concatenate([ones, ones, jnp.tile(g_k_slc.astype(F32), NSA_KV_HEADS), ones,
                            jnp.tile(g_k_win.astype(F32), NSA_KV_HEADS), ones]).reshape(1, n)
    zeros = jnp.zeros((gd,), F32)
    flag = jnp.concatenate([zeros, zeros, ones, zeros, ones, zeros]).reshape(1, n)
    tm = _tile(m, 1024, SUBLANES)

    tn = 2 * gd

    def epi(acc, extras, o_ref):
        parts = [_group_rms(acc[:, c * NSA_HEAD_DIM:(c + 1) * NSA_HEAD_DIM]) for c in range(tn // NSA_HEAD_DIM)]
        normed = jnp.concatenate(parts, axis=-1) * extras[0][...]
        o_ref[...] = jnp.where(extras[1][...] > 0.5, normed, acc)

    row = pl.BlockSpec((1, tn), lambda i, j, k: (0, j))
    return _mm(h, w_kv, tm=tm, tn=tn, epilogue=epi, extras=[(gain, row), (flag, row)], name="kv_proj")[0]


def _mm_q(h, w_in, g_q, b, t):
    m, d_model = h.shape
    g, d = NSA_KV_HEADS, NSA_HEAD_DIM
    hpg = d_model // d // g
    tn = hpg * d
    tm = _tile(t, 1024, SUBLANES) if t >= 256 else _tile(m, 1024, t)
    seqs = max(tm // t, 1)
    rows = min(tm, t)

    def epi(acc, extras, o_ref):
        for sq in range(seqs):
            for hh in range(hpg):
                a = acc[sq * rows:(sq + 1) * rows, hh * d:(hh + 1) * d]
                o_ref[sq, 0, hh] = _group_rms(a) * extras[0][...]

    gq = g_q.reshape(1, d).astype(F32)
    tiles_per_b = t // rows
    return _mm(h, w_in, n_out=g * tn, tm=tm, tn=tn, epilogue=epi,
               extras=[(gq, pl.BlockSpec((1, d), lambda i, j, k: (0, 0)))],
               out_shape=jax.ShapeDtypeStruct((b, g, hpg, t, d), F32),
               out_spec=pl.BlockSpec((seqs, 1, hpg, rows, d),
                                     lambda i, j, k: (i // tiles_per_b, j, 0, i % tiles_per_b, 0)),
               name="nsa_q")[0]


def _ret_kernel(*refs, c, cp, nc, hb, has_s0):
    if has_s0:
        q_ref, k_ref, v_ref, g_ref, cos_ref, sin_ref, mask_ref, qd_ref, kd_ref, cd_ref, s0_ref, o_ref, so_ref, s_ref = refs
    else:
        q_ref, k_ref, v_ref, g_ref, cos_ref, sin_ref, mask_ref, qd_ref, kd_ref, cd_ref, o_ref, so_ref, s_ref = refs
    ci = pl.program_id(2)
    dk, dv = RET_HEAD_DIM, RET_V_DIM

    @pl.when(ci == 0)
    def _():
        if has_s0:
            s_ref[...] = s0_ref[0]
        else:
            s_ref[...] = jnp.zeros_like(s_ref)

    def padded(a):
        if cp == c:
            return a
        return jnp.concatenate([a, jnp.zeros((cp - c, a.shape[1]), a.dtype)], axis=0)

    half = dk // 2
    cos, sin = cos_ref[...], sin_ref[...]

    def rot(a):
        a1, a2 = a[:, :half], a[:, half:]
        return jnp.concatenate([a1 * cos - a2 * sin, a1 * sin + a2 * cos], axis=-1)

    for hh in range(hb):
        q = rot(padded(q_ref[0, :, hh * dk:(hh + 1) * dk]))
        k = rot(padded(k_ref[0, :, hh * dk:(hh + 1) * dk])) * (dk ** -0.5)
        v = padded(v_ref[0, :, hh * dv:(hh + 1) * dv]).astype(BF16)
        s = s_ref[hh]
        att = lax.dot_general(q.astype(BF16), k.astype(BF16), (((1,), (1,)), ((), ())),
                              preferred_element_type=F32) * mask_ref[hh]
        o = (jnp.dot(att.astype(BF16), v, preferred_element_type=F32)
             + jnp.dot((q * qd_ref[hh]).astype(BF16), s.astype(BF16), preferred_element_type=F32))
        kt = jnp.transpose(k * kd_ref[hh]).astype(BF16)
        s_new = s * cd_ref[hh] + jnp.dot(kt, v, preferred_element_type=F32)
        s_ref[hh] = s_new
        o = _group_rms(o[:c])
        gate = g_ref[0, :, hh * dv:(hh + 1) * dv]
        o_ref[0, :, hh * dv:(hh + 1) * dv] = (gate * jax.nn.sigmoid(gate) * o).astype(o_ref.dtype)

    @pl.when(ci == nc - 1)
    def _():
        so_ref[0] = s_ref[...]


def _retention(qkvg, pos, s0):
    b, t, width = qkvg.shape
    dk, dv = RET_HEAD_DIM, RET_V_DIM
    h = width // (2 * dk + 2 * dv)
    c = RET_CHUNK if t % RET_CHUNK == 0 else t
    nc = t // c
    cp = max(c, LANES)
    half = dk // 2
    lg = jnp.log1p(-(2.0 ** (-5.0 - jnp.arange(h, dtype=F32))))
    idx = jnp.arange(c, dtype=F32)
    diff = idx[:, None] - idx[None, :]
    mask = jnp.where(diff >= 0, jnp.exp(jnp.maximum(diff, 0.0)[None] * lg[:, None, None]), 0.0)
    q_dec = jnp.exp((idx + 1.0)[None, :] * lg[:, None])
    k_dec = jnp.exp((c - 1.0 - idx)[None, :] * lg[:, None])
    c_dec = jnp.exp(c * lg)
    mask = jnp.pad(mask, ((0, 0), (0, cp - c), (0, cp - c)))
    q_dec = jnp.pad(q_dec, ((0, 0), (0, cp - c)))[..., None]
    k_dec = jnp.pad(k_dec, ((0, 0), (0, cp - c)))[..., None]
    c_dec = c_dec.reshape(h, 1, 1)
    inv = ROPE_BASE ** (-jnp.arange(half, dtype=F32) / half)
    ang = pos.astype(F32)[:, None] * inv[None, :]
    cos = jnp.pad(jnp.cos(ang), ((0, nc * cp - t), (0, 0)))
    sin = jnp.pad(jnp.sin(ang), ((0, nc * cp - t), (0, 0)))

    hb = _tile(h, 4, 1)
    ng = h // hb
    vb = (2 * h * dk) // (hb * dv)
    assert (2 * h * dk) % (hb * dv) == 0
    in_specs = [
        pl.BlockSpec((1, c, hb * dk), lambda bi, hi, ci: (bi, ci, hi)),
        pl.BlockSpec((1, c, hb * dk), lambda bi, hi, ci: (bi, ci, ng + hi)),
        pl.BlockSpec((1, c, hb * dv), lambda bi, hi, ci: (bi, ci, vb + hi)),
        pl.BlockSpec((1, c, hb * dv), lambda bi, hi, ci: (bi, ci, vb + ng + hi)),
        pl.BlockSpec((cp, half), lambda bi, hi, ci: (ci, 0)),
        pl.BlockSpec((cp, half), lambda bi, hi, ci: (ci, 0)),
        pl.BlockSpec((hb, cp, cp), lambda bi, hi, ci: (hi, 0, 0)),
        pl.BlockSpec((hb, cp, 1), lambda bi, hi, ci: (hi, 0, 0)),
        pl.BlockSpec((hb, cp, 1), lambda bi, hi, ci: (hi, 0, 0)),
        pl.BlockSpec((hb, 1, 1), lambda bi, hi, ci: (hi, 0, 0)),
    ]
    args = [qkvg, qkvg, qkvg, qkvg, cos, sin, mask, q_dec, k_dec, c_dec]
    if s0 is not None:
        in_specs.append(pl.BlockSpec((1, hb, dk, dv), lambda bi, hi, ci: (bi, hi, 0, 0)))
        args.append(s0)
    o, s_out = pl.pallas_call(
        functools.partial(_ret_kernel, c=c, cp=cp, nc=nc, hb=hb, has_s0=s0 is not None),
        grid=(b, ng, nc),
        in_specs=in_specs,
        out_specs=[pl.BlockSpec((1, c, hb * dv), lambda bi, hi, ci: (bi, ci, hi)),
                   pl.BlockSpec((1, hb, dk, dv), lambda bi, hi, ci: (bi, hi, 0, 0))],
        out_shape=[jax.ShapeDtypeStruct((b, t, h * dv), BF16), jax.ShapeDtypeStruct((b, h, dk, dv), F32)],
        scratch_shapes=[pltpu.VMEM((hb, dk, dv), F32)],
        compiler_params=_params(("parallel", "parallel", "arbitrary")),
        name="retention",
    )(*args)
    return o, s_out


def _router_kernel(h_ref, w_ref, b_ref, ids_ref, gates_ref):
    logits = jnp.dot(h_ref[...], w_ref[...].astype(BF16), preferred_element_type=F32) + b_ref[...]
    lane = lax.broadcasted_iota(jnp.int32, logits.shape, 1)
    big = jnp.int32(LANES)
    ng, ne = MOE_GROUPS, MOE_EXPERTS_PER_GROUP
    gl = jnp.where(lane < ng, logits, NEG_INF)
    gmax = jnp.max(gl, axis=-1, keepdims=True)
    gsum = jnp.sum(jnp.exp(gl - gmax), axis=-1, keepdims=True)
    g_sel = jnp.min(jnp.where(gl == gmax, lane, big), axis=-1, keepdims=True)
    g_w = 1.0 / gsum
    lo = ng + g_sel * ne
    in_group = (lane >= lo) & (lane < lo + ne)
    el = jnp.where(in_group, logits, NEG_INF)
    emax = jnp.max(el, axis=-1, keepdims=True)
    ee = jnp.exp(el - emax)
    ep = ee / jnp.sum(ee, axis=-1, keepdims=True)
    ep = jnp.where(in_group, ep, -1.0)
    p1 = jnp.max(ep, axis=-1, keepdims=True)
    i1 = jnp.min(jnp.where(ep == p1, lane, big), axis=-1, keepdims=True)
    ep2 = jnp.where(lane == i1, -1.0, ep)
    p2 = jnp.max(ep2, axis=-1, keepdims=True)
    i2 = jnp.min(jnp.where(ep2 == p2, lane, big), axis=-1, keepdims=True)
    psum = p1 + p2
    ids_ref[...] = jnp.where(lane == 0, i1 - ng, jnp.where(lane == 1, i2 - ng, 0))
    gates_ref[...] = jnp.where(lane == 0, g_w * p1 / psum, jnp.where(lane == 1, g_w * p2 / psum, 0.0))


def _router(hf, w_rg, b_rg, w_re, b_re):
    n, d = hf.shape
    ng, ne = MOE_GROUPS, MOE_EXPERTS_PER_GROUP
    w = jnp.concatenate([w_rg.astype(F32), jnp.transpose(w_re.astype(F32), (1, 0, 2)).reshape(d, ng * ne)], axis=1)
    w = jnp.pad(w, ((0, 0), (0, LANES - w.shape[1])))
    bias = jnp.pad(jnp.concatenate([b_rg.astype(F32), b_re.astype(F32).reshape(-1)]), (0, LANES - ng - ng * ne)).reshape(1, LANES)
    tm = _tile(n, 256, SUBLANES)
    ids, gates = pl.pallas_call(
        _router_kernel,
        grid=(n // tm,),
        in_specs=[pl.BlockSpec((tm, d), lambda i: (i, 0)), pl.BlockSpec((d, LANES), lambda i: (0, 0)),
                  pl.BlockSpec((1, LANES), lambda i: (0, 0))],
        out_specs=[pl.BlockSpec((tm, LANES), lambda i: (i, 0)), pl.BlockSpec((tm, LANES), lambda i: (i, 0))],
        out_shape=[jax.ShapeDtypeStruct((n, LANES), jnp.int32), jax.ShapeDtypeStruct((n, LANES), F32)],
        compiler_params=_params(("parallel",)),
        name="moe_router",
    )(hf, w, bias)
    return ids[:, :MOE_TOP], gates[:, :MOE_TOP]


def _row_copy(src, s_row, dst, d_row, sem):
    return pltpu.make_async_copy(src.at[pl.ds(s_row, 1)], dst.at[pl.ds(d_row, 1)], sem)


def _row_pitch(n_planes):
    return n_planes + (4 - n_planes) % SUBLANES


def _moe_up_kernel(be_ref, nu_ref, tokc_ref, tokn_ref, x_hbm, w_ref, h_ref, xbuf, sem, *, rows):
    b = pl.program_id(0)
    n_used = nu_ref[0]
    slot = lax.rem(b, 2)
    n_planes = x_hbm.shape[1]
    ROW_PITCH = _row_pitch(n_planes)

    def row_copy(tok_ref, s, r):
        dst = xbuf.at[pl.ds((s * rows + r) * ROW_PITCH, n_planes), :]
        return pltpu.make_async_copy(x_hbm.at[tok_ref[0, 0, r]], dst, sem.at[s])

    def gather(tok_ref, s):
        def body(r, carry):
            row_copy(tok_ref, s, r).start()
            return carry
        lax.fori_loop(0, rows, body, 0, unroll=8)

    @pl.when(b == 0)
    def _():
        gather(tokc_ref, 0)

    @pl.when(b + 1 < n_used)
    def _():
        gather(tokn_ref, 1 - slot)

    @pl.when(b < n_used)
    def _():
        def wait_body(r, carry):
            row_copy(tokc_ref, slot, r).wait()
            return carry
        lax.fori_loop(0, rows, wait_body, 0, unroll=8)
        f = h_ref.shape[1]
        base = slot * (rows * ROW_PITCH)
        acc = jnp.zeros((rows, 2 * f), F32)
        for j in range(0, n_planes, 2):
            x2 = jnp.concatenate([xbuf[pl.ds(base + j, rows, stride=ROW_PITCH), :],
                                  xbuf[pl.ds(base + j + 1, rows, stride=ROW_PITCH), :]], axis=1)
            acc = acc + jnp.dot(x2.astype(BF16), w_ref[0, 0, j * LANES:(j + 2) * LANES, :].astype(BF16),
                                preferred_element_type=F32)
        a, g = acc[:, :f], acc[:, f:]
        h_ref[...] = (a * jax.nn.sigmoid(a) * g).astype(h_ref.dtype)

    @pl.when(b >= n_used)
    def _():
        h_ref[...] = jnp.zeros_like(h_ref)


def _moe_down_kernel(be_ref, nu_ref, nv_ref, dstc_ref, dstp_ref, h_ref, g_ref, w_ref, y_hbm, ybuf, sem, *, nb):
    b = pl.program_id(0)
    n_used = nu_ref[0]
    slot = lax.rem(b, 2)

    def scatter(dst_ref, s, count, wait):
        def body(r, carry):
            cp = _row_copy(ybuf.at[s], r, y_hbm, dst_ref[0, 0, r], sem.at[s])
            if wait:
                cp.wait()
            else:
                cp.start()
            return carry
        lax.fori_loop(0, count, body, 0)

    @pl.when(b < n_used)
    def _():
        y = jnp.dot(h_ref[...], w_ref[0, 0].astype(BF16), preferred_element_type=F32) * g_ref[...]
        ybuf[slot] = y
        scatter(dstc_ref, slot, nv_ref[b], False)

    @pl.when((b >= 1) & (b - 1 < n_used))
    def _():
        scatter(dstp_ref, 1 - slot, nv_ref[jnp.maximum(b - 1, 0)], True)

    @pl.when((b == nb - 1) & (b < n_used))
    def _():
        scatter(dstc_ref, slot, nv_ref[b], True)


def _moe(h16, hf3, layer, w_rg, b_rg, w_re, b_re, w_up, w_down):
    n, d = h16.shape
    e, f2 = w_up.shape[1], w_up.shape[3]
    f = f2 // 2
    rows = MOE_ROWS
    ids, gates = _router(h16, w_rg, b_rg, w_re, b_re)

    a = n * MOE_TOP
    nb = -(-a // rows) + e
    e_flat = ids.reshape(-1)
    a_idx = jnp.arange(a, dtype=jnp.int32)
    tok_flat = a_idx // MOE_TOP
    _, tok_s, dst_s, gate_s = lax.sort(
        (e_flat, tok_flat, (a_idx % MOE_TOP) * n + tok_flat, lax.bitcast_convert_type(gates.reshape(-1), jnp.int32)),
        num_keys=1, is_stable=True)
    packed = jnp.stack([tok_s, dst_s, gate_s, jnp.zeros_like(tok_s)], axis=1)
    counts = jnp.bincount(e_flat, length=e).astype(jnp.int32)
    starts = jnp.cumsum(counts) - counts
    blocks_per = (counts + rows - 1) // rows
    blk_end = jnp.cumsum(blocks_per)
    first_blk = blk_end - blocks_per
    n_used = blk_end[-1].astype(jnp.int32)
    blk_ids = jnp.arange(nb, dtype=jnp.int32)
    owner = jnp.minimum(jnp.searchsorted(blk_end, blk_ids, side="right"), e - 1).astype(jnp.int32)
    in_e0 = (blk_ids - first_blk[owner]) * rows
    n_valid = jnp.where(blk_ids < n_used, jnp.clip(counts[owner] - in_e0, 0, rows), 0).astype(jnp.int32)
    within = jnp.arange(rows, dtype=jnp.int32)[None, :]
    live = within < n_valid[:, None]
    src = jnp.clip((starts[owner] + in_e0)[:, None] + within, 0, a - 1)
    picked = packed[src]
    tok3 = jnp.where(live, picked[..., 0], 0).reshape(nb, 1, rows)
    dst3 = jnp.where(live, picked[..., 1], 0).reshape(nb, 1, rows)
    gate_buf = jnp.where(live, lax.bitcast_convert_type(picked[..., 2], F32), 0.0)
    block_expert = owner[jnp.minimum(blk_ids, n_used - 1)]
    n_used_arr = n_used.reshape(1)
    smem_blk = functools.partial(pl.BlockSpec, (1, 1, rows), memory_space=pltpu.SMEM)
    h_mid = pl.pallas_call(
        functools.partial(_moe_up_kernel, rows=rows),
        grid_spec=pltpu.PrefetchScalarGridSpec(
            num_scalar_prefetch=2,
            grid=(nb,),
            in_specs=[smem_blk(lambda b, be, nu: (b, 0, 0)),
                      smem_blk(lambda b, be, nu: (jnp.minimum(b + 1, nb - 1), 0, 0)),
                      pl.BlockSpec(memory_space=pl.ANY),
                      pl.BlockSpec((1, 1, d, f2), lambda b, be, nu: (layer, be[b], 0, 0))],
            out_specs=pl.BlockSpec((rows, f), lambda b, be, nu: (b, 0)),
            scratch_shapes=[pltpu.VMEM((2 * rows * _row_pitch(d // LANES), LANES), F32),
                            pltpu.SemaphoreType.DMA((2,))]),
        out_shape=jax.ShapeDtypeStruct((nb * rows, f), BF16),
        compiler_params=_params(("arbitrary",)),
        name="moe_up",
    )(block_expert, n_used_arr, tok3, tok3, hf3, w_up)

    y_tok = pl.pallas_call(
        functools.partial(_moe_down_kernel, nb=nb),
        grid_spec=pltpu.PrefetchScalarGridSpec(
            num_scalar_prefetch=3,
            grid=(nb,),
            in_specs=[smem_blk(lambda b, be, nu, nv: (b, 0, 0)),
                      smem_blk(lambda b, be, nu, nv: (jnp.maximum(b - 1, 0), 0, 0)),
                      pl.BlockSpec((rows, f), lambda b, be, nu, nv: (b, 0)),
                      pl.BlockSpec((rows, 1), lambda b, be, nu, nv: (b, 0)),
                      pl.BlockSpec((1, 1, f, d), lambda b, be, nu, nv: (layer, be[b], 0, 0))],
            out_specs=pl.BlockSpec(memory_space=pl.ANY),
            scratch_shapes=[pltpu.VMEM((2, rows, d), F32), pltpu.SemaphoreType.DMA((2,))]),
        out_shape=jax.ShapeDtypeStruct((MOE_TOP * n, d), F32),
        compiler_params=_params(("arbitrary",)),
        name="moe_down",
    )(block_expert, n_used_arr, n_valid, dst3, dst3, h_mid, gate_buf.reshape(nb * rows, 1), w_down)
    return y_tok


def _page_specs(cache_shape, n_pages, pgs):
    return [pl.BlockSpec((1,) + tuple(cache_shape[1:]),
                         functools.partial(lambda bi, j, pt, r: (pt[bi * n_pages + j * pgs + r], 0, 0, 0, 0), r=r))
            for r in range(pgs)]


def _cmp_paged_kernel(pt_ref, *refs, pgs, g, d):
    page_refs = refs[:pgs]
    perm_ref, wk_ref, wv_ref, uk_ref, uv_ref = refs[pgs:]
    cs = CMP_STRIDE
    page = page_refs[0].shape[1]
    cpp = page // cs
    xs = [[[jnp.dot(perm_ref[...], pr[0, :, kv, gi, :].astype(BF16), preferred_element_type=F32)
            for gi in range(g)] for kv in range(2)] for pr in page_refs]
    m = g * pgs * cpp
    accs = [jnp.zeros((m, wk_ref.shape[1]), F32), jnp.zeros((m, wv_ref.shape[1]), F32)]
    for pp in range(0, cs, 2):
        for kv, w_ref in enumerate((wk_ref, wv_ref)):
            halves = []
            for p in (pp, pp + 1):
                pieces = [xs[r][kv][gi][p * cpp:(p + 1) * cpp, :] for gi in range(g) for r in range(pgs)]
                halves.append(jnp.concatenate(pieces, axis=0))
            lhs = jnp.concatenate(halves, axis=1).astype(BF16)
            accs[kv] = accs[kv] + jnp.dot(lhs, w_ref[pp * d:(pp + 2) * d, :], preferred_element_type=F32)
    uk_ref[0] = accs[0].reshape(g, pgs * cpp, wk_ref.shape[1])
    uv_ref[0] = accs[1].reshape(g, pgs * cpp, wv_ref.shape[1])


def _cmp_hidden_paged(cache, page_table, wcat_k, wcat_v):
    page = cache.shape[1]
    g, d = cache.shape[3], cache.shape[4]
    b, n_pages = page_table.shape
    pgs = _tile(n_pages, 8, 1)
    cpp = page // CMP_STRIDE
    nch = n_pages * cpp
    perm = np.zeros((page, page), np.float32)
    for c in range(cpp):
        for p in range(CMP_STRIDE):
            perm[p * cpp + c, c * CMP_STRIDE + p] = 1.0
    hid2 = wcat_k.shape[1]
    const = lambda shape: pl.BlockSpec(shape, lambda bi, j, pt: (0, 0))
    uk, uv = pl.pallas_call(
        functools.partial(_cmp_paged_kernel, pgs=pgs, g=g, d=d),
        grid_spec=pltpu.PrefetchScalarGridSpec(
            num_scalar_prefetch=1,
            grid=(b, n_pages // pgs),
            in_specs=_page_specs(cache.shape, n_pages, pgs)
            + [const((page, page)), const(wcat_k.shape), const(wcat_v.shape)],
            out_specs=[pl.BlockSpec((1, g, pgs * cpp, hid2), lambda bi, j, pt: (bi, 0, j, 0))] * 2),
        out_shape=[jax.ShapeDtypeStruct((b, g, nch, hid2), F32)] * 2,
        compiler_params=_params(("parallel", "parallel")),
        name="cmp_hidden_paged",
    )(page_table.reshape(-1).astype(jnp.int32), *([cache] * pgs), jnp.asarray(perm, BF16),
      wcat_k.astype(BF16), wcat_v.astype(BF16))
    return uk.reshape(b * g, nch, hid2), uv.reshape(b * g, nch, hid2)
def _cmp_post_kernel(u_ref, pe_ref, w2_ref, g_ref, o_ref, *, n_cmp, norm):
    u = u_ref[0]
    nch, hid2 = u.shape
    hid = hid2 // 2
    nxt = pltpu.roll(u[:, hid:], nch - 1, axis=0)
    x = (pe_ref[0:1, :] + u[:, :hid]) + nxt
    y = 0.5 * x * (1.0 + jnp.tanh(0.7978845608028654 * (x + 0.044715 * (x * x * x))))
    z = jnp.dot(y.astype(BF16), w2_ref[...].astype(BF16), preferred_element_type=F32)
    if norm:
        z = _group_rms(z) * g_ref[...]
    row = lax.broadcasted_iota(jnp.int32, z.shape, 0)
    o_ref[0] = jnp.where(row < n_cmp, z, 0.0)


def _cmp_wcat(w1):
    r = CMP_LEN // CMP_STRIDE
    assert r == 2
    w1r = w1.reshape(r, w1.shape[0] // r, w1.shape[1])
    return jnp.concatenate([w1r[0], w1r[1]], axis=1)


def _cmp_hidden_dense(rows, nch, wcat):
    b, _, g, d = rows.shape
    ch = rows[:, :nch * CMP_STRIDE].reshape(b, nch, CMP_STRIDE, g, d)
    ch = jnp.transpose(ch, (0, 3, 1, 2, 4)).reshape(b * g * nch, CMP_STRIDE * d).astype(BF16)
    return _mm(ch, wcat, name="cmp_hidden")[0].reshape(b * g, nch, wcat.shape[1])


def _cmp_finish(u, n_cmp, pe, w1, w2, g_k):
    bg, nch, hid2 = u.shape
    hid = hid2 // 2
    d = w2.shape[1]
    pe_rows = jnp.pad(pe.reshape(1, -1), ((0, SUBLANES - 1), (0, 0)))
    pe_hid = _mm(pe_rows, w1, name="cmp_pe")[0]
    gain = (jnp.ones((d,), F32) if g_k is None else g_k.astype(F32)).reshape(1, d)
    return pl.pallas_call(
        functools.partial(_cmp_post_kernel, n_cmp=n_cmp, norm=g_k is not None),
        grid=(bg,),
        in_specs=[pl.BlockSpec((1, nch, hid2), lambda i: (i, 0, 0)), pl.BlockSpec((SUBLANES, hid), lambda i: (0, 0)),
                  pl.BlockSpec((hid, d), lambda i: (0, 0)), pl.BlockSpec((1, d), lambda i: (0, 0))],
        out_specs=pl.BlockSpec((1, nch, d), lambda i: (i, 0, 0)),
        out_shape=jax.ShapeDtypeStruct((bg, nch, d), F32),
        compiler_params=_params(("parallel",)),
        name="cmp_post",
    )(u, pe_hid, w2, gain)


def _slc_map(n_cmp, n_slc, rows, cols):
    a = SLC_BLOCK // CMP_STRIDE
    bb = CMP_LEN // CMP_STRIDE
    j = np.arange(n_slc)[:, None, None]
    i = j * a + np.arange(a)[None, :, None] + np.arange(bb)[None, None, :] - bb + 1
    i, jj = np.broadcast_arrays(i, j)
    ok = (i >= 0) & (i < n_cmp)
    m = np.zeros((rows, cols), np.float32)
    np.add.at(m, (i[ok], jj[ok]), 1.0)
    return jnp.asarray(m)


def _masked_softmax_rows(s):
    m = jnp.max(s, axis=-1, keepdims=True)
    e = jnp.exp(s - jnp.where(m > NEG_INF, m, 0.0))
    den = jnp.sum(e, axis=-1, keepdims=True)
    return e / jnp.where(den > 0, den, 1.0)


def _store_gated(o, gl_ref, prev_ref, o_ref, branch, hpg, tq):
    d = NSA_HEAD_DIM
    gate = jax.nn.sigmoid(gl_ref[0])
    for hh in range(hpg):
        c = hh * N_BRANCH + branch
        val = gate[:, c:c + 1] * o[hh * tq:(hh + 1) * tq]
        if prev_ref is not None:
            val = prev_ref[0, :, hh * d:(hh + 1) * d] + val
        o_ref[0, :, hh * d:(hh + 1) * d] = val.astype(o_ref.dtype)


def _nsa_cmp_kernel(q_ref, kc_ref, vc_ref, map_ref, gl_ref, o_ref, sel_ref, *, hpg, tq, pos0, n_cmp, n_slc, n_top):
    i = pl.program_id(2)
    rws = hpg * tq
    q = q_ref[0, 0].reshape(rws, NSA_HEAD_DIM).astype(BF16)
    kc = kc_ref[0].astype(BF16)
    s = lax.dot_general(q, kc, (((1,), (1,)), ((), ())), preferred_element_type=F32) * NSA_SCALE
    ncp = s.shape[1]
    tok = lax.broadcasted_iota(jnp.int32, (rws, 1), 0) & (tq - 1)
    pos = pos0 + i * tq + tok
    cidx = lax.broadcasted_iota(jnp.int32, (1, ncp), 1)
    ok = (cidx * CMP_STRIDE + (CMP_LEN - 1) <= pos) & (cidx < n_cmp)
    p = _masked_softmax_rows(jnp.where(ok, s, NEG_INF))
    o = jnp.dot(p.astype(BF16), vc_ref[0].astype(BF16), preferred_element_type=F32)
    _store_gated(o, gl_ref, None, o_ref, 0, hpg, tq)

    psum = jnp.sum(p.astype(BF16).astype(F32).reshape(hpg, tq, ncp), axis=0)
    imp = jnp.dot(psum, map_ref[...], precision=lax.Precision.HIGHEST, preferred_element_type=F32)
    nsp = imp.shape[1]
    posq = pos0 + i * tq + lax.broadcasted_iota(jnp.int32, (tq, 1), 0)
    blk = lax.broadcasted_iota(jnp.int32, (tq, nsp), 1)
    back = (posq >> SLC_SHIFT) - blk
    real = blk < n_slc
    valid = (blk * SLC_BLOCK <= posq) & real
    forced = (blk == 0) | ((back >= 0) & (back < SLC_LOCAL))
    score = jnp.where(valid, imp + jnp.where(forced, FORCE_BONUS, 0.0), NEG_INF)
    rank = jnp.zeros((tq, nsp), jnp.int32)
    for kb in range(n_slc):
        col = score[:, kb:kb + 1]
        ahead = (col > score) | ((col == score) & (blk > kb))
        rank = rank + ahead.astype(jnp.int32)
    sel_ref[0, 0] = jnp.where((rank < n_top) & real, 1.0, 0.0)


def _nsa_cmp(qn, kc, vc, glog, pos0, n_cmp, n_slc):
    b, g, hpg, t, d = qn.shape
    ncp = kc.shape[1]
    nsp = -(-n_slc // LANES) * LANES
    tq = _tile(t, 128, SUBLANES)
    smap = _slc_map(n_cmp, n_slc, ncp, nsp)
    n_top = min(SLC_TOP, n_slc)
    return pl.pallas_call(
        functools.partial(_nsa_cmp_kernel, hpg=hpg, tq=tq, pos0=pos0, n_cmp=n_cmp, n_slc=n_slc, n_top=n_top),
        grid=(b, g, t // tq),
        in_specs=[pl.BlockSpec((1, 1, hpg, tq, d), lambda bi, gi, i: (bi, gi, 0, i, 0)),
                  pl.BlockSpec((1, ncp, d), lambda bi, gi, i: (bi * g + gi, 0, 0)),
                  pl.BlockSpec((1, ncp, d), lambda bi, gi, i: (bi * g + gi, 0, 0)),
                  pl.BlockSpec((ncp, nsp), lambda bi, gi, i: (0, 0)),
                  pl.BlockSpec((1, tq, LANES), lambda bi, gi, i: (bi, i, gi))],
        out_specs=[pl.BlockSpec((1, tq, hpg * d), lambda bi, gi, i: (bi, i, gi)),
                   pl.BlockSpec((1, 1, tq, nsp), lambda bi, gi, i: (bi, gi, i, 0))],
        out_shape=[jax.ShapeDtypeStruct((b, t, g * hpg * d), F32), jax.ShapeDtypeStruct((b, g, t, nsp), F32)],
        compiler_params=_params(("parallel", "parallel", "parallel")),
        name="nsa_cmp",
    )(qn, kc, vc, smap, glog)


def _nsa_attn_kernel(*refs, hpg, tq, tk, n_kt, qpos0, kpos0, window, use_sel, branch):
    refs = list(refs)
    q_ref, k_ref, v_ref = refs[:3]
    refs = refs[3:]
    sel_ref = None
    if use_sel:
        sel_ref = refs[0]
        refs = refs[1:]
    gl_ref, prev_ref, o_ref, qs_ref, s_ref, p_ref, bias_ref, m_ref, a_ref, acc_ref = refs
    i = pl.program_id(2)
    d = NSA_HEAD_DIM
    rws = hpg * tq
    rb = min(tq, 64)
    per_head = tq // rb
    pvb = min(rws, 512)
    qs_ref[...] = (q_ref[0, 0].reshape(rws, d) * NSA_SCALE).astype(BF16)
    posq = qpos0 + i * tq + lax.broadcasted_iota(jnp.int32, (tq, 1), 0)
    sel = sel_ref[0, 0].astype(BF16) if use_sel else None
    m_ref[...] = jnp.full_like(m_ref, NEG_INF)
    acc_ref[...] = jnp.zeros_like(acc_ref)
    ones = jnp.ones((tk, d), BF16)

    def body(kt, carry):
        key0 = kt * tk
        off = pl.multiple_of(key0, tk)
        k = k_ref[0, pl.ds(off, tk), :].astype(BF16)
        v1 = jnp.concatenate([v_ref[0, pl.ds(off, tk), :].astype(BF16), ones], axis=1)
        s_ref[...] = lax.dot_general(qs_ref[...], k, (((1,), (1,)), ((), ())), preferred_element_type=F32)
        kpos = kpos0 + key0 + lax.broadcasted_iota(jnp.int32, (1, tk), 1)
        ok = kpos <= posq
        if window is not None:
            ok = ok & (posq - kpos < window)
        if use_sel:
            nsp = sel.shape[1]
            kblk = (key0 + lax.broadcasted_iota(jnp.int32, (nsp, tk), 1)) >> SLC_SHIFT
            expand = (kblk == lax.broadcasted_iota(jnp.int32, (nsp, tk), 0)).astype(BF16)
            ok = ok & (jnp.dot(sel, expand, preferred_element_type=F32) > 0.5)
        bias_ref[...] = jnp.where(ok, 0.0, NEG_INF)
        for blk in range(rws // rb):
            rows = slice(blk * rb, (blk + 1) * rb)
            part = blk % per_head
            s = s_ref[rows, :] + bias_ref[part * rb:(part + 1) * rb, :]
            m_old = m_ref[rows, :]
            m_new = jnp.maximum(m_old, jnp.max(s, axis=-1, keepdims=True))
            m_safe = jnp.where(m_new > NEG_INF, m_new, 0.0)
            p_ref[rows, :] = jnp.exp(s - jnp.tile(m_safe, (1, tk // LANES))).astype(BF16)
            a_ref[rows, :] = jnp.exp(m_old - m_safe)
            m_ref[rows, :] = m_new
        for r0 in range(0, rws, pvb):
            rows = slice(r0, r0 + pvb)
            pv = jnp.dot(p_ref[rows, :], v1, preferred_element_type=F32)
            acc_ref[rows, :] = jnp.tile(a_ref[rows, :], (1, 2)) * acc_ref[rows, :] + pv
        return carry

    q_lo = qpos0 + i * tq
    q_hi = q_lo + tq - 1
    hi = jnp.clip((q_hi - kpos0) // tk + 1, 0, n_kt)
    if window is None:
        lo = 0
    else:
        lo = jnp.clip((q_lo - (window - 1) - kpos0) // tk, 0, n_kt)
    lax.fori_loop(lo, hi, body, 0)
    l = acc_ref[:, d:]
    o = acc_ref[:, :d] / jnp.where(l > 0, l, 1.0)
    _store_gated(o, gl_ref, prev_ref, o_ref, branch, hpg, tq)


def _nsa_attn(qn, kv, glog, prev, *, qpos0, kpos0, branch, out_dtype, window=None, sel=None, tk=512):
    b, g, hpg, t, d = qn.shape
    tk_total = kv.shape[1]
    tq = _tile(t, 256, SUBLANES)
    tk = _tile(tk_total, tk, LANES)
    n_kt = tk_total // tk
    in_specs = [pl.BlockSpec((1, 1, hpg, tq, d), lambda bi, gi, i: (bi, gi, 0, i, 0)),
                pl.BlockSpec((1, tk_total, d), lambda bi, gi, i: (bi, 0, gi)),
                pl.BlockSpec((1, tk_total, d), lambda bi, gi, i: (bi, 0, g + gi))]
    args = [qn, kv, kv]
    if sel is not None:
        nsp = sel.shape[-1]
        in_specs.append(pl.BlockSpec((1, 1, tq, nsp), lambda bi, gi, i: (bi, gi, i, 0)))
        args.append(sel)
    in_specs += [pl.BlockSpec((1, tq, LANES), lambda bi, gi, i: (bi, i, gi)),
                 pl.BlockSpec((1, tq, hpg * d), lambda bi, gi, i: (bi, i, gi))]
    args += [glog, prev]
    rws = hpg * tq
    return pl.pallas_call(
        functools.partial(_nsa_attn_kernel, hpg=hpg, tq=tq, tk=tk, n_kt=n_kt, qpos0=qpos0, kpos0=kpos0,
                          window=window, use_sel=sel is not None, branch=branch),
        grid=(b, g, t // tq),
        in_specs=in_specs,
        out_specs=pl.BlockSpec((1, tq, hpg * d), lambda bi, gi, i: (bi, i, gi)),
        out_shape=jax.ShapeDtypeStruct((b, t, g * hpg * d), out_dtype),
        scratch_shapes=[pltpu.VMEM((rws, d), BF16), pltpu.VMEM((rws, tk), F32), pltpu.VMEM((rws, tk), BF16),
                        pltpu.VMEM((tq, tk), F32), pltpu.VMEM((rws, LANES), F32), pltpu.VMEM((rws, LANES), F32),
                        pltpu.VMEM((rws, 2 * d), F32)],
        compiler_params=_params(("parallel", "parallel", "parallel")),
        name="nsa_attn_%d" % branch,
    )(*args)


def _nsa_slc_paged_kernel(pt_ref, *refs, pgs, g, hpg, tq, qpos0, past_len, n_steps):
    page_refs = refs[:pgs]
    q_ref, sel_ref, tail_ref, gl_ref, prev_ref, o_ref, m_ref, l_ref, acc_ref = refs[pgs:]
    j = pl.program_id(1)
    d = NSA_HEAD_DIM
    rws = hpg * tq
    page = page_refs[0].shape[1]
    nsp = sel_ref.shape[-1]
    posq = qpos0 + lax.broadcasted_iota(jnp.int32, (tq, 1), 0)
    sel_all = sel_ref[0].reshape(g * tq, nsp).astype(BF16)

    @pl.when(j == 0)
    def _():
        m_ref[...] = jnp.full_like(m_ref, NEG_INF)
        l_ref[...] = jnp.zeros_like(l_ref)
        acc_ref[...] = jnp.zeros_like(acc_ref)

    def update(plane, n, key0):
        ok_pos = key0 + lax.broadcasted_iota(jnp.int32, (1, n), 1) <= posq
        kblk = (key0 + lax.broadcasted_iota(jnp.int32, (nsp, n), 1)) >> SLC_SHIFT
        expand = (kblk == lax.broadcasted_iota(jnp.int32, (nsp, n), 0)).astype(BF16)
        picked = jnp.dot(sel_all, expand, preferred_element_type=F32) > 0.5
        for gi in range(g):
            k = plane(0, gi).astype(BF16)
            v = plane(1, gi).astype(BF16)
            q = q_ref[0, gi].reshape(rws, d).astype(BF16)
            s = lax.dot_general(q, k, (((1,), (1,)), ((), ())), preferred_element_type=F32) * NSA_SCALE
            ok = ok_pos & picked[gi * tq:(gi + 1) * tq]
            s = jnp.where(ok[None], s.reshape(hpg, tq, n), NEG_INF).reshape(rws, n)
            m_old = m_ref[gi]
            m_new = jnp.maximum(m_old, jnp.max(s, axis=-1, keepdims=True))
            m_safe = jnp.where(m_new > NEG_INF, m_new, 0.0)
            p = jnp.exp(s - m_safe)
            alpha = jnp.exp(m_old - m_safe)
            l_ref[gi] = alpha * l_ref[gi] + jnp.sum(p, axis=-1, keepdims=True)
            acc_ref[gi] = alpha * acc_ref[gi] + jnp.dot(p.astype(BF16), v, preferred_element_type=F32)
            m_ref[gi] = m_new

    update(lambda kv, gi: jnp.concatenate([pr[0, :, kv, gi, :] for pr in page_refs], axis=0),
           pgs * page, j * (pgs * page))

    @pl.when(j == n_steps - 1)
    def _():
        update(lambda kv, gi: tail_ref[0, :, (kv * g + gi) * d:(kv * g + gi + 1) * d], tail_ref.shape[1], past_len)
        gate = jax.nn.sigmoid(gl_ref[0])
        for gi in range(g):
            l = l_ref[gi]
            o = acc_ref[gi] / jnp.where(l > 0, l, 1.0)
            for hh in range(hpg):
                c = gi * LANES + hh * N_BRANCH + 1
                col = (gi * hpg + hh) * d
                o_ref[0, :, col:col + d] = prev_ref[0, :, col:col + d] + gate[:, c:c + 1] * o[hh * tq:(hh + 1) * tq]


def _nsa_slc_paged(qn, cache, page_table, sel, tail, glog, prev, *, qpos0):
    b, g, hpg, t, d = qn.shape
    page = cache.shape[1]
    width = 2 * g * d
    n_pages = page_table.shape[1]
    pgs = _tile(n_pages, 8, 1)
    n_steps = n_pages // pgs
    nsp = sel.shape[-1]
    nt = tail.shape[1]
    rws = hpg * t
    return pl.pallas_call(
        functools.partial(_nsa_slc_paged_kernel, pgs=pgs, g=g, hpg=hpg, tq=t, qpos0=qpos0,
                          past_len=n_pages * page, n_steps=n_steps),
        grid_spec=pltpu.PrefetchScalarGridSpec(
            num_scalar_prefetch=1,
            grid=(b, n_steps),
            in_specs=_page_specs(cache.shape, n_pages, pgs)
            + [pl.BlockSpec((1, g, hpg, t, d), lambda bi, j, pt: (bi, 0, 0, 0, 0)),
               pl.BlockSpec((1, g, t, nsp), lambda bi, j, pt: (bi, 0, 0, 0)),
               pl.BlockSpec((1, nt, width), lambda bi, j, pt: (bi, 0, 0)),
               pl.BlockSpec((1, t, g * LANES), lambda bi, j, pt: (bi, 0, 0)),
               pl.BlockSpec((1, t, g * hpg * d), lambda bi, j, pt: (bi, 0, 0))],
            out_specs=pl.BlockSpec((1, t, g * hpg * d), lambda bi, j, pt: (bi, 0, 0)),
            scratch_shapes=[pltpu.VMEM((g, rws, 1), F32), pltpu.VMEM((g, rws, 1), F32), pltpu.VMEM((g, rws, d), F32)]),
        out_shape=jax.ShapeDtypeStruct((b, t, g * hpg * d), F32),
        compiler_params=_params(("parallel", "arbitrary")),
        name="nsa_slc_paged",
    )(page_table.reshape(-1).astype(jnp.int32), *([cache] * pgs), qn, sel, tail, glog, prev)


def _gate_weight(w_in, d_model):
    g = NSA_KV_HEADS
    hpg = d_model // NSA_HEAD_DIM // g
    wg = w_in[:, d_model:].reshape(d_model, g, hpg * N_BRANCH)
    wg = jnp.pad(wg, ((0, 0), (0, 0), (0, LANES - hpg * N_BRANCH)))
    return wg.reshape(d_model, g * LANES)


def kernel(x_prompt, x_sample, state_ret, cache_cmp_kv, cache_slc_kv, cache_win_kv, page_table, p_prompt, p_sample, g_mix, g_ffn, w_ret_in, w_ret_out, w_nsa_in, g_nsa_q, w_nsa_out, g_kv, w_kv, g_k_cmp, g_k_slc, g_k_win, pe_cmp_k, w_cmp_k1, w_cmp_k2, pe_cmp_v, w_cmp_v1, w_cmp_v2, w_rg, b_rg, w_re, b_re, w_moe_up, w_moe_down, w_ple_up, g_ple, w_ple_gate):
    depth = g_mix.shape[0]
    n_a = w_ret_in.shape[0]
    g, d = NSA_KV_HEADS, NSA_HEAD_DIM
    gd = g * d
    d_model = x_prompt.shape[-1]
    page = cache_cmp_kv.shape[1]
    past_len = page_table.shape[1] * page

    groups = [
        dict(x=x_prompt.reshape(-1, d_model), p=p_prompt, b=x_prompt.shape[0], t=x_prompt.shape[1], pos0=0, s0=None),
        dict(x=x_sample.reshape(-1, d_model), p=p_sample, b=x_sample.shape[0], t=x_sample.shape[1], pos0=past_len, s0=state_ret),
    ]
    n_rows = [gr["x"].shape[0] for gr in groups]
    n_tok = sum(n_rows)
    offs = [0, n_rows[0]]
    for gr in groups:
        gr["ret"] = []

    for i in range(depth):
        for gr in groups:
            b, t = gr["b"], gr["t"]
            h = _rms([(gr["x"], 0)], g_mix[i], [BF16])[0]
            if i < n_a:
                qkvg = _mm(h, w_ret_in[i], tm=2048, name="ret_in")[0]
                pos = gr["pos0"] + jnp.arange(t)
                s0 = None if gr["s0"] is None else gr["s0"][i]
                o, s_new = _retention(qkvg.reshape(b, t, -1), pos, s0)
                gr["ret"].append(s_new)
                gr["x"] = _mm_resid(o.reshape(b * t, -1), w_ret_out[i], gr["x"], "ret_out")
            else:
                j = i - n_a
                qn = _mm_q(h, w_nsa_in[j], g_nsa_q[j], b, t)
                glog = _mm(h, _gate_weight(w_nsa_in[j], d_model), name="nsa_gate")[0].reshape(b, t, g * LANES)
                ctx = gr["ctx"]
                o1, sel = _nsa_cmp(qn, ctx["k_c"], ctx["v_c"], glog, gr["pos0"], ctx["n_cmp"], ctx["n_slc"])
                if ctx["slc_tail"] is None:
                    o2 = _nsa_attn(qn, ctx["slc"], glog, o1, qpos0=gr["pos0"], kpos0=0, branch=1, out_dtype=F32, sel=sel)
                else:
                    o2 = _nsa_slc_paged(qn, cache_slc_kv, page_table, sel, ctx["slc_tail"], glog, o1, qpos0=gr["pos0"])
                o3 = _nsa_attn(qn, ctx["win"], glog, o2, qpos0=gr["pos0"], kpos0=ctx["win_pos0"], branch=2,
                               out_dtype=BF16, window=WINDOW)
                gr["x"] = _mm_resid(o3.reshape(b * t, -1), w_nsa_out[j], gr["x"], "nsa_out")

        normed = [_rms([(gr["x"], 0)], g_ffn[i], [BF16], planes=True) for gr in groups]
        h16 = jnp.concatenate([nm[0] for nm in normed], axis=0)
        hf3 = jnp.concatenate([nm[1] for nm in normed], axis=0)
        y_tok = _moe(h16, hf3, i, w_rg[i], b_rg[i], w_re[i], b_re[i], w_moe_up, w_moe_down)

        for gi, gr in enumerate(groups):
            rows = n_rows[gi]
            x_new, hp = _rms([(gr["x"], 0), (y_tok, offs[gi]), (y_tok, n_tok + offs[gi])], g_ple[i], [BF16],
                             want_sum=True, rows=rows, tm=64)
            gr["x"] = _mm_ple(hp, w_ple_gate, i, x_new, gr["p"][i].reshape(rows, -1), w_ple_up)

        if i == n_a - 1:
            for gi, gr in enumerate(groups):
                b, t = gr["b"], gr["t"]
                hk = _rms([(gr["x"], 0)], g_kv, [BF16])[0]
                kv = _mm_kv(hk, w_kv, g_k_slc, g_k_win).reshape(b, t, 2 * N_BRANCH * gd)
                cmp_new, slc_new, win_new = kv[..., :2 * gd], kv[..., 2 * gd:4 * gd], kv[..., 4 * gd:]
                gr["cmp_new"], gr["slc_new"] = cmp_new, slc_new
                wcat_k, wcat_v = _cmp_wcat(w_cmp_k1), _cmp_wcat(w_cmp_v1)
                n_keys = gr["pos0"] + t
                n_cmp = (n_keys - CMP_LEN) // CMP_STRIDE + 1
                nch = n_cmp + CMP_LEN // CMP_STRIDE - 1
                if gi == 0:
                    slc_tail = None
                    win_keys, win_pos0 = win_new, 0
                    gr["win_state"] = win_new[:, t - min(WINDOW, t):]
                    cmp_rows = cmp_new.reshape(b, t, 2, g, d)
                    u_k = _cmp_hidden_dense(cmp_rows[:, :, 0], nch, wcat_k)
                    u_v = _cmp_hidden_dense(cmp_rows[:, :, 1], nch, wcat_v)
                else:
                    slc_tail = jnp.pad(slc_new, ((0, 0), (0, LANES - t), (0, 0)))
                    w_buf = cache_win_kv.shape[1]
                    win_all = jnp.concatenate([cache_win_kv.reshape(b, w_buf, 2 * gd), win_new], axis=1)
                    n_all = w_buf + t
                    gr["win_state"] = win_all[:, n_all - min(WINDOW, past_len + t):]
                    win_keys = jnp.pad(win_all, ((0, 0), (0, -n_all % LANES), (0, 0)))
                    win_pos0 = past_len - w_buf
                    assert nch * CMP_STRIDE == past_len
                    u_k, u_v = _cmp_hidden_paged(cache_cmp_kv, page_table, wcat_k, wcat_v)
                k_c = _cmp_finish(u_k, n_cmp, pe_cmp_k, w_cmp_k1, w_cmp_k2, g_k_cmp)
                v_c = _cmp_finish(u_v, n_cmp, pe_cmp_v, w_cmp_v1, w_cmp_v2, None)
                gr["ctx"] = dict(k_c=k_c, v_c=v_c, n_cmp=n_cmp, n_slc=-(-n_keys // SLC_BLOCK), slc=slc_new,
                                 slc_tail=slc_tail, win=win_keys, win_pos0=win_pos0)

    outs = []
    for gr in groups:
        outs.append(gr["x"].reshape(gr["b"], gr["t"], d_model))
    rets = [jnp.stack(gr["ret"]) for gr in groups]
    kvs = []
    for name in ("cmp_new", "slc_new", "win_state"):
        for gr in groups:
            a = gr[name]
            kvs.append(a.reshape(a.shape[0], a.shape[1], 2, g, d))
    return (outs[0], outs[1], rets[0], rets[1], kvs[0], kvs[1], kvs[2], kvs[3], kvs[4], kvs[5])
```

```python
import functools

import numpy as np
import jax
import jax.numpy as jnp
from jax import lax
from jax.experimental import pallas as pl
from jax.experimental.pallas import tpu as pltpu

F32 = jnp.float32
BF16 = jnp.bfloat16

RET_HEAD_DIM = 256
RET_V_DIM = 2 * RET_HEAD_DIM
RET_CHUNK = 128
ROPE_BASE = 10000.0
NSA_HEAD_DIM = 128
NSA_KV_HEADS = 4
NSA_SCALE = NSA_HEAD_DIM ** -0.5
CMP_LEN = 32
CMP_STRIDE = 16
SLC_BLOCK = 64
SLC_SHIFT = 6
SLC_TOP = 16
SLC_LOCAL = 2
FORCE_BONUS = 1e4
WINDOW = 512
N_BRANCH = 3
MOE_GROUPS = 8
MOE_EXPERTS_PER_GROUP = 8
MOE_TOP = 2
EPS = 1e-6

LANES = 128
SUBLANES = 8
VMEM_LIMIT_BYTES = 56 * 1024 * 1024
MOE_ROWS = 256
NEG_INF = float("-inf")


def _tile(n, pref, align):
    best = None
    for t in range(align, min(n, pref) + 1, align):
        if n % t == 0:
            best = t
    return n if best is None else best


def _params(semantics):
    return pltpu.CompilerParams(dimension_semantics=semantics, vmem_limit_bytes=VMEM_LIMIT_BYTES)


def _rms_kernel(*refs, n_add, want_sum, planes):
    adds, g_ref, outs = refs[:n_add], refs[n_add], refs[n_add + 1:]
    x = adds[0][...]
    for r in adds[1:]:
        x = x + r[...]
    y = (x * lax.rsqrt(jnp.mean(x * x, axis=-1, keepdims=True) + EPS)) * g_ref[...]
    if want_sum:
        outs[0][...] = x
        outs = outs[1:]
    if planes:
        for j in range(y.shape[1] // LANES):
            outs[-1][:, j, :] = y[:, j * LANES:(j + 1) * LANES]
        outs = outs[:-1]
    for o in outs:
        o[...] = y.astype(o.dtype)


def _rms(addends, g, out_dtypes, want_sum=False, rows=None, tm=128, planes=False):
    d = addends[0][0].shape[1]
    rows = addends[0][0].shape[0] if rows is None else rows
    tm = _tile(rows, tm, SUBLANES)
    for _, off in addends:
        assert off % tm == 0
    in_specs = [pl.BlockSpec((tm, d), functools.partial(lambda i, o: (i + o, 0), o=off // tm)) for _, off in addends]
    in_specs.append(pl.BlockSpec((1, d), lambda i: (0, 0)))
    dts = ([F32] if want_sum else []) + list(out_dtypes)
    out_specs = [pl.BlockSpec((tm, d), lambda i: (i, 0)) for _ in dts]
    out_shape = [jax.ShapeDtypeStruct((rows, d), dt) for dt in dts]
    if planes:
        out_specs.append(pl.BlockSpec((tm, d // LANES, LANES), lambda i: (i, 0, 0)))
        out_shape.append(jax.ShapeDtypeStruct((rows, d // LANES, LANES), F32))
    outs = pl.pallas_call(
        functools.partial(_rms_kernel, n_add=len(addends), want_sum=want_sum, planes=planes),
        grid=(rows // tm,),
        in_specs=in_specs,
        out_specs=out_specs,
        out_shape=out_shape,
        compiler_params=_params(("parallel",)),
        name="rms",
    )(*[a for a, _ in addends], g.reshape(1, d).astype(F32))
    return outs


def _mm_kernel(x_ref, w_ref, *rest, n_extra, epilogue, nk, in_place):
    extras, o_ref = rest[:n_extra], rest[n_extra]
    acc_ref = o_ref if in_place else rest[n_extra + 1]
    k = pl.program_id(2)

    def part():
        w = w_ref[0] if len(w_ref.shape) == 3 else w_ref[...]
        return jnp.dot(x_ref[...].astype(BF16), w.astype(BF16), preferred_element_type=F32)

    @pl.when(k == 0)
    def _():
        acc_ref[...] = part()

    @pl.when(k > 0)
    def _():
        acc_ref[...] += part()

    if epilogue is not None or not in_place:
        @pl.when(k == nk - 1)
        def _():
            acc = acc_ref[...]
            if epilogue is None:
                o_ref[...] = acc.astype(o_ref.dtype)
            else:
                epilogue(acc, extras, o_ref)


def _mm(x, w, *, layer=None, n_out=None, out_dtype=F32, tm=1024, tn=1024, tk=512, epilogue=None, extras=(),
        out_shape=None, out_spec=None, name="mm"):
    m, kdim = x.shape
    n = w.shape[-1] if n_out is None else n_out
    if m <= 256:
        tk = 2048
    tm, tn, tk = _tile(m, tm, SUBLANES), _tile(n, tn, LANES), _tile(kdim, tk, LANES)
    nk = kdim // tk
    in_place = out_shape is None and out_dtype == F32
    if out_shape is None:
        out_shape = jax.ShapeDtypeStruct((m, n), out_dtype)
        out_spec = pl.BlockSpec((tm, tn), lambda i, j, k: (i, j))
    return pl.pallas_call(
        functools.partial(_mm_kernel, n_extra=len(extras), epilogue=epilogue, nk=nk, in_place=in_place),
        grid=(m // tm, n // tn, nk),
        in_specs=[pl.BlockSpec((tm, tk), lambda i, j, k: (i, k)),
                  pl.BlockSpec((tk, tn), lambda i, j, k: (k, j)) if layer is None
                  else pl.BlockSpec((1, tk, tn), lambda i, j, k: (layer, k, j))]
        + [s for _, s in extras],
        out_specs=out_spec,
        out_shape=out_shape,
        scratch_shapes=[] if in_place else [pltpu.VMEM((tm, tn), F32)],
        compiler_params=_params(("parallel", "parallel", "arbitrary")),
        name=name,
    )(x, w, *[a for a, _ in extras]), (tm, tn)


def _tile_spec(tm, tn):
    return pl.BlockSpec((tm, tn), lambda i, j, k: (i, j))


def _mm_resid_kernel(x_ref, w_ref, r_ref, o_ref):
    def part():
        return jnp.dot(x_ref[...].astype(BF16), w_ref[...].astype(BF16), preferred_element_type=F32)

    @pl.when(pl.program_id(2) == 0)
    def _():
        o_ref[...] = r_ref[...] + part()

    @pl.when(pl.program_id(2) > 0)
    def _():
        o_ref[...] += part()


def _mm_resid(x, w, resid, name):
    m, n = resid.shape
    kdim = x.shape[1]
    tm, tn = _tile(m, 2048, SUBLANES), _tile(n, 1024, LANES)
    tk = _tile(kdim, 2048 if m <= 256 else 512, LANES)
    return pl.pallas_call(
        _mm_resid_kernel,
        grid=(m // tm, n // tn, kdim // tk),
        in_specs=[pl.BlockSpec((tm, tk), lambda i, j, k: (i, k)), pl.BlockSpec((tk, tn), lambda i, j, k: (k, j)),
                  _tile_spec(tm, tn)],
        out_specs=_tile_spec(tm, tn),
        out_shape=jax.ShapeDtypeStruct((m, n), F32),
        compiler_params=_params(("parallel", "parallel", "arbitrary")),
        name=name,
    )(x, w, resid)


def _mm_ple(h, w_gate, layer, x, p, w_up):
    m, n = x.shape
    pdim = p.shape[1]
    tm, tn = _tile(m, 2048, SUBLANES), _tile(n, 1024, LANES)

    def epi(acc, extras, o_ref):
        pu =jnp.dot(extras[1][...].astype(BF16), extras[2][0].astype(BF16), preferred_element_type=F32)
        o_ref[...] = extras[0][...] + pu * jax.nn.sigmoid(acc)

    return _mm(h, w_gate, layer=layer, tm=tm, tn=tn, epilogue=epi,
               extras=[(x, _tile_spec(tm, tn)),
                       (p, pl.BlockSpec((tm, pdim), lambda i, j, k: (i, 0))),
                       (w_up, pl.BlockSpec((1, pdim, tn), lambda i, j, k: (layer, 0, j)))], name="ple_gate")[0]


def _group_rms(a):
    return a * lax.rsqrt(jnp.mean(a * a, axis=-1, keepdims=True) + EPS)


def _mm_kv(h, w_kv, g_k_slc, g_k_win):
    m = h.shape[0]
    gd = NSA_KV_HEADS * NSA_HEAD_DIM
    n = 2 * N_BRANCH * gd
    ones = jnp.ones((gd,), F32)
    gain = jnp.concatenate([ones, ones, jnp.tile(g_k_slc.astype(F32), NSA_KV_HEADS), ones,
                            jnp.tile(g_k_win.astype(F32), NSA_KV_HEADS), ones]).reshape(1, n)
    zeros = jnp.zeros((gd,), F32)
    flag = jnp.concatenate([zeros, zeros, ones, zeros, ones, zeros]).reshape(1, n)
    tm = _tile(m, 2048, SUBLANES)
    tn = 2 * gd

    def epi(acc, extras, o_ref):
        parts = [_group_rms(acc[:, c * NSA_HEAD_DIM:(c + 1) * NSA_HEAD_DIM]) for c in range(tn // NSA_HEAD_DIM)]
        normed = jnp.concatenate(parts, axis=-1) * extras[0][...]
        o_ref[...] = jnp.where(extras[1][...] > 0.5, normed, acc)

    row = pl.BlockSpec((1, tn), lambda i, j, k: (0, j))
    return _mm(h, w_kv, tm=tm, tn=tn, epilogue=epi, extras=[(gain, row), (flag, row)], name="kv_proj")[0]


def _mm_q(h, w_in, g_q, b, t):
    m, d_model = h.shape
    g, d = NSA_KV_HEADS, NSA_HEAD_DIM
    hpg = d_model // d // g
    tn = hpg * d
    tm = _tile(t, 2048, SUBLANES) if t >= 256 else _tile(m, 1024, t)
    seqs = max(tm // t, 1)
    rows = min(tm, t)

    def epi(acc, extras, o_ref):
        for sq in range(seqs):
            for hh in range(hpg):
                a = acc[sq * rows:(sq + 1) * rows, hh * d:(hh + 1) * d]
                o_ref[sq, 0, hh] = _group_rms(a) * extras[0][...]

    gq = g_q.reshape(1, d).astype(F32)
    tiles_per_b = t // rows
    return _mm(h, w_in, n_out=g * tn, tm=tm, tn=tn, epilogue=epi,
               extras=[(gq, pl.BlockSpec((1, d), lambda i, j, k: (0, 0)))],
               out_shape=jax.ShapeDtypeStruct((b, g, hpg, t, d), F32),
               out_spec=pl.BlockSpec((seqs, 1, hpg, rows, d),
                                     lambda i, j, k: (i // tiles_per_b, j, 0, i % tiles_per_b, 0)),
               name="nsa_q")[0]


def _ret_kernel(*refs, c, cp, nc, hb, has_s0):
    if has_s0:
        q_ref, k_ref, v_ref, g_ref, cos_ref, sin_ref, mask_ref, qd_ref, kd_ref, cd_ref, s0_ref, o_ref, so_ref, s_ref = refs
    else:
        q_ref, k_ref, v_ref, g_ref, cos_ref, sin_ref, mask_ref, qd_ref, kd_ref, cd_ref, o_ref, so_ref, s_ref = refs
    ci = pl.program_id(2)
    dk, dv = RET_HEAD_DIM, RET_V_DIM

    @pl.when(ci == 0)
    def _():
        if has_s0:
            s_ref[...] = s0_ref[0]
        else:
            s_ref[...] = jnp.zeros_like(s_ref)

    def padded(a):
        if cp == c:
            return a
        return jnp.concatenate([a, jnp.zeros((cp - c, a.shape[1]), a.dtype)], axis=0)

    half = dk // 2
    cos, sin = cos_ref[...], sin_ref[...]

    def rot(a):
        a1, a2 = a[:, :half], a[:, half:]
        return jnp.concatenate([a1 * cos - a2 * sin, a1 * sin + a2 * cos], axis=-1)

    for hh in range(hb):
        q = rot(padded(q_ref[0, :, hh * dk:(hh + 1) * dk]))
        k = rot(padded(k_ref[0, :, hh * dk:(hh + 1) * dk])) * (dk ** -0.5)
        v = padded(v_ref[0, :, hh * dv:(hh + 1) * dv]).astype(BF16)
        s = s_ref[hh]
        att = lax.dot_general(q.astype(BF16), k.astype(BF16), (((1,), (1,)), ((), ())),
                              preferred_element_type=F32) * mask_ref[hh]
        o = (jnp.dot(att.astype(BF16), v, preferred_element_type=F32)
             + jnp.dot((q * qd_ref[hh]).astype(BF16), s.astype(BF16), preferred_element_type=F32))
        kt = jnp.transpose(k * kd_ref[hh]).astype(BF16)
        s_new = s * cd_ref[hh] + jnp.dot(kt, v, preferred_element_type=F32)
        s_ref[hh] = s_new
        o = _group_rms(o[:c])
        gate = g_ref[0, :, hh * dv:(hh + 1) * dv]
        o_ref[0, :, hh * dv:(hh + 1) * dv] = (gate * jax.nn.sigmoid(gate) * o).astype(o_ref.dtype)

    @pl.when(ci == nc - 1)
    def _():
        so_ref[0] = s_ref[...]


def _retention(qkvg, pos, s0):
    b, t, width = qkvg.shape
    dk, dv = RET_HEAD_DIM, RET_V_DIM
    h = width // (2 * dk + 2 * dv)
    c = RET_CHUNK if t % RET_CHUNK == 0 else t
    nc = t // c
    cp = max(c, LANES)
    half = dk // 2
    lg = jnp.log1p(-(2.0 ** (-5.0 - jnp.arange(h, dtype=F32))))
    idx = jnp.arange(c, dtype=F32)
    diff = idx[:, None] - idx[None, :]
    mask = jnp.where(diff >= 0, jnp.exp(jnp.maximum(diff, 0.0)[None] * lg[:, None, None]), 0.0)
    q_dec = jnp.exp((idx + 1.0)[None, :] * lg[:, None])
    k_dec = jnp.exp((c - 1.0 - idx)[None, :] * lg[:, None])
    c_dec = jnp.exp(c * lg)
    mask = jnp.pad(mask, ((0, 0), (0, cp - c), (0, cp - c)))
    q_dec = jnp.pad(q_dec, ((0, 0), (0, cp - c)))[..., None]
    k_dec = jnp.pad(k_dec, ((0, 0), (0, cp - c)))[..., None]
    c_dec = c_dec.reshape(h, 1, 1)
    inv = ROPE_BASE ** (-jnp.arange(half, dtype=F32) / half)
    ang = pos.astype(F32)[:, None] * inv[None, :]
    cos = jnp.pad(jnp.cos(ang), ((0, nc * cp - t), (0, 0)))
    sin = jnp.pad(jnp.sin(ang), ((0, nc * cp - t), (0, 0)))

    hb = _tile(h, 4, 1)
    ng = h // hb
    vb = (2 * h * dk) // (hb * dv)
    assert (2 * h * dk) % (hb * dv) == 0
    in_specs = [
        pl.BlockSpec((1, c, hb * dk), lambda bi, hi, ci: (bi, ci, hi)),
        pl.BlockSpec((1, c, hb * dk), lambda bi, hi, ci: (bi, ci, ng + hi)),
        pl.BlockSpec((1, c, hb * dv), lambda bi, hi, ci: (bi, ci, vb + hi)),
        pl.BlockSpec((1, c, hb * dv), lambda bi, hi, ci: (bi, ci, vb + ng + hi)),
        pl.BlockSpec((cp, half), lambda bi, hi, ci: (ci, 0)),
        pl.BlockSpec((cp, half), lambda bi, hi, ci: (ci, 0)),
        pl.BlockSpec((hb, cp, cp), lambda bi, hi, ci: (hi, 0, 0)),
        pl.BlockSpec((hb, cp, 1), lambda bi, hi, ci: (hi, 0, 0)),
        pl.BlockSpec((hb, cp, 1), lambda bi, hi, ci: (hi, 0, 0)),
        pl.BlockSpec((hb, 1, 1), lambda bi, hi, ci: (hi, 0, 0)),
    ]
    args = [qkvg, qkvg, qkvg, qkvg, cos, sin, mask, q_dec, k_dec, c_dec]
    if s0 is not None:
        in_specs.append(pl.BlockSpec((1, hb, dk, dv), lambda bi, hi, ci: (bi, hi, 0, 0)))
        args.append(s0)
    o, s_out = pl.pallas_call(
        functools.partial(_ret_kernel, c=c, cp=cp, nc=nc, hb=hb, has_s0=s0 is not None),
        grid=(b, ng, nc),
        in_specs=in_specs,
        out_specs=[pl.BlockSpec((1, c, hb * dv), lambda bi, hi, ci: (bi, ci, hi)),
                   pl.BlockSpec((1, hb, dk, dv), lambda bi, hi, ci: (bi, hi, 0, 0))],
        out_shape=[jax.ShapeDtypeStruct((b, t, h * dv), BF16), jax.ShapeDtypeStruct((b, h, dk, dv), F32)],
        scratch_shapes=[pltpu.VMEM((hb, dk, dv), F32)],
        compiler_params=_params(("parallel", "parallel", "arbitrary")),
        name="retention",
    )(*args)
    return o, s_out


def _router_kernel(h_ref, w_ref, b_ref, ids_ref, gates_ref):
    logits = jnp.dot(h_ref[...], w_ref[...].astype(BF16), preferred_element_type=F32) + b_ref[...]
    lane = lax.broadcasted_iota(jnp.int32, logits.shape, 1)
    big = jnp.int32(LANES)
    ng, ne = MOE_GROUPS, MOE_EXPERTS_PER_GROUP
    gl = jnp.where(lane < ng, logits, NEG_INF)
    gmax = jnp.max(gl, axis=-1, keepdims=True)
    gsum = jnp.sum(jnp.exp(gl - gmax), axis=-1, keepdims=True)
    g_sel = jnp.min(jnp.where(gl == gmax, lane, big), axis=-1, keepdims=True)
    g_w = 1.0 / gsum
    lo = ng + g_sel * ne
    in_group = (lane >= lo) & (lane < lo + ne)
    el = jnp.where(in_group, logits, NEG_INF)
    emax = jnp.max(el, axis=-1, keepdims=True)
    ee = jnp.exp(el - emax)
    ep = ee / jnp.sum(ee, axis=-1, keepdims=True)
    ep = jnp.where(in_group, ep, -1.0)
    p1 = jnp.max(ep, axis=-1, keepdims=True)
    i1 = jnp.min(jnp.where(ep == p1, lane, big), axis=-1, keepdims=True)
    ep2 = jnp.where(lane == i1, -1.0, ep)
    p2 = jnp.max(ep2, axis=-1, keepdims=True)
    i2 = jnp.min(jnp.where(ep2 == p2, lane, big), axis=-1, keepdims=True)
    psum = p1 + p2
    ids_ref[...] = jnp.where(lane == 0, i1 - ng, jnp.where(lane == 1, i2 - ng, 0))
    gates_ref[...] = jnp.where(lane == 0, g_w * p1 / psum, jnp.where(lane == 1, g_w * p2 / psum, 0.0))


def _router(hf, w_rg, b_rg, w_re, b_re):
    n, d = hf.shape
    ng, ne = MOE_GROUPS, MOE_EXPERTS_PER_GROUP
    w = jnp.concatenate([w_rg.astype(F32), jnp.transpose(w_re.astype(F32), (1, 0, 2)).reshape(d, ng * ne)], axis=1)
    w = jnp.pad(w, ((0, 0), (0, LANES - w.shape[1])))
    bias = jnp.pad(jnp.concatenate([b_rg.astype(F32), b_re.astype(F32).reshape(-1)]), (0, LANES - ng - ng * ne)).reshape(1, LANES)
    tm = _tile(n, 256, SUBLANES)
    ids, gates = pl.pallas_call(
        _router_kernel,
        grid=(n // tm,),
        in_specs=[pl.BlockSpec((tm, d), lambda i: (i, 0)), pl.BlockSpec((d, LANES), lambda i: (0, 0)),
                  pl.BlockSpec((1, LANES), lambda i: (0, 0))],
        out_specs=[pl.BlockSpec((tm, LANES), lambda i: (i, 0)), pl.BlockSpec((tm, LANES), lambda i: (i, 0))],
        out_shape=[jax.ShapeDtypeStruct((n, LANES), jnp.int32), jax.ShapeDtypeStruct((n, LANES), F32)],
        compiler_params=_params(("parallel",)),
        name="moe_router",
    )(hf, w, bias)
    return ids[:, :MOE_TOP], gates[:, :MOE_TOP]


def _row_copy(src, s_row, dst, d_row, sem):
    return pltpu.make_async_copy(src.at[pl.ds(s_row, 1)], dst.at[pl.ds(d_row, 1)], sem)


def _row_pitch(n_planes):
    return n_planes + (4 - n_planes) % SUBLANES


def _moe_up_kernel(be_ref, nu_ref, tokc_ref, tokn_ref, x_hbm, w_ref, h_ref, xbuf, sem, *, rows):
    b = pl.program_id(0)
    n_used = nu_ref[0]
    slot = lax.rem(b, 2)
    n_planes = x_hbm.shape[1]
    ROW_PITCH = _row_pitch(n_planes)

    def row_copy(tok_ref, s, r):
        dst = xbuf.at[pl.ds((s * rows + r) * ROW_PITCH, n_planes), :]
        return pltpu.make_async_copy(x_hbm.at[tok_ref[0, 0, r]], dst, sem.at[s])

    def gather(tok_ref, s):
        def body(r, carry):
            row_copy(tok_ref, s, r).start()
            return carry
        lax.fori_loop(0, rows, body, 0, unroll=8)

    @pl.when(b == 0)
    def _():
        gather(tokc_ref, 0)

    @pl.when(b + 1 < n_used)
    def _():
        gather(tokn_ref, 1 - slot)

    @pl.when(b < n_used)
    def _():
        def wait_body(r, carry):
            row_copy(tokc_ref, slot, r).wait()
            return carry
        lax.fori_loop(0, rows, wait_body, 0, unroll=8)
        f = h_ref.shape[1]
        base = slot * (rows * ROW_PITCH)
        acc = jnp.zeros((rows, 2 * f), F32)
        for j in range(0, n_planes, 2):
            x2 = jnp.concatenate([xbuf[pl.ds(base + j, rows, stride=ROW_PITCH), :],
                                  xbuf[pl.ds(base + j + 1, rows, stride=ROW_PITCH), :]], axis=1)
            acc = acc + jnp.dot(x2.astype(BF16), w_ref[0, 0, j * LANES:(j + 2) * LANES, :].astype(BF16),
                                preferred_element_type=F32)
        a, g = acc[:, :f], acc[:, f:]
        h_ref[...] = (a * jax.nn.sigmoid(a) * g).astype(h_ref.dtype)

    @pl.when(b >= n_used)
    def _():
        h_ref[...] = jnp.zeros_like(h_ref)


def _moe_down_kernel(be_ref, nu_ref, nv_ref, dstc_ref, dstp_ref, h_ref, g_ref, w_ref, y_hbm, ybuf, sem, *, nb):
    b = pl.program_id(0)
    n_used = nu_ref[0]
    slot = lax.rem(b, 2)

    def scatter(dst_ref, s, count, wait):
        def body(r, carry):
            cp = _row_copy(ybuf.at[s], r, y_hbm, dst_ref[0, 0, r], sem.at[s])
            if wait:
                cp.wait()
            else:
                cp.start()
            return carry
        lax.fori_loop(0, count, body, 0)

    @pl.when(b < n_used)
    def _():
        y = jnp.dot(h_ref[...], w_ref[0, 0].astype(BF16), preferred_element_type=F32) * g_ref[...]
        ybuf[slot] = y
        scatter(dstc_ref, slot, nv_ref[b], False)

    @pl.when((b >= 1) & (b - 1 < n_used))
    def _():
        scatter(dstp_ref, 1 - slot, nv_ref[jnp.maximum(b - 1, 0)], True)

    @pl.when((b == nb - 1) & (b < n_used))
    def _():
        scatter(dstc_ref, slot, nv_ref[b], True)


def _moe(h16, hf3, layer, w_rg, b_rg, w_re, b_re, w_up, w_down):
    n, d = h16.shape
    e, f2 = w_up.shape[1], w_up.shape[3]
    f = f2 // 2
    rows = MOE_ROWS
    ids, gates = _router(h16, w_rg, b_rg, w_re, b_re)

    a = n * MOE_TOP
    nb = -(-a // rows) + e
    e_flat = ids.reshape(-1)
    a_idx = jnp.arange(a, dtype=jnp.int32)
    tok_flat = a_idx // MOE_TOP
    _, tok_s, dst_s, gate_s = lax.sort(
        (e_flat, tok_flat, (a_idx % MOE_TOP) * n + tok_flat, lax.bitcast_convert_type(gates.reshape(-1), jnp.int32)),
        num_keys=1, is_stable=True)
    packed = jnp.stack([tok_s, dst_s, gate_s, jnp.zeros_like(tok_s)], axis=1)
    counts = jnp.bincount(e_flat, length=e).astype(jnp.int32)
    starts = jnp.cumsum(counts) - counts
    blocks_per = (counts + rows - 1) // rows
    blk_end = jnp.cumsum(blocks_per)
    first_blk = blk_end - blocks_per
    n_used = blk_end[-1].astype(jnp.int32)
    blk_ids = jnp.arange(nb, dtype=jnp.int32)
    owner = jnp.minimum(jnp.searchsorted(blk_end, blk_ids, side="right"), e - 1).astype(jnp.int32)
    in_e0 = (blk_ids - first_blk[owner]) * rows
    n_valid = jnp.where(blk_ids < n_used, jnp.clip(counts[owner] - in_e0, 0, rows), 0).astype(jnp.int32)
    within = jnp.arange(rows, dtype=jnp.int32)[None, :]
    live = within < n_valid[:, None]
    src = jnp.clip((starts[owner] + in_e0)[:, None] + within, 0, a - 1)
    picked = packed[src]
    tok3 = jnp.where(live, picked[..., 0], 0).reshape(nb, 1, rows)
    dst3 = jnp.where(live, picked[..., 1], 0).reshape(nb, 1, rows)
    gate_buf = jnp.where(live, lax.bitcast_convert_type(picked[..., 2], F32), 0.0)
    block_expert = owner[jnp.minimum(blk_ids, n_used - 1)]
    n_used_arr = n_used.reshape(1)
    smem_blk = functools.partial(pl.BlockSpec, (1, 1, rows), memory_space=pltpu.SMEM)
    h_mid = pl.pallas_call(
        functools.partial(_moe_up_kernel, rows=rows),
        grid_spec=pltpu.PrefetchScalarGridSpec(
            num_scalar_prefetch=2,
            grid=(nb,),
            in_specs=[smem_blk(lambda b, be, nu: (b, 0, 0)),
                      smem_blk(lambda b, be, nu: (jnp.minimum(b + 1, nb - 1), 0, 0)),
                      pl.BlockSpec(memory_space=pl.ANY),
                      pl.BlockSpec((1, 1, d, f2), lambda b, be, nu: (layer, be[b], 0, 0))],
            out_specs=pl.BlockSpec((rows, f), lambda b, be, nu: (b, 0)),
            scratch_shapes=[pltpu.VMEM((2 * rows * _row_pitch(d // LANES), LANES), F32),
                            pltpu.SemaphoreType.DMA((2,))]),
        out_shape=jax.ShapeDtypeStruct((nb * rows, f), BF16),
        compiler_params=_params(("arbitrary",)),
        name="moe_up",
    )(block_expert, n_used_arr, tok3, tok3, hf3, w_up)

    y_tok = pl.pallas_call(
        functools.partial(_moe_down_kernel, nb=nb),
        grid_spec=pltpu.PrefetchScalarGridSpec(
            num_scalar_prefetch=3,
            grid=(nb,),
            in_specs=[smem_blk(lambda b, be, nu, nv: (b, 0, 0)),
                      smem_blk(lambda b, be, nu, nv: (jnp.maximum(b - 1, 0), 0, 0)),
                      pl.BlockSpec((rows, f), lambda b, be, nu, nv: (b, 0)),
                      pl.BlockSpec((rows, 1), lambda b, be, nu, nv: (b, 0)),
                      pl.BlockSpec((1, 1, f, d), lambda b, be, nu, nv: (layer, be[b], 0, 0))],
            out_specs=pl.BlockSpec(memory_space=pl.ANY),
            scratch_shapes=[pltpu.VMEM((2, rows, d), F32), pltpu.SemaphoreType.DMA((2,))]),
        out_shape=jax.ShapeDtypeStruct((MOE_TOP * n, d), F32),
        compiler_params=_params(("arbitrary",)),
        name="moe_down",
    )(block_expert, n_used_arr, n_valid, dst3, dst3, h_mid, gate_buf.reshape(nb * rows, 1), w_down)
    return y_tok


def _page_specs(cache_shape, n_pages, pgs):
    return [pl.BlockSpec((1,) + tuple(cache_shape[1:]),
                         functools.partial(lambda bi, j, pt, r: (pt[bi * n_pages + j * pgs + r], 0, 0, 0, 0), r=r))
            for r in range(pgs)]


def _cmp_paged_kernel(pt_ref, *refs, pgs, g, d):
    page_refs = refs[:pgs]
    perm_ref, wk_ref, wv_ref, uk_ref, uv_ref = refs[pgs:]
    cs = CMP_STRIDE
    page = page_refs[0].shape[1]
    cpp = page // cs
    xs = [[[jnp.dot(perm_ref[...], pr[0, :, kv, gi, :].astype(BF16), preferred_element_type=F32)
            for gi in range(g)] for kv in range(2)] for pr in page_refs]
    m = g * pgs * cpp
    accs = [jnp.zeros((m, wk_ref.shape[1]), F32), jnp.zeros((m, wv_ref.shape[1]), F32)]
    for pp in range(0, cs, 2):
        for kv, w_ref in enumerate((wk_ref, wv_ref)):
            halves = []
            for p in (pp, pp + 1):
                pieces = [xs[r][kv][gi][p * cpp:(p + 1) * cpp, :] for gi in range(g) for r in range(pgs)]
                halves.append(jnp.concatenate(pieces, axis=0))
            lhs = jnp.concatenate(halves, axis=1).astype(BF16)
            accs[kv] = accs[kv] + jnp.dot(lhs, w_ref[pp * d:(pp + 2) * d, :], preferred_element_type=F32)
    uk_ref[0] = accs[0].reshape(g, pgs * cpp, wk_ref.shape[1])
    uv_ref[0] = accs[1].reshape(g, pgs * cpp, wv_ref.shape[1])


def _cmp_hidden_paged(cache, page_table, wcat_k, wcat_v):
    page = cache.shape[1]
    g, d = cache.shape[3], cache.shape[4]
    b, n_pages = page_table.shape
    pgs = _tile(n_pages, 8, 1)
    cpp = page // CMP_STRIDE
    nch = n_pages * cpp
    perm = np.zeros((page, page), np.float32)
    for c in range(cpp):
        for p in range(CMP_STRIDE):
            perm[p * cpp + c, c * CMP_STRIDE + p] = 1.0
    hid2 = wcat_k.shape[1]
    const = lambda shape: pl.BlockSpec(shape, lambda bi, j, pt: (0, 0))
    uk, uv = pl.pallas_call(
        functools.partial(_cmp_paged_kernel, pgs=pgs, g=g, d=d),
        grid_spec=pltpu.PrefetchScalarGridSpec(
            num_scalar_prefetch=1,
            grid=(b, n_pages // pgs),
            in_specs=_page_specs(cache.shape, n_pages, pgs)
            + [const((page, page)), const(wcat_k.shape), const(wcat_v.shape)],
            out_specs=[pl.BlockSpec((1, g, pgs * cpp, hid2), lambda bi, j, pt: (bi, 0, j, 0))] * 2),
        out_shape=[jax.ShapeDtypeStruct((b, g, nch, hid2), F32)] * 2,
        compiler_params=_params(("parallel", "parallel")),
        name="cmp_hidden_paged",
    )(page_table.reshape(-1).astype(jnp.int32), *([cache] * pgs), jnp.asarray(perm, BF16),
      wcat_k.astype(BF16), wcat_v.astype(BF16))
    return uk.reshape(b * g, nch, hid2), uv.reshape(b * g, nch, hid2)
def _cmp_post_kernel(u_ref, pe_ref, w2_ref, g_ref, o_ref, *, n_cmp, norm):
    u = u_ref[0]
    nch, hid2 = u.shape
    hid = hid2 // 2
    nxt = pltpu.roll(u[:, hid:], nch - 1, axis=0)
    x = (pe_ref[0:1, :] + u[:, :hid]) + nxt
    y = 0.5 * x * (1.0 + jnp.tanh(0.7978845608028654 * (x + 0.044715 * (x * x * x))))
    z = jnp.dot(y.astype(BF16), w2_ref[...].astype(BF16), preferred_element_type=F32)
    if norm:
        z = _group_rms(z) * g_ref[...]
    row = lax.broadcasted_iota(jnp.int32, z.shape, 0)
    o_ref[0] = jnp.where(row < n_cmp, z, 0.0)


def _cmp_wcat(w1):
    r = CMP_LEN // CMP_STRIDE
    assert r == 2
    w1r = w1.reshape(r, w1.shape[0] // r, w1.shape[1])
    return jnp.concatenate([w1r[0], w1r[1]], axis=1)


def _cmp_hidden_dense(rows, nch, wcat):
    b, _, g, d = rows.shape
    ch = rows[:, :nch * CMP_STRIDE].reshape(b, nch, CMP_STRIDE, g, d)
    ch = jnp.transpose(ch, (0, 3, 1, 2, 4)).reshape(b * g * nch, CMP_STRIDE * d).astype(BF16)
    return _mm(ch, wcat, name="cmp_hidden")[0].reshape(b * g, nch, wcat.shape[1])


def _cmp_finish(u, n_cmp, pe, w1, w2, g_k):
    bg, nch, hid2 = u.shape
    hid = hid2 // 2
    d = w2.shape[1]
    pe_rows = jnp.pad(pe.reshape(1, -1), ((0, SUBLANES - 1), (0, 0)))
    pe_hid = _mm(pe_rows, w1, name="cmp_pe")[0]
    gain = (jnp.ones((d,), F32) if g_k is None else g_k.astype(F32)).reshape(1, d)
    return pl.pallas_call(
        functools.partial(_cmp_post_kernel, n_cmp=n_cmp, norm=g_k is not None),
        grid=(bg,),
        in_specs=[pl.BlockSpec((1, nch, hid2), lambda i: (i, 0, 0)), pl.BlockSpec((SUBLANES, hid), lambda i: (0, 0)),
                  pl.BlockSpec((hid, d), lambda i: (0, 0)), pl.BlockSpec((1, d), lambda i: (0, 0))],
        out_specs=pl.BlockSpec((1, nch, d), lambda i: (i, 0, 0)),
        out_shape=jax.ShapeDtypeStruct((bg, nch, d), F32),
        compiler_params=_params(("parallel",)),
        name="cmp_post",
    )(u, pe_hid, w2, gain)


def _slc_map(n_cmp, n_slc, rows, cols):
    a = SLC_BLOCK // CMP_STRIDE
    bb = CMP_LEN // CMP_STRIDE
    j = np.arange(n_slc)[:, None, None]
    i = j * a + np.arange(a)[None, :, None] + np.arange(bb)[None, None, :] - bb + 1
    i, jj = np.broadcast_arrays(i, j)
    ok = (i >= 0) & (i < n_cmp)
    m = np.zeros((rows, cols), np.float32)
    np.add.at(m, (i[ok], jj[ok]), 1.0)
    return jnp.asarray(m)


def _masked_softmax_rows(s):
    m = jnp.max(s, axis=-1, keepdims=True)
    e = jnp.exp(s - jnp.where(m > NEG_INF, m, 0.0))
    den = jnp.sum(e, axis=-1, keepdims=True)
    return e / jnp.where(den > 0, den, 1.0)


def _store_gated(o, gl_ref, prev_ref, o_ref, branch, hpg, tq):
    d = NSA_HEAD_DIM
    gate = jax.nn.sigmoid(gl_ref[0])
    for hh in range(hpg):
        c = hh * N_BRANCH + branch
        val = gate[:, c:c + 1] * o[hh * tq:(hh + 1) * tq]
        if prev_ref is not None:
            val = prev_ref[0, :, hh * d:(hh + 1) * d] + val
        o_ref[0, :, hh * d:(hh + 1) * d] = val.astype(o_ref.dtype)


def _nsa_cmp_kernel(q_ref, kc_ref, vc_ref, map_ref, gl_ref, o_ref, sel_ref, *, hpg, tq, pos0, n_cmp, n_slc, n_top):
    i = pl.program_id(2)
    rws = hpg * tq
    q = q_ref[0, 0].reshape(rws, NSA_HEAD_DIM).astype(BF16)
    kc = kc_ref[0].astype(BF16)
    s = lax.dot_general(q, kc, (((1,), (1,)), ((), ())), preferred_element_type=F32) * NSA_SCALE
    ncp = s.shape[1]
    tok = lax.broadcasted_iota(jnp.int32, (rws, 1), 0) & (tq - 1)
    pos = pos0 + i * tq + tok
    cidx = lax.broadcasted_iota(jnp.int32, (1, ncp), 1)
    ok = (cidx * CMP_STRIDE + (CMP_LEN - 1) <= pos) & (cidx < n_cmp)
    p = _masked_softmax_rows(jnp.where(ok, s, NEG_INF))
    o = jnp.dot(p.astype(BF16), vc_ref[0].astype(BF16), preferred_element_type=F32)
    _store_gated(o, gl_ref, None, o_ref, 0, hpg, tq)

    psum = jnp.sum(p.astype(BF16).astype(F32).reshape(hpg, tq, ncp), axis=0)
    imp = jnp.dot(psum, map_ref[...], precision=lax.Precision.HIGHEST, preferred_element_type=F32)
    nsp = imp.shape[1]
    posq = pos0 + i * tq + lax.broadcasted_iota(jnp.int32, (tq, 1), 0)
    blk = lax.broadcasted_iota(jnp.int32, (tq, nsp), 1)
    back = (posq >> SLC_SHIFT) - blk
    real = blk < n_slc
    valid = (blk * SLC_BLOCK <= posq) & real
    forced = (blk == 0) | ((back >= 0) & (back < SLC_LOCAL))
    score = jnp.where(valid, imp + jnp.where(forced, FORCE_BONUS, 0.0), NEG_INF)
    if tq % LANES == 0 and n_slc <= LANES:
        nr = -(-n_slc // SUBLANES) * SUBLANES
        st = jnp.transpose(score)[:nr]
        blk_t = lax.broadcasted_iota(jnp.int32, (nr, tq), 0)
        rank = jnp.zeros((nr, tq), jnp.int32)
        for kb in range(n_slc):
            row = st[kb:kb + 1, :]
            ahead = (row > st) | ((row == st) & (blk_t > kb))
            rank = rank + ahead.astype(jnp.int32)
        sel_t = jnp.where((rank < n_top) & (blk_t < n_slc), 1.0, 0.0)
        sel_t = jnp.concatenate([sel_t, jnp.zeros((nsp - nr, tq), F32)], axis=0)
        sel_ref[0, 0] = jnp.transpose(sel_t)
    else:
        rank = jnp.zeros((tq, nsp), jnp.int32)
        for kb in range(n_slc):
            col = score[:, kb:kb + 1]
            ahead = (col > score) | ((col == score) & (blk > kb))
            rank = rank + ahead.astype(jnp.int32)
        sel_ref[0, 0] = jnp.where((rank < n_top) & real, 1.0, 0.0)


def _nsa_cmp(qn, kc, vc, glog, pos0, n_cmp, n_slc):
    b, g, hpg, t, d = qn.shape
    ncp = kc.shape[1]
    nsp = -(-n_slc // LANES) * LANES
    tq = _tile(t, 128, SUBLANES)
    smap = _slc_map(n_cmp, n_slc, ncp, nsp)
    n_top = min(SLC_TOP, n_slc)
    return pl.pallas_call(
        functools.partial(_nsa_cmp_kernel, hpg=hpg, tq=tq, pos0=pos0, n_cmp=n_cmp, n_slc=n_slc, n_top=n_top),
        grid=(b, g, t // tq),
        in_specs=[pl.BlockSpec((1, 1, hpg, tq, d), lambda bi, gi, i: (bi, gi, 0, i, 0)),
                  pl.BlockSpec((1, ncp, d), lambda bi, gi, i: (bi * g + gi, 0, 0)),
                  pl.BlockSpec((1, ncp, d), lambda bi, gi, i: (bi * g + gi, 0, 0)),
                  pl.BlockSpec((ncp, nsp), lambda bi, gi, i: (0, 0)),
                  pl.BlockSpec((1, tq, LANES), lambda bi, gi, i: (bi, i, gi))],
        out_specs=[pl.BlockSpec((1, tq, hpg * d), lambda bi, gi, i: (bi, i, gi)),
                   pl.BlockSpec((1, 1, tq, nsp), lambda bi, gi, i: (bi, gi, i, 0))],
        out_shape=[jax.ShapeDtypeStruct((b, t, g * hpg * d), F32), jax.ShapeDtypeStruct((b, g, t, nsp), F32)],
        compiler_params=_params(("parallel", "parallel", "parallel")),
        name="nsa_cmp",
    )(qn, kc, vc, smap, glog)


def _nsa_attn_kernel(*refs, hpg, tq, tk, n_kt, qpos0, kpos0, window, use_sel, branch):
    refs = list(refs)
    q_ref, k_ref, v_ref = refs[:3]
    refs = refs[3:]
    sel_ref = None
    if use_sel:
        sel_ref = refs[0]
        refs = refs[1:]
    gl_ref, prev_ref, o_ref, qs_ref, s_ref, p_ref, bias_ref, m_ref, a_ref, acc_ref = refs
    i = pl.program_id(2)
    d = NSA_HEAD_DIM
    rws = hpg * tq
    rb = min(tq, 64)
    per_head = tq // rb
    pvb = min(rws, 512)
    qs_ref[...] = (q_ref[0, 0].reshape(rws, d) * NSA_SCALE).astype(BF16)
    posq = qpos0 + i * tq + lax.broadcasted_iota(jnp.int32, (tq, 1), 0)
    sel = sel_ref[0, 0].astype(BF16) if use_sel else None
    m_ref[...] = jnp.full_like(m_ref, NEG_INF)
    acc_ref[...] = jnp.zeros_like(acc_ref)
    ones = jnp.ones((tk, d), BF16)

    def body(kt, carry):
        key0 = kt * tk
        off = pl.multiple_of(key0, tk)
        k = k_ref[0, pl.ds(off, tk), :].astype(BF16)
        v1 = jnp.concatenate([v_ref[0, pl.ds(off, tk), :].astype(BF16), ones], axis=1)
        s_ref[...] = lax.dot_general(qs_ref[...], k, (((1,), (1,)), ((), ())), preferred_element_type=F32)
        kpos = kpos0 + key0 + lax.broadcasted_iota(jnp.int32, (1, tk), 1)
        ok = kpos <= posq
        if window is not None:
            ok = ok & (posq - kpos < window)
        if use_sel:
            nsp = sel.shape[1]
            kblk = (key0 + lax.broadcasted_iota(jnp.int32, (nsp, tk), 1)) >> SLC_SHIFT
            expand = (kblk == lax.broadcasted_iota(jnp.int32, (nsp, tk), 0)).astype(BF16)
            ok = ok & (jnp.dot(sel, expand, preferred_element_type=F32) > 0.5)
        bias_ref[...] = jnp.where(ok, 0.0, NEG_INF)
        for blk in range(rws // rb):
            rows = slice(blk * rb, (blk + 1) * rb)
            part = blk % per_head
            s = s_ref[rows, :] + bias_ref[part * rb:(part + 1) * rb, :]
            m_old = m_ref[rows, :]
            m_new = jnp.maximum(m_old, jnp.max(s, axis=-1, keepdims=True))
            m_safe = jnp.where(m_new > NEG_INF, m_new, 0.0)
            p_ref[rows, :] = jnp.exp(s - jnp.tile(m_safe, (1, tk // LANES))).astype(BF16)
            a_ref[rows, :] = jnp.exp(m_old - m_safe)
            m_ref[rows, :] = m_new
        for r0 in range(0, rws, pvb):
            rows = slice(r0, r0 + pvb)
            pv = jnp.dot(p_ref[rows, :], v1, preferred_element_type=F32)
            acc_ref[rows, :] = jnp.tile(a_ref[rows, :], (1, 2)) * acc_ref[rows, :] + pv
        return carry

    q_lo = qpos0 + i * tq
    q_hi = q_lo + tq - 1
    hi = jnp.clip((q_hi - kpos0) // tk + 1, 0, n_kt)
    if window is None:
        lo = 0
    else:
        lo = jnp.clip((q_lo - (window - 1) - kpos0) // tk, 0, n_kt)
    lax.fori_loop(lo, hi, body, 0)
    l = acc_ref[:, d:]
    o = acc_ref[:, :d] / jnp.where(l > 0, l, 1.0)
    _store_gated(o, gl_ref, prev_ref, o_ref, branch, hpg, tq)


def _nsa_attn(qn, kv, glog, prev, *, qpos0, kpos0, branch, out_dtype, window=None, sel=None, tk=512):
    b, g, hpg, t, d = qn.shape
    tk_total = kv.shape[1]
    tq = _tile(t, 256, SUBLANES)
    tk = _tile(tk_total, tk, LANES)
    n_kt = tk_total // tk
    in_specs = [pl.BlockSpec((1, 1, hpg, tq, d), lambda bi, gi, i: (bi, gi, 0, i, 0)),
                pl.BlockSpec((1, tk_total, d), lambda bi, gi, i: (bi, 0, gi)),
                pl.BlockSpec((1, tk_total, d), lambda bi, gi, i: (bi, 0, g + gi))]
    args = [qn, kv, kv]
    if sel is not None:
        nsp = sel.shape[-1]
        in_specs.append(pl.BlockSpec((1, 1, tq, nsp), lambda bi, gi, i: (bi, gi, i, 0)))
        args.append(sel)
    in_specs += [pl.BlockSpec((1, tq, LANES), lambda bi, gi, i: (bi, i, gi)),
                 pl.BlockSpec((1, tq, hpg * d), lambda bi, gi, i: (bi, i, gi))]
    args += [glog, prev]
    rws = hpg * tq
    return pl.pallas_call(
        functools.partial(_nsa_attn_kernel, hpg=hpg, tq=tq, tk=tk, n_kt=n_kt, qpos0=qpos0, kpos0=kpos0,
                          window=window, use_sel=sel is not None, branch=branch),
        grid=(b, g, t // tq),
        in_specs=in_specs,
        out_specs=pl.BlockSpec((1, tq, hpg * d), lambda bi, gi, i: (bi, i, gi)),
        out_shape=jax.ShapeDtypeStruct((b, t, g * hpg * d), out_dtype),
        scratch_shapes=[pltpu.VMEM((rws, d), BF16), pltpu.VMEM((rws, tk), F32), pltpu.VMEM((rws, tk), BF16),
                        pltpu.VMEM((tq, tk), F32), pltpu.VMEM((rws, LANES), F32), pltpu.VMEM((rws, LANES), F32),
                        pltpu.VMEM((rws, 2 * d), F32)],
        compiler_params=_params(("parallel", "parallel", "parallel")),
        name="nsa_attn_%d" % branch,
    )(*args)


def _nsa_slc_paged_kernel(pt_ref, *refs, pgs, g, hpg, tq, qpos0, past_len, n_steps):
    page_refs = refs[:pgs]
    q_ref, sel_ref, tail_ref, gl_ref, prev_ref, o_ref, m_ref, l_ref, acc_ref = refs[pgs:]
    j = pl.program_id(1)
    d = NSA_HEAD_DIM
    rws = hpg * tq
    page = page_refs[0].shape[1]
    nsp = sel_ref.shape[-1]
    posq = qpos0 + lax.broadcasted_iota(jnp.int32, (tq, 1), 0)
    sel_all = sel_ref[0].reshape(g * tq, nsp).astype(BF16)

    @pl.when(j == 0)
    def _():
        m_ref[...] = jnp.full_like(m_ref, NEG_INF)
        l_ref[...] = jnp.zeros_like(l_ref)
        acc_ref[...] = jnp.zeros_like(acc_ref)

    def update(plane, n, key0):
        ok_pos = key0 + lax.broadcasted_iota(jnp.int32, (1, n), 1) <= posq
        kblk = (key0 + lax.broadcasted_iota(jnp.int32, (nsp, n), 1)) >> SLC_SHIFT
        expand = (kblk == lax.broadcasted_iota(jnp.int32, (nsp, n), 0)).astype(BF16)
        picked = jnp.dot(sel_all, expand, preferred_element_type=F32) > 0.5
        for gi in range(g):
            k = plane(0, gi).astype(BF16)
            v = plane(1, gi).astype(BF16)
            q = q_ref[0, gi].reshape(rws, d).astype(BF16)
            s = lax.dot_general(q, k, (((1,), (1,)), ((), ())), preferred_element_type=F32) * NSA_SCALE
            ok = ok_pos & picked[gi * tq:(gi + 1) * tq]
            s = jnp.where(ok[None], s.reshape(hpg, tq, n), NEG_INF).reshape(rws, n)
            m_old = m_ref[gi]
            m_new = jnp.maximum(m_old, jnp.max(s, axis=-1, keepdims=True))
            m_safe = jnp.where(m_new > NEG_INF, m_new, 0.0)
            p = jnp.exp(s - m_safe)
            alpha = jnp.exp(m_old - m_safe)
            l_ref[gi] = alpha * l_ref[gi] + jnp.sum(p, axis=-1, keepdims=True)
            acc_ref[gi] = alpha * acc_ref[gi] + jnp.dot(p.astype(BF16), v, preferred_element_type=F32)
            m_ref[gi] = m_new

    update(lambda kv, gi: jnp.concatenate([pr[0, :, kv, gi, :] for pr in page_refs], axis=0),
           pgs * page, j * (pgs * page))

    @pl.when(j == n_steps - 1)
    def _():
        update(lambda kv, gi: tail_ref[0, :, (kv * g + gi) * d:(kv * g + gi + 1) * d], tail_ref.shape[1], past_len)
        gate = jax.nn.sigmoid(gl_ref[0])
        for gi in range(g):
            l = l_ref[gi]
            o = acc_ref[gi] / jnp.where(l > 0, l, 1.0)
            for hh in range(hpg):
                c = gi * LANES + hh * N_BRANCH + 1
                col = (gi * hpg + hh) * d
                o_ref[0, :, col:col + d] = prev_ref[0, :, col:col + d] + gate[:, c:c + 1] * o[hh * tq:(hh + 1) * tq]


def _nsa_slc_paged(qn, cache, page_table, sel, tail, glog, prev, *, qpos0):
    b, g, hpg, t, d = qn.shape
    page = cache.shape[1]
    width = 2 * g * d
    n_pages = page_table.shape[1]
    pgs = _tile(n_pages, 8, 1)
    n_steps = n_pages // pgs
    nsp = sel.shape[-1]
    nt = tail.shape[1]
    rws = hpg * t
    return pl.pallas_call(
        functools.partial(_nsa_slc_paged_kernel, pgs=pgs, g=g, hpg=hpg, tq=t, qpos0=qpos0,
                          past_len=n_pages * page, n_steps=n_steps),
        grid_spec=pltpu.PrefetchScalarGridSpec(
            num_scalar_prefetch=1,
            grid=(b, n_steps),
            in_specs=_page_specs(cache.shape, n_pages, pgs)
            + [pl.BlockSpec((1, g, hpg, t, d), lambda bi, j, pt: (bi, 0, 0, 0, 0)),
               pl.BlockSpec((1, g, t, nsp), lambda bi, j, pt: (bi, 0, 0, 0)),
               pl.BlockSpec((1, nt, width), lambda bi, j, pt: (bi, 0, 0)),
               pl.BlockSpec((1, t, g * LANES), lambda bi, j, pt: (bi, 0, 0)),
               pl.BlockSpec((1, t, g * hpg * d), lambda bi, j, pt: (bi, 0, 0))],
            out_specs=pl.BlockSpec((1, t, g * hpg * d), lambda bi, j, pt: (bi, 0, 0)),
            scratch_shapes=[pltpu.VMEM((g, rws, 1), F32), pltpu.VMEM((g, rws, 1), F32), pltpu.VMEM((g, rws, d), F32)]),
        out_shape=jax.ShapeDtypeStruct((b, t, g * hpg * d), F32),
        compiler_params=_params(("parallel", "arbitrary")),
        name="nsa_slc_paged",
    )(page_table.reshape(-1).astype(jnp.int32), *([cache] * pgs), qn, sel, tail, glog, prev)


def _gate_weight(w_in, d_model):
    g = NSA_KV_HEADS
    hpg = d_model // NSA_HEAD_DIM // g
    wg = w_in[:, d_model:].reshape(d_model, g, hpg * N_BRANCH)
    wg = jnp.pad(wg, ((0, 0), (0, 0), (0, LANES - hpg * N_BRANCH)))
    return wg.reshape(d_model, g * LANES)


def kernel(x_prompt, x_sample, state_ret, cache_cmp_kv, cache_slc_kv, cache_win_kv, page_table, p_prompt, p_sample, g_mix, g_ffn, w_ret_in, w_ret_out, w_nsa_in, g_nsa_q, w_nsa_out, g_kv, w_kv, g_k_cmp, g_k_slc, g_k_win, pe_cmp_k, w_cmp_k1, w_cmp_k2, pe_cmp_v, w_cmp_v1, w_cmp_v2, w_rg, b_rg, w_re, b_re, w_moe_up, w_moe_down, w_ple_up, g_ple, w_ple_gate):
    depth = g_mix.shape[0]
    n_a = w_ret_in.shape[0]
    g, d = NSA_KV_HEADS, NSA_HEAD_DIM
    gd = g * d
    d_model = x_prompt.shape[-1]
    page = cache_cmp_kv.shape[1]
    past_len = page_table.shape[1] * page

    groups = [
        dict(x=x_prompt.reshape(-1, d_model), p=p_prompt, b=x_prompt.shape[0], t=x_prompt.shape[1], pos0=0, s0=None),
        dict(x=x_sample.reshape(-1, d_model), p=p_sample, b=x_sample.shape[0], t=x_sample.shape[1], pos0=past_len, s0=state_ret),
    ]
    n_rows = [gr["x"].shape[0] for gr in groups]
    n_tok = sum(n_rows)
    offs = [0, n_rows[0]]
    for gr in groups:
        gr["ret"] = []

    for i in range(depth):
        w_gate_nsa = None if i < n_a else _gate_weight(w_nsa_in[i - n_a], d_model)
        for gr in groups:
            b, t = gr["b"], gr["t"]
            h = _rms([(gr["x"], 0)], g_mix[i], [BF16])[0]
            if i < n_a:
                qkvg = _mm(h, w_ret_in[i], tm=2048, name="ret_in")[0]
                pos = gr["pos0"] + jnp.arange(t)
                s0 = None if gr["s0"] is None else gr["s0"][i]
                o, s_new = _retention(qkvg.reshape(b, t, -1), pos, s0)
                gr["ret"].append(s_new)
                gr["x"] = _mm_resid(o.reshape(b * t, -1), w_ret_out[i], gr["x"], "ret_out")
            else:
                j = i - n_a
                qn = _mm_q(h, w_nsa_in[j], g_nsa_q[j], b, t)
                glog = _mm(h, w_gate_nsa, name="nsa_gate")[0].reshape(b, t, g * LANES)
                ctx = gr["ctx"]
                o1, sel = _nsa_cmp(qn, ctx["k_c"], ctx["v_c"], glog, gr["pos0"], ctx["n_cmp"], ctx["n_slc"])
                if ctx["slc_tail"] is None:
                    o2 = _nsa_attn(qn, ctx["slc"], glog, o1, qpos0=gr["pos0"], kpos0=0, branch=1, out_dtype=F32, sel=sel)
                else:
                    o2 = _nsa_slc_paged(qn, cache_slc_kv, page_table, sel, ctx["slc_tail"], glog, o1, qpos0=gr["pos0"])
                o3 = _nsa_attn(qn, ctx["win"], glog, o2, qpos0=gr["pos0"], kpos0=ctx["win_pos0"], branch=2,
                               out_dtype=BF16, window=WINDOW)
                gr["x"] = _mm_resid(o3.reshape(b * t, -1), w_nsa_out[j], gr["x"], "nsa_out")

        normed = [_rms([(gr["x"], 0)], g_ffn[i], [BF16], planes=True) for gr in groups]
        h16 = jnp.concatenate([nm[0] for nm in normed], axis=0)
        hf3 = jnp.concatenate([nm[1] for nm in normed], axis=0)
        y_tok = _moe(h16, hf3, i, w_rg[i], b_rg[i], w_re[i], b_re[i], w_moe_up, w_moe_down)

        for gi, gr in enumerate(groups):
            rows = n_rows[gi]
            x_new, hp = _rms([(gr["x"], 0), (y_tok, offs[gi]), (y_tok, n_tok + offs[gi])], g_ple[i], [BF16],
                             want_sum=True, rows=rows, tm=64)
            gr["x"] = _mm_ple(hp, w_ple_gate, i, x_new, gr["p"][i].reshape(rows, -1), w_ple_up)

        if i == n_a - 1:
            for gi, gr in enumerate(groups):
                b, t = gr["b"], gr["t"]
                hk = _rms([(gr["x"], 0)], g_kv, [BF16])[0]
                kv = _mm_kv(hk, w_kv, g_k_slc, g_k_win).reshape(b, t, 2 * N_BRANCH * gd)
                cmp_new, slc_new, win_new = kv[..., :2 * gd], kv[..., 2 * gd:4 * gd], kv[..., 4 * gd:]
                gr["cmp_new"], gr["slc_new"] = cmp_new, slc_new
                wcat_k, wcat_v = _cmp_wcat(w_cmp_k1), _cmp_wcat(w_cmp_v1)
                n_keys = gr["pos0"] + t
                n_cmp = (n_keys - CMP_LEN) // CMP_STRIDE + 1
                nch = n_cmp + CMP_LEN // CMP_STRIDE - 1
                if gi == 0:
                    slc_tail = None
                    win_keys, win_pos0 = win_new, 0
                    gr["win_state"] = win_new[:, t - min(WINDOW, t):]
                    cmp_rows = cmp_new.reshape(b, t, 2, g, d)
                    u_k = _cmp_hidden_dense(cmp_rows[:, :, 0], nch, wcat_k)
                    u_v = _cmp_hidden_dense(cmp_rows[:, :, 1], nch, wcat_v)
                else:
                    slc_tail = jnp.pad(slc_new, ((0, 0), (0, LANES - t), (0, 0)))
                    w_buf = cache_win_kv.shape[1]
                    win_all = jnp.concatenate([cache_win_kv.reshape(b, w_buf, 2 * gd), win_new], axis=1)
                    n_all = w_buf + t
                    gr["win_state"] = win_all[:, n_all - min(WINDOW, past_len + t):]
                    win_keys = jnp.pad(win_all, ((0, 0), (0, -n_all % LANES), (0, 0)))
                    win_pos0 = past_len - w_buf
                    assert nch * CMP_STRIDE == past_len
                    u_k, u_v = _cmp_hidden_paged(cache_cmp_kv, page_table, wcat_k, wcat_v)
                k_c = _cmp_finish(u_k, n_cmp, pe_cmp_k, w_cmp_k1, w_cmp_k2, g_k_cmp)
                v_c = _cmp_finish(u_v, n_cmp, pe_cmp_v, w_cmp_v1, w_cmp_v2, None)
                gr["ctx"] = dict(k_c=k_c, v_c=v_c, n_cmp=n_cmp, n_slc=-(-n_keys // SLC_BLOCK), slc=slc_new,
                                 slc_tail=slc_tail, win=win_keys, win_pos0=win_pos0)

    outs = []
    for gr in groups:
        outs.append(gr["x"].reshape(gr["b"], gr["t"], d_model))
    rets = [jnp.stack(gr["ret"]) for gr in groups]
    kvs = []
    for name in ("cmp_new", "slc_new", "win_state"):
        for gr in groups:
            a = gr[name]
            kvs.append(a.reshape(a.shape[0], a.shape[1], 2, g, d))
    return (outs[0], outs[1], rets[0], rets[1], kvs[0], kvs[1], kvs[2], kvs[3], kvs[4], kvs[5])
```

```python
import functools

import numpy as np
import jax
import jax.numpy as jnp
from jax import lax
from jax.experimental import pallas as pl
from jax.experimental.pallas import tpu as pltpu

F32 = jnp.float32
BF16 = jnp.bfloat16

RET_HEAD_DIM = 256
RET_V_DIM = 2 * RET_HEAD_DIM
RET_CHUNK = 128
ROPE_BASE = 10000.0
NSA_HEAD_DIM = 128
NSA_KV_HEADS = 4
NSA_SCALE = NSA_HEAD_DIM ** -0.5
CMP_LEN = 32
CMP_STRIDE = 16
SLC_BLOCK = 64
SLC_SHIFT = 6
SLC_TOP = 16
SLC_LOCAL = 2
FORCE_BONUS = 1e4
WINDOW = 512
N_BRANCH = 3
MOE_GROUPS = 8
MOE_EXPERTS_PER_GROUP = 8
MOE_TOP = 2
EPS = 1e-6

LANES = 128
SUBLANES = 8
VMEM_LIMIT_BYTES = 56 * 1024 * 1024
MOE_ROWS = 256
NEG_INF = float("-inf")


def _tile(n, pref, align):
    best = None
    for t in range(align, min(n, pref) + 1, align):
        if n % t == 0:
            best = t
    return n if best is None else best


def _params(semantics):
    return pltpu.CompilerParams(dimension_semantics=semantics, vmem_limit_bytes=VMEM_LIMIT_BYTES)


def _rms_kernel(*refs, n_add, want_sum, planes):
    adds, g_ref, outs = refs[:n_add], refs[n_add], refs[n_add + 1:]
    x = adds[0][...]
    for r in adds[1:]:
        x = x + r[...]
    y = (x * lax.rsqrt(jnp.mean(x * x, axis=-1, keepdims=True) + EPS)) * g_ref[...]
    if want_sum:
        outs[0][...] = x
        outs = outs[1:]
    if planes:
        for j in range(y.shape[1] // LANES):
            outs[-1][:, j, :] = y[:, j * LANES:(j + 1) * LANES]
        outs = outs[:-1]
    for o in outs:
        o[...] = y.astype(o.dtype)


def _rms(addends, g, out_dtypes, want_sum=False, rows=None, tm=128, planes=False):
    d = addends[0][0].shape[1]
    rows = addends[0][0].shape[0] if rows is None else rows
    tm = _tile(rows, tm, SUBLANES)
    for _, off in addends:
        assert off % tm == 0
    in_specs = [pl.BlockSpec((tm, d), functools.partial(lambda i, o: (i + o, 0), o=off // tm)) for _, off in addends]
    in_specs.append(pl.BlockSpec((1, d), lambda i: (0, 0)))
    dts = ([F32] if want_sum else []) + list(out_dtypes)
    out_specs = [pl.BlockSpec((tm, d), lambda i: (i, 0)) for _ in dts]
    out_shape = [jax.ShapeDtypeStruct((rows, d), dt) for dt in dts]
    if planes:
        out_specs.append(pl.BlockSpec((tm, d // LANES, LANES), lambda i: (i, 0, 0)))
        out_shape.append(jax.ShapeDtypeStruct((rows, d // LANES, LANES), F32))
    outs = pl.pallas_call(
        functools.partial(_rms_kernel, n_add=len(addends), want_sum=want_sum, planes=planes),
        grid=(rows // tm,),
        in_specs=in_specs,
        out_specs=out_specs,
        out_shape=out_shape,
        compiler_params=_params(("parallel",)),
        name="rms",
    )(*[a for a, _ in addends], g.reshape(1, d).astype(F32))
    return outs


def _mm_kernel(x_ref, w_ref, *rest, n_extra, epilogue, nk, in_place):
    extras, o_ref = rest[:n_extra], rest[n_extra]
    acc_ref = o_ref if in_place else rest[n_extra + 1]
    k = pl.program_id(2)

    def part():
        w = w_ref[0] if len(w_ref.shape) == 3 else w_ref[...]
        return jnp.dot(x_ref[...].astype(BF16), w.astype(BF16), preferred_element_type=F32)

    @pl.when(k == 0)
    def _():
        acc_ref[...] = part()

    @pl.when(k > 0)
    def _():
        acc_ref[...] += part()

    if epilogue is not None or not in_place:
        @pl.when(k == nk - 1)
        def _():
            acc = acc_ref[...]
            if epilogue is None:
                o_ref[...] = acc.astype(o_ref.dtype)
            else:
                epilogue(acc, extras, o_ref)


def _mm(x, w, *, layer=None, n_out=None, out_dtype=F32, tm=1024, tn=1024, tk=512, epilogue=None, extras=(),
        out_shape=None, out_spec=None, name="mm"):
    m, kdim = x.shape
    n = w.shape[-1] if n_out is None else n_out
    if m <= 256:
        tk = 2048
    tm, tn, tk = _tile(m, tm, SUBLANES), _tile(n, tn, LANES), _tile(kdim, tk, LANES)
    nk = kdim // tk
    in_place = out_shape is None and out_dtype == F32
    if out_shape is None:
        out_shape = jax.ShapeDtypeStruct((m, n), out_dtype)
        out_spec = pl.BlockSpec((tm, tn), lambda i, j, k: (i, j))
    return pl.pallas_call(
        functools.partial(_mm_kernel, n_extra=len(extras), epilogue=epilogue, nk=nk, in_place=in_place),
        grid=(m // tm, n // tn, nk),
        in_specs=[pl.BlockSpec((tm, tk), lambda i, j, k: (i, k)),
                  pl.BlockSpec((tk, tn), lambda i, j, k: (k, j)) if layer is None
                  else pl.BlockSpec((1, tk, tn), lambda i, j, k: (layer, k, j))]
        + [s for _, s in extras],
        out_specs=out_spec,
        out_shape=out_shape,
        scratch_shapes=[] if in_place else [pltpu.VMEM((tm, tn), F32)],
        compiler_params=_params(("parallel", "parallel", "arbitrary")),
        name=name,
    )(x, w, *[a for a, _ in extras]), (tm, tn)


def _tile_spec(tm, tn):
    return pl.BlockSpec((tm, tn), lambda i, j, k: (i, j))


def _mm_resid_kernel(x_ref, w_ref, r_ref, o_ref):
    def part():
        return jnp.dot(x_ref[...].astype(BF16), w_ref[...].astype(BF16), preferred_element_type=F32)

    @pl.when(pl.program_id(2) == 0)
    def _():
        o_ref[...] = r_ref[...] + part()

    @pl.when(pl.program_id(2) > 0)
    def _():
        o_ref[...] += part()


def _mm_resid(x, w, resid, name):
    m, n = resid.shape
    kdim = x.shape[1]
    tm, tn = _tile(m, 2048, SUBLANES), _tile(n, 1024, LANES)
    tk = _tile(kdim, 2048 if m <= 256 else 1024, LANES)
    return pl.pallas_call(
        _mm_resid_kernel,
        grid=(m // tm, n // tn, kdim // tk),
        in_specs=[pl.BlockSpec((tm, tk), lambda i, j, k: (i, k)), pl.BlockSpec((tk, tn), lambda i, j, k: (k, j)),
                  _tile_spec(tm, tn)],
        out_specs=_tile_spec(tm, tn),
        out_shape=jax.ShapeDtypeStruct((m, n), F32),
        compiler_params=_params(("parallel", "parallel", "arbitrary")),
        name=name,
    )(x, w, resid)


def _mm_ple(h, w_gate, layer, x, p, w_up):
    m, n = x.shape
    pdim = p.shape[1]
    tm, tn = _tile(m, 2048, SUBLANES), _tile(n, 1024, LANES)

    def epi(acc, extras, o_ref):
        pu =jnp.dot(extras[1][...].astype(BF16), extras[2][0].astype(BF16), preferred_element_type=F32)
        o_ref[...] = extras[0][...] + pu * jax.nn.sigmoid(acc)

    return _mm(h, w_gate, layer=layer, tm=tm, tn=tn, epilogue=epi,
               extras=[(x, _tile_spec(tm, tn)),
                       (p, pl.BlockSpec((tm, pdim), lambda i, j, k: (i, 0))),
                       (w_up, pl.BlockSpec((1, pdim, tn), lambda i, j, k: (layer, 0, j)))], name="ple_gate")[0]


def _group_rms(a):
    return a * lax.rsqrt(jnp.mean(a * a, axis=-1, keepdims=True) + EPS)


def _mm_kv(h, w_kv, g_k_slc, g_k_win):
    m = h.shape[0]
    gd = NSA_KV_HEADS * NSA_HEAD_DIM
    n = 2 * N_BRANCH * gd
    ones = jnp.ones((gd,), F32)
    gain = jnp.concatenate([ones, ones, jnp.tile(g_k_slc.astype(F32), NSA_KV_HEADS), ones,
                            jnp.tile(g_k_win.astype(F32), NSA_KV_HEADS), ones]).reshape(1, n)
    zeros = jnp.zeros((gd,), F32)
    flag = jnp.concatenate([zeros, zeros, ones, zeros, ones, zeros]).reshape(1, n)
    tm = _tile(m, 2048, SUBLANES)
    tn = 2 * gd

    def epi(acc, extras, o_ref):
        parts = [_group_rms(acc[:, c * NSA_HEAD_DIM:(c + 1) * NSA_HEAD_DIM]) for c in range(tn // NSA_HEAD_DIM)]
        normed = jnp.concatenate(parts, axis=-1) * extras[0][...]
        o_ref[...] = jnp.where(extras[1][...] > 0.5, normed, acc)

    row = pl.BlockSpec((1, tn), lambda i, j, k: (0, j))
    return _mm(h, w_kv, tm=tm, tn=tn, epilogue=epi, extras=[(gain, row), (flag, row)], name="kv_proj")[0]


def _mm_q(h, w_in, g_q, b, t):
    m, d_model = h.shape
    g, d = NSA_KV_HEADS, NSA_HEAD_DIM
    hpg = d_model // d // g
    tn = hpg * d
    tm = _tile(t, 2048, SUBLANES) if t >= 256 else _tile(m, 1024, t)
    seqs = max(tm // t, 1)
    rows = min(tm, t)

    def epi(acc, extras, o_ref):
        for sq in range(seqs):
            for hh in range(hpg):
                a = acc[sq * rows:(sq + 1) * rows, hh * d:(hh + 1) * d]
                o_ref[sq, 0, hh] = _group_rms(a) * extras[0][...]

    gq = g_q.reshape(1, d).astype(F32)
    tiles_per_b = t // rows
    return _mm(h, w_in, n_out=g * tn, tm=tm, tn=tn, epilogue=epi,
               extras=[(gq, pl.BlockSpec((1, d), lambda i, j, k: (0, 0)))],
               out_shape=jax.ShapeDtypeStruct((b, g, hpg, t, d), F32),
               out_spec=pl.BlockSpec((seqs, 1, hpg, rows, d),
                                     lambda i, j, k: (i // tiles_per_b, j, 0, i % tiles_per_b, 0)),
               name="nsa_q")[0]


def _ret_kernel(*refs, c, cp, nc, hb, has_s0):
    if has_s0:
        q_ref, k_ref, v_ref, g_ref, cos_ref, sin_ref, mask_ref, qd_ref, kd_ref, cd_ref, s0_ref, o_ref, so_ref, s_ref = refs
    else:
        q_ref, k_ref, v_ref, g_ref, cos_ref, sin_ref, mask_ref, qd_ref, kd_ref, cd_ref, o_ref, so_ref, s_ref = refs
    ci = pl.program_id(2)
    dk, dv = RET_HEAD_DIM, RET_V_DIM

    @pl.when(ci == 0)
    def _():
        if has_s0:
            s_ref[...] = s0_ref[0]
        else:
            s_ref[...] = jnp.zeros_like(s_ref)

    def padded(a):
        if cp == c:
            return a
        return jnp.concatenate([a, jnp.zeros((cp - c, a.shape[1]), a.dtype)], axis=0)

    half = dk // 2
    cos, sin = cos_ref[...], sin_ref[...]

    def rot(a):
        a1, a2 = a[:, :half], a[:, half:]
        return jnp.concatenate([a1 * cos - a2 * sin, a1 * sin + a2 * cos], axis=-1)

    for hh in range(hb):
        q = rot(padded(q_ref[0, :, hh * dk:(hh + 1) * dk]))
        k = rot(padded(k_ref[0, :, hh * dk:(hh + 1) * dk])) * (dk ** -0.5)
        v = padded(v_ref[0, :, hh * dv:(hh + 1) * dv]).astype(BF16)
        s = s_ref[hh]
        att = lax.dot_general(q.astype(BF16), k.astype(BF16), (((1,), (1,)), ((), ())),
                              preferred_element_type=F32) * mask_ref[hh]
        o = (jnp.dot(att.astype(BF16), v, preferred_element_type=F32)
             + jnp.dot((q * qd_ref[hh]).astype(BF16), s.astype(BF16), preferred_element_type=F32))
        kt = jnp.transpose(k * kd_ref[hh]).astype(BF16)
        s_new = s * cd_ref[hh] + jnp.dot(kt, v, preferred_element_type=F32)
        s_ref[hh] = s_new
        o = _group_rms(o[:c])
        gate = g_ref[0, :, hh * dv:(hh + 1) * dv]
        o_ref[0, :, hh * dv:(hh + 1) * dv] = (gate * jax.nn.sigmoid(gate) * o).astype(o_ref.dtype)

    @pl.when(ci == nc - 1)
    def _():
        so_ref[0] = s_ref[...]


def _retention(qkvg, pos, s0):
    b, t, width = qkvg.shape
    dk, dv = RET_HEAD_DIM, RET_V_DIM
    h = width // (2 * dk + 2 * dv)
    c = RET_CHUNK if t % RET_CHUNK == 0 else t
    nc = t // c
    cp = max(c, LANES)
    half = dk // 2
    lg = np.log1p(-(2.0 ** (-5.0 - np.arange(h, dtype=np.float32)))).astype(np.float32)
    idx = np.arange(c, dtype=np.float32)
    diff = idx[:, None] - idx[None, :]
    mask = np.where(diff >= 0, np.exp(np.maximum(diff, 0.0)[None] * lg[:, None, None]), 0.0).astype(np.float32)
    q_dec = np.exp((idx + 1.0)[None, :] * lg[:, None]).astype(np.float32)
    k_dec = np.exp((c - 1.0 - idx)[None, :] * lg[:, None]).astype(np.float32)
    c_dec = np.exp(c * lg).astype(np.float32)
    mask = jnp.asarray(np.pad(mask, ((0, 0), (0, cp - c), (0, cp - c))))
    q_dec = jnp.asarray(np.pad(q_dec, ((0, 0), (0, cp - c)))[..., None])
    k_dec = jnp.asarray(np.pad(k_dec, ((0, 0), (0, cp - c)))[..., None])
    c_dec = jnp.asarray(c_dec.reshape(h, 1, 1))
    inv = ROPE_BASE ** (-jnp.arange(half, dtype=F32) / half)
    ang = pos.astype(F32)[:, None] * inv[None, :]
    cos = jnp.pad(jnp.cos(ang), ((0, nc * cp - t), (0, 0)))
    sin = jnp.pad(jnp.sin(ang), ((0, nc * cp - t), (0, 0)))

    hb = _tile(h, 4, 1)
    ng = h // hb
    vb = (2 * h * dk) // (hb * dv)
    assert (2 * h * dk) % (hb * dv) == 0
    in_specs = [
        pl.BlockSpec((1, c, hb * dk), lambda bi, hi, ci: (bi, ci, hi)),
        pl.BlockSpec((1, c, hb * dk), lambda bi, hi, ci: (bi, ci, ng + hi)),
        pl.BlockSpec((1, c, hb * dv), lambda bi, hi, ci: (bi, ci, vb + hi)),
        pl.BlockSpec((1, c, hb * dv), lambda bi, hi, ci: (bi, ci, vb + ng + hi)),
        pl.BlockSpec((cp, half), lambda bi, hi, ci: (ci, 0)),
        pl.BlockSpec((cp, half), lambda bi, hi, ci: (ci, 0)),
        pl.BlockSpec((hb, cp, cp), lambda bi, hi, ci: (hi, 0, 0)),
        pl.BlockSpec((hb, cp, 1), lambda bi, hi, ci: (hi, 0, 0)),
        pl.BlockSpec((hb, cp, 1), lambda bi, hi, ci: (hi, 0, 0)),
        pl.BlockSpec((hb, 1, 1), lambda bi, hi, ci: (hi, 0, 0)),
    ]
    args = [qkvg, qkvg, qkvg, qkvg, cos, sin, mask, q_dec, k_dec, c_dec]
    if s0 is not None:
        in_specs.append(pl.BlockSpec((1, hb, dk, dv), lambda bi, hi, ci: (bi, hi, 0, 0)))
        args.append(s0)
    o, s_out = pl.pallas_call(
        functools.partial(_ret_kernel, c=c, cp=cp, nc=nc, hb=hb, has_s0=s0 is not None),
        grid=(b, ng, nc),
        in_specs=in_specs,
        out_specs=[pl.BlockSpec((1, c, hb * dv), lambda bi, hi, ci: (bi, ci, hi)),
                   pl.BlockSpec((1, hb, dk, dv), lambda bi, hi, ci: (bi, hi, 0, 0))],
        out_shape=[jax.ShapeDtypeStruct((b, t, h * dv), BF16), jax.ShapeDtypeStruct((b, h, dk, dv), F32)],
        scratch_shapes=[pltpu.VMEM((hb, dk, dv), F32)],
        compiler_params=_params(("parallel", "parallel", "arbitrary")),
        name="retention",
    )(*args)
    return o, s_out


def _router_kernel(h_ref, w_ref, b_ref, ids_ref, gates_ref):
    logits = jnp.dot(h_ref[...], w_ref[...].astype(BF16), preferred_element_type=F32) + b_ref[...]
    lane = lax.broadcasted_iota(jnp.int32, logits.shape, 1)
    big = jnp.int32(LANES)
    ng, ne = MOE_GROUPS, MOE_EXPERTS_PER_GROUP
    gl = jnp.where(lane < ng, logits, NEG_INF)
    gmax = jnp.max(gl, axis=-1, keepdims=True)
    gsum = jnp.sum(jnp.exp(gl - gmax), axis=-1, keepdims=True)
    g_sel = jnp.min(jnp.where(gl == gmax, lane, big), axis=-1, keepdims=True)
    g_w = 1.0 / gsum
    lo = ng + g_sel * ne
    in_group = (lane >= lo) & (lane < lo + ne)
    el = jnp.where(in_group, logits, NEG_INF)
    emax = jnp.max(el, axis=-1, keepdims=True)
    ee = jnp.exp(el - emax)
    ep = ee / jnp.sum(ee, axis=-1, keepdims=True)
    ep = jnp.where(in_group, ep, -1.0)
    p1 = jnp.max(ep, axis=-1, keepdims=True)
    i1 = jnp.min(jnp.where(ep == p1, lane, big), axis=-1, keepdims=True)
    ep2 = jnp.where(lane == i1, -1.0, ep)
    p2 = jnp.max(ep2, axis=-1, keepdims=True)
    i2 = jnp.min(jnp.where(ep2 == p2, lane, big), axis=-1, keepdims=True)
    psum = p1 + p2
    ids_ref[...] = jnp.where(lane == 0, i1 - ng, jnp.where(lane == 1, i2 - ng, 0))
    gates_ref[...] = jnp.where(lane == 0, g_w * p1 / psum, jnp.where(lane == 1, g_w * p2 / psum, 0.0))


def _router(hf, w_rg, b_rg, w_re, b_re):
    n, d = hf.shape
    ng, ne = MOE_GROUPS, MOE_EXPERTS_PER_GROUP
    w = jnp.concatenate([w_rg.astype(F32), jnp.transpose(w_re.astype(F32), (1, 0, 2)).reshape(d, ng * ne)], axis=1)
    w = jnp.pad(w, ((0, 0), (0, LANES - w.shape[1])))
    bias = jnp.pad(jnp.concatenate([b_rg.astype(F32), b_re.astype(F32).reshape(-1)]), (0, LANES - ng - ng * ne)).reshape(1, LANES)
    tm = _tile(n, 256, SUBLANES)
    ids, gates = pl.pallas_call(
        _router_kernel,
        grid=(n // tm,),
        in_specs=[pl.BlockSpec((tm, d), lambda i: (i, 0)), pl.BlockSpec((d, LANES), lambda i: (0, 0)),
                  pl.BlockSpec((1, LANES), lambda i: (0, 0))],
        out_specs=[pl.BlockSpec((tm, LANES), lambda i: (i, 0)), pl.BlockSpec((tm, LANES), lambda i: (i, 0))],
        out_shape=[jax.ShapeDtypeStruct((n, LANES), jnp.int32), jax.ShapeDtypeStruct((n, LANES), F32)],
        compiler_params=_params(("parallel",)),
        name="moe_router",
    )(hf, w, bias)
    return ids[:, :MOE_TOP], gates[:, :MOE_TOP]


def _row_copy(src, s_row, dst, d_row, sem):
    return pltpu.make_async_copy(src.at[pl.ds(s_row, 1)], dst.at[pl.ds(d_row, 1)], sem)


def _row_pitch(n_planes):
    return n_planes + (4 - n_planes) % SUBLANES


def _moe_up_kernel(be_ref, nu_ref, tokc_ref, tokn_ref, x_hbm, w_ref, h_ref, xbuf, sem, *, rows):
    b = pl.program_id(0)
    n_used = nu_ref[0]
    slot = lax.rem(b, 2)
    n_planes = x_hbm.shape[1]
    ROW_PITCH = _row_pitch(n_planes)

    def row_copy(tok_ref, s, r):
        dst = xbuf.at[pl.ds((s * rows + r) * ROW_PITCH, n_planes), :]
        return pltpu.make_async_copy(x_hbm.at[tok_ref[0, 0, r]], dst, sem.at[s])

    def gather(tok_ref, s):
        def body(r, carry):
            row_copy(tok_ref, s, r).start()
            return carry
        lax.fori_loop(0, rows, body, 0, unroll=8)

    @pl.when(b == 0)
    def _():
        gather(tokc_ref, 0)

    @pl.when(b + 1 < n_used)
    def _():
        gather(tokn_ref, 1 - slot)

    @pl.when(b < n_used)
    def _():
        def wait_body(r, carry):
            row_copy(tokc_ref, slot, r).wait()
            return carry
        lax.fori_loop(0, rows, wait_body, 0, unroll=8)
        f = h_ref.shape[1]
        base = slot * (rows * ROW_PITCH)
        acc = jnp.zeros((rows, 2 * f), F32)
        for j in range(0, n_planes, 2):
            x2 = jnp.concatenate([xbuf[pl.ds(base + j, rows, stride=ROW_PITCH), :],
                                  xbuf[pl.ds(base + j + 1, rows, stride=ROW_PITCH), :]], axis=1)
            acc = acc + jnp.dot(x2.astype(BF16), w_ref[0, 0, j * LANES:(j + 2) * LANES, :].astype(BF16),
                                preferred_element_type=F32)
        a, g = acc[:, :f], acc[:, f:]
        h_ref[...] = (a * jax.nn.sigmoid(a) * g).astype(h_ref.dtype)

    @pl.when(b >= n_used)
    def _():
        h_ref[...] = jnp.zeros_like(h_ref)


def _moe_down_kernel(be_ref, nu_ref, nv_ref, dstc_ref, dstp_ref, h_ref, g_ref, w_ref, y_hbm, ybuf, sem, *, nb):
    b = pl.program_id(0)
    n_used = nu_ref[0]
    slot = lax.rem(b, 2)

    def scatter(dst_ref, s, count, wait):
        def body(r, carry):
            cp = _row_copy(ybuf.at[s], r, y_hbm, dst_ref[0, 0, r], sem.at[s])
            if wait:
                cp.wait()
            else:
                cp.start()
            return carry
        lax.fori_loop(0, count, body, 0)

    @pl.when(b < n_used)
    def _():
        y = jnp.dot(h_ref[...], w_ref[0, 0].astype(BF16), preferred_element_type=F32) * g_ref[...]
        ybuf[slot] = y
        scatter(dstc_ref, slot, nv_ref[b], False)

    @pl.when((b >= 1) & (b - 1 < n_used))
    def _():
        scatter(dstp_ref, 1 - slot, nv_ref[jnp.maximum(b - 1, 0)], True)

    @pl.when((b == nb - 1) & (b < n_used))
    def _():
        scatter(dstc_ref, slot, nv_ref[b], True)


def _moe(h16, hf3, layer, w_rg, b_rg, w_re, b_re, w_up, w_down):
    n, d = h16.shape
    e, f2 = w_up.shape[1], w_up.shape[3]
    f = f2 // 2
    rows = MOE_ROWS
    ids, gates = _router(h16, w_rg, b_rg, w_re, b_re)

    a = n * MOE_TOP
    nb = -(-a // rows) + e
    e_flat = ids.reshape(-1)
    a_idx = jnp.arange(a, dtype=jnp.int32)
    tok_flat = a_idx // MOE_TOP
    _, tok_s, dst_s, gate_s = lax.sort(
        (e_flat, tok_flat, (a_idx % MOE_TOP) * n + tok_flat, lax.bitcast_convert_type(gates.reshape(-1), jnp.int32)),
        num_keys=1, is_stable=True)
    packed = jnp.stack([tok_s, dst_s, gate_s, jnp.zeros_like(tok_s)], axis=1)
    counts = jnp.bincount(e_flat, length=e).astype(jnp.int32)
    starts = jnp.cumsum(counts) - counts
    blocks_per = (counts + rows - 1) // rows
    blk_end = jnp.cumsum(blocks_per)
    first_blk = blk_end - blocks_per
    n_used = blk_end[-1].astype(jnp.int32)
    blk_ids = jnp.arange(nb, dtype=jnp.int32)
    owner = jnp.minimum(jnp.searchsorted(blk_end, blk_ids, side="right"), e - 1).astype(jnp.int32)
    in_e0 = (blk_ids - first_blk[owner]) * rows
    n_valid = jnp.where(blk_ids < n_used, jnp.clip(counts[owner] - in_e0, 0, rows), 0).astype(jnp.int32)
    within = jnp.arange(rows, dtype=jnp.int32)[None, :]
    live = within < n_valid[:, None]
    src = jnp.clip((starts[owner] + in_e0)[:, None] + within, 0, a - 1)
    picked = packed[src]
    tok3 = jnp.where(live, picked[..., 0], 0).reshape(nb, 1, rows)
    dst3 = jnp.where(live, picked[..., 1], 0).reshape(nb, 1, rows)
    gate_buf = jnp.where(live, lax.bitcast_convert_type(picked[..., 2], F32), 0.0)
    block_expert = owner[jnp.minimum(blk_ids, n_used - 1)]
    n_used_arr = n_used.reshape(1)
    smem_blk = functools.partial(pl.BlockSpec, (1, 1, rows), memory_space=pltpu.SMEM)
    h_mid = pl.pallas_call(
        functools.partial(_moe_up_kernel, rows=rows),
        grid_spec=pltpu.PrefetchScalarGridSpec(
            num_scalar_prefetch=2,
            grid=(nb,),
            in_specs=[smem_blk(lambda b, be, nu: (b, 0, 0)),
                      smem_blk(lambda b, be, nu: (jnp.minimum(b + 1, nb - 1), 0, 0)),
                      pl.BlockSpec(memory_space=pl.ANY),
                      pl.BlockSpec((1, 1, d, f2), lambda b, be, nu: (layer, be[b], 0, 0))],
            out_specs=pl.BlockSpec((rows, f), lambda b, be, nu: (b, 0)),
            scratch_shapes=[pltpu.VMEM((2 * rows * _row_pitch(d // LANES), LANES), F32),
                            pltpu.SemaphoreType.DMA((2,))]),
        out_shape=jax.ShapeDtypeStruct((nb * rows, f), BF16),
        compiler_params=_params(("arbitrary",)),
        name="moe_up",
    )(block_expert, n_used_arr, tok3, tok3, hf3, w_up)

    y_tok = pl.pallas_call(
        functools.partial(_moe_down_kernel, nb=nb),
        grid_spec=pltpu.PrefetchScalarGridSpec(
            num_scalar_prefetch=3,
            grid=(nb,),
            in_specs=[smem_blk(lambda b, be, nu, nv: (b, 0, 0)),
                      smem_blk(lambda b, be, nu, nv: (jnp.maximum(b - 1, 0), 0, 0)),
                      pl.BlockSpec((rows, f), lambda b, be, nu, nv: (b, 0)),
                      pl.BlockSpec((rows, 1), lambda b, be, nu, nv: (b, 0)),
                      pl.BlockSpec((1, 1, f, d), lambda b, be, nu, nv: (layer, be[b], 0, 0))],
            out_specs=pl.BlockSpec(memory_space=pl.ANY),
            scratch_shapes=[pltpu.VMEM((2, rows, d), F32), pltpu.SemaphoreType.DMA((2,))]),
        out_shape=jax.ShapeDtypeStruct((MOE_TOP * n, d), F32),
        compiler_params=_params(("arbitrary",)),
        name="moe_down",
    )(block_expert, n_used_arr, n_valid, dst3, dst3, h_mid, gate_buf.reshape(nb * rows, 1), w_down)
    return y_tok


def _page_specs(cache_shape, n_pages, pgs):
    return [pl.BlockSpec((1,) + tuple(cache_shape[1:]),
                         functools.partial(lambda bi, j, pt, r: (pt[bi * n_pages + j * pgs + r], 0, 0, 0, 0), r=r))
            for r in range(pgs)]


def _cmp_paged_kernel(pt_ref, *refs, pgs, g, d):
    page_refs = refs[:pgs]
    perm_ref, wk_ref, wv_ref, uk_ref, uv_ref = refs[pgs:]
    cs = CMP_STRIDE
    page = page_refs[0].shape[1]
    cpp = page // cs
    xs = [[[jnp.dot(perm_ref[...], pr[0, :, kv, gi, :].astype(BF16), preferred_element_type=F32)
            for gi in range(g)] for kv in range(2)] for pr in page_refs]
    m = g * pgs * cpp
    accs = [jnp.zeros((m, wk_ref.shape[1]), F32), jnp.zeros((m, wv_ref.shape[1]), F32)]
    for pp in range(0, cs, 2):
        for kv, w_ref in enumerate((wk_ref, wv_ref)):
            halves = []
            for p in (pp, pp + 1):
                pieces = [xs[r][kv][gi][p * cpp:(p + 1) * cpp, :] for gi in range(g) for r in range(pgs)]
                halves.append(jnp.concatenate(pieces, axis=0))
            lhs = jnp.concatenate(halves, axis=1).astype(BF16)
            accs[kv] = accs[kv] + jnp.dot(lhs, w_ref[pp * d:(pp + 2) * d, :], preferred_element_type=F32)
    uk_ref[0] = accs[0].reshape(g, pgs * cpp, wk_ref.shape[1])
    uv_ref[0] = accs[1].reshape(g, pgs * cpp, wv_ref.shape[1])


def _cmp_hidden_paged(cache, page_table, wcat_k, wcat_v):
    page = cache.shape[1]
    g, d = cache.shape[3], cache.shape[4]
    b, n_pages = page_table.shape
    pgs = _tile(n_pages, 8, 1)
    cpp = page // CMP_STRIDE
    nch = n_pages * cpp
    perm = np.zeros((page, page), np.float32)
    for c in range(cpp):
        for p in range(CMP_STRIDE):
            perm[p * cpp + c, c * CMP_STRIDE + p] = 1.0
    hid2 = wcat_k.shape[1]
    const = lambda shape: pl.BlockSpec(shape, lambda bi, j, pt: (0, 0))
    uk, uv = pl.pallas_call(
        functools.partial(_cmp_paged_kernel, pgs=pgs, g=g, d=d),
        grid_spec=pltpu.PrefetchScalarGridSpec(
            num_scalar_prefetch=1,
            grid=(b, n_pages // pgs),
            in_specs=_page_specs(cache.shape, n_pages, pgs)
            + [const((page, page)), const(wcat_k.shape), const(wcat_v.shape)],
            out_specs=[pl.BlockSpec((1, g, pgs * cpp, hid2), lambda bi, j, pt: (bi, 0, j, 0))] * 2),
        out_shape=[jax.ShapeDtypeStruct((b, g, nch, hid2), F32)] * 2,
        compiler_params=_params(("parallel", "parallel")),
        name="cmp_hidden_paged",
    )(page_table.reshape(-1).astype(jnp.int32), *([cache] * pgs), jnp.asarray(perm, BF16),
      wcat_k.astype(BF16), wcat_v.astype(BF16))
    return uk.reshape(b * g, nch, hid2), uv.reshape(b * g, nch, hid2)
def _cmp_post_kernel(u_ref, pe_ref, w2_ref, g_ref, o_ref, *, n_cmp, norm):
    u = u_ref[0]
    nch, hid2 = u.shape
    hid = hid2 // 2
    nxt = pltpu.roll(u[:, hid:], nch - 1, axis=0)
    x = (pe_ref[0:1, :] + u[:, :hid]) + nxt
    y = 0.5 * x * (1.0 + jnp.tanh(0.7978845608028654 * (x + 0.044715 * (x * x * x))))
    z = jnp.dot(y.astype(BF16), w2_ref[...].astype(BF16), preferred_element_type=F32)
    if norm:
        z = _group_rms(z) * g_ref[...]
    row = lax.broadcasted_iota(jnp.int32, z.shape, 0)
    o_ref[0] = jnp.where(row < n_cmp, z, 0.0)


def _cmp_wcat(w1):
    r = CMP_LEN // CMP_STRIDE
    assert r == 2
    w1r = w1.reshape(r, w1.shape[0] // r, w1.shape[1])
    return jnp.concatenate([w1r[0], w1r[1]], axis=1)


def _cmp_hidden_dense(rows, nch, wcat):
    b, _, g, d = rows.shape
    ch = rows[:, :nch * CMP_STRIDE].reshape(b, nch, CMP_STRIDE, g, d)
    ch = jnp.transpose(ch, (0, 3, 1, 2, 4)).reshape(b * g * nch, CMP_STRIDE * d).astype(BF16)
    return _mm(ch, wcat, name="cmp_hidden")[0].reshape(b * g, nch, wcat.shape[1])


def _cmp_finish(u, n_cmp, pe, w1, w2, g_k):
    bg, nch, hid2 = u.shape
    hid = hid2 // 2
    d = w2.shape[1]
    pe_rows = jnp.pad(pe.reshape(1, -1), ((0, SUBLANES - 1), (0, 0)))
    pe_hid = _mm(pe_rows, w1, name="cmp_pe")[0]
    gain = (jnp.ones((d,), F32) if g_k is None else g_k.astype(F32)).reshape(1, d)
    return pl.pallas_call(
        functools.partial(_cmp_post_kernel, n_cmp=n_cmp, norm=g_k is not None),
        grid=(bg,),
        in_specs=[pl.BlockSpec((1, nch, hid2), lambda i: (i, 0, 0)), pl.BlockSpec((SUBLANES, hid), lambda i: (0, 0)),
                  pl.BlockSpec((hid, d), lambda i: (0, 0)), pl.BlockSpec((1, d), lambda i: (0, 0))],
        out_specs=pl.BlockSpec((1, nch, d), lambda i: (i, 0, 0)),
        out_shape=jax.ShapeDtypeStruct((bg, nch, d), F32),
        compiler_params=_params(("parallel",)),
        name="cmp_post",
    )(u, pe_hid, w2, gain)


def _slc_map(n_cmp, n_slc, rows, cols):
    a = SLC_BLOCK // CMP_STRIDE
    bb = CMP_LEN // CMP_STRIDE
    j = np.arange(n_slc)[:, None, None]
    i = j * a + np.arange(a)[None, :, None] + np.arange(bb)[None, None, :] - bb + 1
    i, jj = np.broadcast_arrays(i, j)
    ok = (i >= 0) & (i < n_cmp)
    m = np.zeros((rows, cols), np.float32)
    np.add.at(m, (i[ok], jj[ok]), 1.0)
    return jnp.asarray(m)


def _masked_softmax_rows(s):
    m = jnp.max(s, axis=-1, keepdims=True)
    e = jnp.exp(s - jnp.where(m > NEG_INF, m, 0.0))
    den = jnp.sum(e, axis=-1, keepdims=True)
    return e / jnp.where(den > 0, den, 1.0)


def _store_gated(o, gl_ref, prev_ref, o_ref, branch, hpg, tq):
    d = NSA_HEAD_DIM
    gate = jax.nn.sigmoid(gl_ref[0])
    for hh in range(hpg):
        c = hh * N_BRANCH + branch
        val = gate[:, c:c + 1] * o[hh * tq:(hh + 1) * tq]
        if prev_ref is not None:
            val = prev_ref[0, :, hh * d:(hh + 1) * d] + val
        o_ref[0, :, hh * d:(hh + 1) * d] = val.astype(o_ref.dtype)


def _nsa_cmp_kernel(q_ref, kc_ref, vc_ref, map_ref, gl_ref, o_ref, sel_ref, *, hpg, tq, pos0, n_cmp, n_slc, n_top):
    i = pl.program_id(2)
    rws = hpg * tq
    q = q_ref[0, 0].reshape(rws, NSA_HEAD_DIM).astype(BF16)
    kc = kc_ref[0].astype(BF16)
    s = lax.dot_general(q, kc, (((1,), (1,)), ((), ())), preferred_element_type=F32) * NSA_SCALE
    ncp = s.shape[1]
    tok = lax.broadcasted_iota(jnp.int32, (rws, 1), 0) & (tq - 1)
    pos = pos0 + i * tq + tok
    cidx = lax.broadcasted_iota(jnp.int32, (1, ncp), 1)
    ok = (cidx * CMP_STRIDE + (CMP_LEN - 1) <= pos) & (cidx < n_cmp)
    p = _masked_softmax_rows(jnp.where(ok, s, NEG_INF))
    o = jnp.dot(p.astype(BF16), vc_ref[0].astype(BF16), preferred_element_type=F32)
    _store_gated(o, gl_ref, None, o_ref, 0, hpg, tq)

    psum = jnp.sum(p.astype(BF16).astype(F32).reshape(hpg, tq, ncp), axis=0)
    imp = jnp.dot(psum, map_ref[...], precision=lax.Precision.HIGHEST, preferred_element_type=F32)
    nsp = imp.shape[1]
    posq = pos0 + i * tq + lax.broadcasted_iota(jnp.int32, (tq, 1), 0)
    blk = lax.broadcasted_iota(jnp.int32, (tq, nsp), 1)
    back = (posq >> SLC_SHIFT) - blk
    real = blk < n_slc
    valid = (blk * SLC_BLOCK <= posq) & real
    forced = (blk == 0) | ((back >= 0) & (back < SLC_LOCAL))
    score = jnp.where(valid, imp + jnp.where(forced, FORCE_BONUS, 0.0), NEG_INF)
    if tq % LANES == 0 and n_slc <= LANES:
        nr = -(-n_slc // SUBLANES) * SUBLANES
        st = jnp.transpose(score)[:nr]
        blk_t = lax.broadcasted_iota(jnp.int32, (nr, tq), 0)
        rank = jnp.zeros((nr, tq), jnp.int32)
        for kb in range(n_slc):
            row = st[kb:kb + 1, :]
            ahead = (row > st) | ((row == st) & (blk_t > kb))
            rank = rank + ahead.astype(jnp.int32)
        sel_t = jnp.where((rank < n_top) & (blk_t < n_slc), 1.0, 0.0)
        sel_t = jnp.concatenate([sel_t, jnp.zeros((nsp - nr, tq), F32)], axis=0)
        sel_ref[0, 0] = jnp.transpose(sel_t)
    else:
        rank = jnp.zeros((tq, nsp), jnp.int32)
        for kb in range(n_slc):
            col = score[:, kb:kb + 1]
            ahead = (col > score) | ((col == score) & (blk > kb))
            rank = rank + ahead.astype(jnp.int32)
        sel_ref[0, 0] = jnp.where((rank < n_top) & real, 1.0, 0.0)


def _nsa_cmp(qn, kc, vc, glog, pos0, n_cmp, n_slc):
    b, g, hpg, t, d = qn.shape
    ncp = kc.shape[1]
    nsp = -(-n_slc // LANES) * LANES
    tq = _tile(t, 128, SUBLANES)
    smap = _slc_map(n_cmp, n_slc, ncp, nsp)
    n_top = min(SLC_TOP, n_slc)
    return pl.pallas_call(
        functools.partial(_nsa_cmp_kernel, hpg=hpg, tq=tq, pos0=pos0, n_cmp=n_cmp, n_slc=n_slc, n_top=n_top),
        grid=(b, g, t // tq),
        in_specs=[pl.BlockSpec((1, 1, hpg, tq, d), lambda bi, gi, i: (bi, gi, 0, i, 0)),
                  pl.BlockSpec((1, ncp, d), lambda bi, gi, i: (bi * g + gi, 0, 0)),
                  pl.BlockSpec((1, ncp, d), lambda bi, gi, i: (bi * g + gi, 0, 0)),
                  pl.BlockSpec((ncp, nsp), lambda bi, gi, i: (0, 0)),
                  pl.BlockSpec((1, tq, LANES), lambda bi, gi, i: (bi, i, gi))],
        out_specs=[pl.BlockSpec((1, tq, hpg * d), lambda bi, gi, i: (bi, i, gi)),
                   pl.BlockSpec((1, 1, tq, nsp), lambda bi, gi, i: (bi, gi, i, 0))],
        out_shape=[jax.ShapeDtypeStruct((b, t, g * hpg * d), F32), jax.ShapeDtypeStruct((b, g, t, nsp), F32)],
        compiler_params=_params(("parallel", "parallel", "parallel")),
        name="nsa_cmp",
    )(qn, kc, vc, smap, glog)


def _nsa_attn_kernel(*refs, hpg, tq, tk, n_kt, qpos0, kpos0, window, use_sel, branch):
    refs = list(refs)
    q_ref, k_ref, v_ref = refs[:3]
    refs = refs[3:]
    sel_ref = None
    if use_sel:
        sel_ref = refs[0]
        refs = refs[1:]
    gl_ref, prev_ref, o_ref, qs_ref, s_ref, p_ref, bias_ref, m_ref, a_ref, acc_ref = refs
    i = pl.program_id(2)
    d = NSA_HEAD_DIM
    rws = hpg * tq
    rb = min(tq, 64)
    per_head = tq // rb
    pvb = min(rws, 512)
    qs_ref[...] = (q_ref[0, 0].reshape(rws, d) * NSA_SCALE).astype(BF16)
    posq = qpos0 + i * tq + lax.broadcasted_iota(jnp.int32, (tq, 1), 0)
    sel = sel_ref[0, 0].astype(BF16) if use_sel else None
    m_ref[...] = jnp.full_like(m_ref, NEG_INF)
    acc_ref[...] = jnp.zeros_like(acc_ref)
    ones = jnp.ones((tk, d), BF16)

    def body(kt, carry):
        key0 = kt * tk
        off = pl.multiple_of(key0, tk)
        k = k_ref[0, pl.ds(off, tk), :].astype(BF16)
        v1 = jnp.concatenate([v_ref[0, pl.ds(off, tk), :].astype(BF16), ones], axis=1)
        s_ref[...] = lax.dot_general(qs_ref[...], k, (((1,), (1,)), ((), ())), preferred_element_type=F32)
        kpos = kpos0 + key0 + lax.broadcasted_iota(jnp.int32, (1, tk), 1)
        ok = kpos <= posq
        if window is not None:
            ok = ok & (posq - kpos < window)
        if use_sel:
            nsp = sel.shape[1]
            kblk = (key0 + lax.broadcasted_iota(jnp.int32, (nsp, tk), 1)) >> SLC_SHIFT
            expand = (kblk == lax.broadcasted_iota(jnp.int32, (nsp, tk), 0)).astype(BF16)
            ok = ok & (jnp.dot(sel, expand, preferred_element_type=F32) > 0.5)
        bias_ref[...] = jnp.where(ok, 0.0, NEG_INF)
        for blk in range(rws // rb):
            rows = slice(blk * rb, (blk + 1) * rb)
            part = blk % per_head
            s = s_ref[rows, :] + bias_ref[part * rb:(part + 1) * rb, :]
            m_old = m_ref[rows, :]
            m_new = jnp.maximum(m_old, jnp.max(s, axis=-1, keepdims=True))
            m_safe = jnp.where(m_new > NEG_INF, m_new, 0.0)
            p_ref[rows, :] = jnp.exp(s - jnp.tile(m_safe, (1, tk // LANES))).astype(BF16)
            a_ref[rows, :] = jnp.exp(m_old - m_safe)
            m_ref[rows, :] = m_new
        for r0 in range(0, rws, pvb):
            rows = slice(r0, r0 + pvb)
            pv = jnp.dot(p_ref[rows, :], v1, preferred_element_type=F32)
            acc_ref[rows, :] = jnp.tile(a_ref[rows, :], (1, 2)) * acc_ref[rows, :] + pv
        return carry

    q_lo = qpos0 + i * tq
    q_hi = q_lo + tq - 1
    hi = jnp.clip((q_hi - kpos0) // tk + 1, 0, n_kt)
    if window is None:
        lo = 0
    else:
        lo = jnp.clip((q_lo - (window - 1) - kpos0) // tk, 0, n_kt)
    lax.fori_loop(lo, hi, body, 0)
    l = acc_ref[:, d:]
    o = acc_ref[:, :d] / jnp.where(l > 0, l, 1.0)
    _store_gated(o, gl_ref, prev_ref, o_ref, branch, hpg, tq)


def _nsa_attn(qn, kv, glog, prev, *, qpos0, kpos0, branch, out_dtype, window=None, sel=None, tk=512):
    b, g, hpg, t, d = qn.shape
    tk_total = kv.shape[1]
    tq = _tile(t, 256, SUBLANES)
    tk = _tile(tk_total, tk, LANES)
    n_kt = tk_total // tk
    in_specs = [pl.BlockSpec((1, 1, hpg, tq, d), lambda bi, gi, i: (bi, gi, 0, i, 0)),
                pl.BlockSpec((1, tk_total, d), lambda bi, gi, i: (bi, 0, gi)),
                pl.BlockSpec((1, tk_total, d), lambda bi, gi, i: (bi, 0, g + gi))]
    args = [qn, kv, kv]
    if sel is not None:
        nsp = sel.shape[-1]
        in_specs.append(pl.BlockSpec((1, 1, tq, nsp), lambda bi, gi, i: (bi, gi, i, 0)))
        args.append(sel)
    in_specs += [pl.BlockSpec((1, tq, LANES), lambda bi, gi, i: (bi, i, gi)),
                 pl.BlockSpec((1, tq, hpg * d), lambda bi, gi, i: (bi, i, gi))]
    args += [glog, prev]
    rws = hpg * tq
    return pl.pallas_call(
        functools.partial(_nsa_attn_kernel, hpg=hpg, tq=tq, tk=tk, n_kt=n_kt, qpos0=qpos0, kpos0=kpos0,
                          window=window, use_sel=sel is not None, branch=branch),
        grid=(b, g, t // tq),
        in_specs=in_specs,
        out_specs=pl.BlockSpec((1, tq, hpg * d), lambda bi, gi, i: (bi, i, gi)),
        out_shape=jax.ShapeDtypeStruct((b, t, g * hpg * d), out_dtype),
        scratch_shapes=[pltpu.VMEM((rws, d), BF16), pltpu.VMEM((rws, tk), F32), pltpu.VMEM((rws, tk), BF16),
                        pltpu.VMEM((tq, tk), F32), pltpu.VMEM((rws, LANES), F32), pltpu.VMEM((rws, LANES), F32),
                        pltpu.VMEM((rws, 2 * d), F32)],
        compiler_params=_params(("parallel", "parallel", "parallel")),
        name="nsa_attn_%d" % branch,
    )(*args)


def _nsa_slc_paged_kernel(pt_ref, *refs, pgs, g, hpg, tq, qpos0, past_len, n_steps):
    page_refs = refs[:pgs]
    q_ref, sel_ref, tail_ref, gl_ref, prev_ref, o_ref, m_ref, l_ref, acc_ref = refs[pgs:]
    j = pl.program_id(1)
    d = NSA_HEAD_DIM
    rws = hpg * tq
    page = page_refs[0].shape[1]
    nsp = sel_ref.shape[-1]
    posq = qpos0 + lax.broadcasted_iota(jnp.int32, (tq, 1), 0)
    sel_all = sel_ref[0].reshape(g * tq, nsp).astype(BF16)

    @pl.when(j == 0)
    def _():
        m_ref[...] = jnp.full_like(m_ref, NEG_INF)
        l_ref[...] = jnp.zeros_like(l_ref)
        acc_ref[...] = jnp.zeros_like(acc_ref)

    def update(plane, n, key0):
        ok_pos = key0 + lax.broadcasted_iota(jnp.int32, (1, n), 1) <= posq
        kblk = (key0 + lax.broadcasted_iota(jnp.int32, (nsp, n), 1)) >> SLC_SHIFT
        expand = (kblk == lax.broadcasted_iota(jnp.int32, (nsp, n), 0)).astype(BF16)
        picked = jnp.dot(sel_all, expand, preferred_element_type=F32) > 0.5
        for gi in range(g):
            k = plane(0, gi).astype(BF16)
            v = plane(1, gi).astype(BF16)
            q = q_ref[0, gi].reshape(rws, d).astype(BF16)
            s = lax.dot_general(q, k, (((1,), (1,)), ((), ())), preferred_element_type=F32) * NSA_SCALE
            ok = ok_pos & picked[gi * tq:(gi + 1) * tq]
            s = jnp.where(ok[None], s.reshape(hpg, tq, n), NEG_INF).reshape(rws, n)
            m_old = m_ref[gi]
            m_new = jnp.maximum(m_old, jnp.max(s, axis=-1, keepdims=True))
            m_safe = jnp.where(m_new > NEG_INF, m_new, 0.0)
            p = jnp.exp(s - m_safe)
            alpha = jnp.exp(m_old - m_safe)
            l_ref[gi] = alpha * l_ref[gi] + jnp.sum(p, axis=-1, keepdims=True)
            acc_ref[gi] = alpha * acc_ref[gi] + jnp.dot(p.astype(BF16), v, preferred_element_type=F32)
            m_ref[gi] = m_new

    update(lambda kv, gi: jnp.concatenate([pr[0, :, kv, gi, :] for pr in page_refs], axis=0),
           pgs * page, j * (pgs * page))

    @pl.when(j == n_steps - 1)
    def _():
        update(lambda kv, gi: tail_ref[0, :, (kv * g + gi) * d:(kv * g + gi + 1) * d], tail_ref.shape[1], past_len)
        gate = jax.nn.sigmoid(gl_ref[0])
        for gi in range(g):
            l = l_ref[gi]
            o = acc_ref[gi] / jnp.where(l > 0, l, 1.0)
            for hh in range(hpg):
                c = gi * LANES + hh * N_BRANCH + 1
                col = (gi * hpg + hh) * d
                o_ref[0, :, col:col + d] = prev_ref[0, :, col:col + d] + gate[:, c:c + 1] * o[hh * tq:(hh + 1) * tq]


def _nsa_slc_paged(qn, cache, page_table, sel, tail, glog, prev, *, qpos0):
    b, g, hpg, t, d = qn.shape
    page = cache.shape[1]
    width = 2 * g * d
    n_pages = page_table.shape[1]
    pgs = _tile(n_pages, 8, 1)
    n_steps = n_pages // pgs
    nsp = sel.shape[-1]
    nt = tail.shape[1]
    rws = hpg * t
    return pl.pallas_call(
        functools.partial(_nsa_slc_paged_kernel, pgs=pgs, g=g, hpg=hpg, tq=t, qpos0=qpos0,
                          past_len=n_pages * page, n_steps=n_steps),
        grid_spec=pltpu.PrefetchScalarGridSpec(
            num_scalar_prefetch=1,
            grid=(b, n_steps),
            in_specs=_page_specs(cache.shape, n_pages, pgs)
            + [pl.BlockSpec((1, g, hpg, t, d), lambda bi, j, pt: (bi, 0, 0, 0, 0)),
               pl.BlockSpec((1, g, t, nsp), lambda bi, j, pt: (bi, 0, 0, 0)),
               pl.BlockSpec((1, nt, width), lambda bi, j, pt: (bi, 0, 0)),
               pl.BlockSpec((1, t, g * LANES), lambda bi, j, pt: (bi, 0, 0)),
               pl.BlockSpec((1, t, g * hpg * d), lambda bi, j, pt: (bi, 0, 0))],
            out_specs=pl.BlockSpec((1, t, g * hpg * d), lambda bi, j, pt: (bi, 0, 0)),
            scratch_shapes=[pltpu.VMEM((g, rws, 1), F32), pltpu.VMEM((g, rws, 1), F32), pltpu.VMEM((g, rws, d), F32)]),
        out_shape=jax.ShapeDtypeStruct((b, t, g * hpg * d), F32),
        compiler_params=_params(("parallel", "arbitrary")),
        name="nsa_slc_paged",
    )(page_table.reshape(-1).astype(jnp.int32), *([cache] * pgs), qn, sel, tail, glog, prev)


def _gate_weight(w_in, d_model):
    g = NSA_KV_HEADS
    hpg = d_model // NSA_HEAD_DIM // g
    wg = w_in[:, d_model:].reshape(d_model, g, hpg * N_BRANCH)
    wg = jnp.pad(wg, ((0, 0), (0, 0), (0, LANES - hpg * N_BRANCH)))
    return wg.reshape(d_model, g * LANES)


def kernel(x_prompt, x_sample, state_ret, cache_cmp_kv, cache_slc_kv, cache_win_kv, page_table, p_prompt, p_sample, g_mix, g_ffn, w_ret_in, w_ret_out, w_nsa_in, g_nsa_q, w_nsa_out, g_kv, w_kv, g_k_cmp, g_k_slc, g_k_win, pe_cmp_k, w_cmp_k1, w_cmp_k2, pe_cmp_v, w_cmp_v1, w_cmp_v2, w_rg, b_rg, w_re, b_re, w_moe_up, w_moe_down, w_ple_up, g_ple, w_ple_gate):
    depth = g_mix.shape[0]
    n_a = w_ret_in.shape[0]
    g, d = NSA_KV_HEADS, NSA_HEAD_DIM
    gd = g * d
    d_model = x_prompt.shape[-1]
    page = cache_cmp_kv.shape[1]
    past_len = page_table.shape[1] * page

    groups = [
        dict(x=x_prompt.reshape(-1, d_model), p=p_prompt, b=x_prompt.shape[0], t=x_prompt.shape[1], pos0=0, s0=None),
        dict(x=x_sample.reshape(-1, d_model), p=p_sample, b=x_sample.shape[0], t=x_sample.shape[1], pos0=past_len, s0=state_ret),
    ]
    n_rows = [gr["x"].shape[0] for gr in groups]
    n_tok = sum(n_rows)
    offs = [0, n_rows[0]]
    for gr in groups:
        gr["ret"] = []

    for i in range(depth):
        w_gate_nsa = None if i < n_a else _gate_weight(w_nsa_in[i - n_a], d_model)
        for gr in groups:
            b, t = gr["b"], gr["t"]
            h = _rms([(gr["x"], 0)], g_mix[i], [BF16])[0]
            if i < n_a:
                qkvg = _mm(h, w_ret_in[i], tm=2048, tk=1024, name="ret_in")[0]
                pos = gr["pos0"] + jnp.arange(t)
                s0 = None if gr["s0"] is None else gr["s0"][i]
                o, s_new = _retention(qkvg.reshape(b, t, -1), pos, s0)
                gr["ret"].append(s_new)
                gr["x"] = _mm_resid(o.reshape(b * t, -1), w_ret_out[i], gr["x"], "ret_out")
            else:
                j = i - n_a
                qn = _mm_q(h, w_nsa_in[j], g_nsa_q[j], b, t)
                glog = _mm(h, w_gate_nsa, name="nsa_gate")[0].reshape(b, t, g * LANES)
                ctx = gr["ctx"]
                o1, sel = _nsa_cmp(qn, ctx["k_c"], ctx["v_c"], glog, gr["pos0"], ctx["n_cmp"], ctx["n_slc"])
                if ctx["slc_tail"] is None:
                    o2 = _nsa_attn(qn, ctx["slc"], glog, o1, qpos0=gr["pos0"], kpos0=0, branch=1, out_dtype=F32, sel=sel)
                else:
                    o2 = _nsa_slc_paged(qn, cache_slc_kv, page_table, sel, ctx["slc_tail"], glog, o1, qpos0=gr["pos0"])
                o3 = _nsa_attn(qn, ctx["win"], glog, o2, qpos0=gr["pos0"], kpos0=ctx["win_pos0"], branch=2,
                               out_dtype=BF16, window=WINDOW)
                gr["x"] = _mm_resid(o3.reshape(b * t, -1), w_nsa_out[j], gr["x"], "nsa_out")

        normed = [_rms([(gr["x"], 0)], g_ffn[i], [BF16], planes=True) for gr in groups]
        h16 = jnp.concatenate([nm[0] for nm in normed], axis=0)
        hf3 = jnp.concatenate([nm[1] for nm in normed], axis=0)
        y_tok = _moe(h16, hf3, i, w_rg[i], b_rg[i], w_re[i], b_re[i], w_moe_up, w_moe_down)

        for gi, gr in enumerate(groups):
            rows = n_rows[gi]
            x_new, hp = _rms([(gr["x"], 0), (y_tok, offs[gi]), (y_tok, n_tok + offs[gi])], g_ple[i], [BF16],
                             want_sum=True, rows=rows, tm=64)
            gr["x"] = _mm_ple(hp, w_ple_gate, i, x_new, gr["p"][i].reshape(rows, -1), w_ple_up)

        if i == n_a - 1:
            for gi, gr in enumerate(groups):
                b, t = gr["b"], gr["t"]
                hk = _rms([(gr["x"], 0)], g_kv, [BF16])[0]
                kv = _mm_kv(hk, w_kv, g_k_slc, g_k_win).reshape(b, t, 2 * N_BRANCH * gd)
                cmp_new, slc_new, win_new = kv[..., :2 * gd], kv[..., 2 * gd:4 * gd], kv[..., 4 * gd:]
                gr["cmp_new"], gr["slc_new"] = cmp_new, slc_new
                wcat_k, wcat_v = _cmp_wcat(w_cmp_k1), _cmp_wcat(w_cmp_v1)
                n_keys = gr["pos0"] + t
                n_cmp = (n_keys - CMP_LEN) // CMP_STRIDE + 1
                nch = n_cmp + CMP_LEN // CMP_STRIDE - 1
                if gi == 0:
                    slc_tail = None
                    win_keys, win_pos0 = win_new, 0
                    gr["win_state"] = win_new[:, t - min(WINDOW, t):]
                    cmp_rows = cmp_new.reshape(b, t, 2, g, d)
                    u_k = _cmp_hidden_dense(cmp_rows[:, :, 0], nch, wcat_k)
                    u_v = _cmp_hidden_dense(cmp_rows[:, :, 1], nch, wcat_v)
                else:
                    slc_tail = jnp.pad(slc_new, ((0, 0), (0, LANES - t), (0, 0)))
                    w_buf = cache_win_kv.shape[1]
                    win_all = jnp.concatenate([cache_win_kv.reshape(b, w_buf, 2 * gd), win_new], axis=1)
                    n_all = w_buf + t
                    gr["win_state"] = win_all[:, n_all - min(WINDOW, past_len + t):]
                    win_keys = jnp.pad(win_all, ((0, 0), (0, -n_all % LANES), (0, 0)))
                    win_pos0 = past_len - w_buf
                    assert nch * CMP_STRIDE == past_len
                    u_k, u_v = _cmp_hidden_paged(cache_cmp_kv, page_table, wcat_k, wcat_v)
                k_c = _cmp_finish(u_k, n_cmp, pe_cmp_k, w_cmp_k1, w_cmp_k2, g_k_cmp)
                v_c = _cmp_finish(u_v, n_cmp, pe_cmp_v, w_cmp_v1, w_cmp_v2, None)
                gr["ctx"] = dict(k_c=k_c, v_c=v_c, n_cmp=n_cmp, n_slc=-(-n_keys // SLC_BLOCK), slc=slc_new,
                                 slc_tail=slc_tail, win=win_keys, win_pos0=win_pos0)

    outs = []
    for gr in groups:
        outs.append(gr["x"].reshape(gr["b"], gr["t"], d_model))
    rets = [jnp.stack(gr["ret"]) for gr in groups]
    kvs = []
    for name in ("cmp_new", "slc_new", "win_state"):
        for gr in groups:
            a = gr[name]
            kvs.append(a.reshape(a.shape[0], a.shape[1], 2, g, d))
    return (outs[0], outs[1], rets[0], rets[1], kvs[0], kvs[1], kvs[2], kvs[3], kvs[4], kvs[5])
```

```python
import functools

import numpy as np
import jax
import jax.numpy as jnp
from jax import lax
from jax.experimental import pallas as pl
from jax.experimental.pallas import tpu as pltpu

F32 = jnp.float32
BF16 = jnp.bfloat16

RET_HEAD_DIM = 256
RET_V_DIM = 2 * RET_HEAD_DIM
RET_CHUNK = 128
ROPE_BASE = 10000.0
NSA_HEAD_DIM = 128
NSA_KV_HEADS = 4
NSA_SCALE = NSA_HEAD_DIM ** -0.5
CMP_LEN = 32
CMP_STRIDE = 16
SLC_BLOCK = 64
SLC_SHIFT = 6
SLC_TOP = 16
SLC_LOCAL = 2
FORCE_BONUS = 1e4
WINDOW = 512
N_BRANCH = 3
MOE_GROUPS = 8
MOE_EXPERTS_PER_GROUP = 8
MOE_TOP = 2
EPS = 1e-6

LANES = 128
SUBLANES = 8
VMEM_LIMIT_BYTES = 56 * 1024 * 1024
MOE_ROWS = 256
NEG_INF = float("-inf")


def _tile(n, pref, align):
    best = None
    for t in range(align, min(n, pref) + 1, align):
        if n % t == 0:
            best = t
    return n if best is None else best


def _params(semantics):
    return pltpu.CompilerParams(dimension_semantics=semantics, vmem_limit_bytes=VMEM_LIMIT_BYTES)


def _rms_kernel(*refs, n_add, want_sum, planes):
    adds, g_ref, outs = refs[:n_add], refs[n_add], refs[n_add + 1:]
    x = adds[0][...]
    for r in adds[1:]:
        x = x + r[...]
    y = (x * lax.rsqrt(jnp.mean(x * x, axis=-1, keepdims=True) + EPS)) * g_ref[...]
    if want_sum:
        outs[0][...] = x
        outs = outs[1:]
    if planes:
        for j in range(y.shape[1] // LANES):
            outs[-1][:, j, :] = y[:, j * LANES:(j + 1) * LANES]
        outs = outs[:-1]
    for o in outs:
        o[...] = y.astype(o.dtype)


def _rms(addends, g, out_dtypes, want_sum=False, rows=None, tm=128, planes=False):
    d = addends[0][0].shape[1]
    rows = addends[0][0].shape[0] if rows is None else rows
    tm = _tile(rows, tm, SUBLANES)
    for _, off in addends:
        assert off % tm == 0
    in_specs = [pl.BlockSpec((tm, d), functools.partial(lambda i, o: (i + o, 0), o=off // tm)) for _, off in addends]
    in_specs.append(pl.BlockSpec((1, d), lambda i: (0, 0)))
    dts = ([F32] if want_sum else []) + list(out_dtypes)
    out_specs = [pl.BlockSpec((tm, d), lambda i: (i, 0)) for _ in dts]
    out_shape = [jax.ShapeDtypeStruct((rows, d), dt) for dt in dts]
    if planes:
        out_specs.append(pl.BlockSpec((tm, d // LANES, LANES), lambda i: (i, 0, 0)))
        out_shape.append(jax.ShapeDtypeStruct((rows, d // LANES, LANES), F32))
    outs = pl.pallas_call(
        functools.partial(_rms_kernel, n_add=len(addends), want_sum=want_sum, planes=planes),
        grid=(rows // tm,),
        in_specs=in_specs,
        out_specs=out_specs,
        out_shape=out_shape,
        compiler_params=_params(("parallel",)),
        name="rms",
    )(*[a for a, _ in addends], g.reshape(1, d).astype(F32))
    return outs


def _mm_kernel(x_ref, w_ref, *rest, n_extra, epilogue, nk, in_place):
    extras, o_ref = rest[:n_extra], rest[n_extra]
    acc_ref = o_ref if in_place else rest[n_extra + 1]
    k = pl.program_id(2)

    def part():
        w = w_ref[0] if len(w_ref.shape) == 3 else w_ref[...]
        return jnp.dot(x_ref[...].astype(BF16), w.astype(BF16), preferred_element_type=F32)

    @pl.when(k == 0)
    def _():
        acc_ref[...] = part()

    @pl.when(k > 0)
    def _():
        acc_ref[...] += part()

    if epilogue is not None or not in_place:
        @pl.when(k == nk - 1)
        def _():
            acc = acc_ref[...]
            if epilogue is None:
                o_ref[...] = acc.astype(o_ref.dtype)
            else:
                epilogue(acc, extras, o_ref)


def _mm(x, w, *, layer=None, n_out=None, out_dtype=F32, tm=1024, tn=1024, tk=512, epilogue=None, extras=(),
        out_shape=None, out_spec=None, name="mm"):
    m, kdim = x.shape
    n = w.shape[-1] if n_out is None else n_out
    if m <= 256:
        tk = 2048
    tm, tn, tk = _tile(m, tm, SUBLANES), _tile(n, tn, LANES), _tile(kdim, tk, LANES)
    nk = kdim // tk
    in_place = out_shape is None and out_dtype == F32
    if out_shape is None:
        out_shape = jax.ShapeDtypeStruct((m, n), out_dtype)
        out_spec = pl.BlockSpec((tm, tn), lambda i, j, k: (i, j))
    return pl.pallas_call(
        functools.partial(_mm_kernel, n_extra=len(extras), epilogue=epilogue, nk=nk, in_place=in_place),
        grid=(m // tm, n // tn, nk),
        in_specs=[pl.BlockSpec((tm, tk), lambda i, j, k: (i, k)),
                  pl.BlockSpec((tk, tn), lambda i, j, k: (k, j)) if layer is None
                  else pl.BlockSpec((1, tk, tn), lambda i, j, k: (layer, k, j))]
        + [s for _, s in extras],
        out_specs=out_spec,
        out_shape=out_shape,
        scratch_shapes=[] if in_place else [pltpu.VMEM((tm, tn), F32)],
        compiler_params=_params(("parallel", "parallel", "arbitrary")),
        name=name,
    )(x, w, *[a for a, _ in extras]), (tm, tn)


def _tile_spec(tm, tn):
    return pl.BlockSpec((tm, tn), lambda i, j, k: (i, j))


def _mm_resid_kernel(x_ref, w_ref, r_ref, o_ref):
    def part():
        return jnp.dot(x_ref[...].astype(BF16), w_ref[...].astype(BF16), preferred_element_type=F32)

    @pl.when(pl.program_id(2) == 0)
    def _():
        o_ref[...] = r_ref[...] + part()

    @pl.when(pl.program_id(2) > 0)
    def _():
        o_ref[...] += part()


def _mm_resid(x, w, resid, name):
    m, n = resid.shape
    kdim = x.shape[1]
    tm, tn = _tile(m, 2048, SUBLANES), _tile(n, 1024, LANES)
    tk = _tile(kdim, 2048 if m <= 256 else 1024, LANES)
    return pl.pallas_call(
        _mm_resid_kernel,
        grid=(m // tm, n // tn, kdim // tk),
        in_specs=[pl.BlockSpec((tm, tk), lambda i, j, k: (i, k)), pl.BlockSpec((tk, tn), lambda i, j, k: (k, j)),
                  _tile_spec(tm, tn)],
        out_specs=_tile_spec(tm, tn),
        out_shape=jax.ShapeDtypeStruct((m, n), F32),
        compiler_params=_params(("parallel", "parallel", "arbitrary")),
        name=name,
    )(x, w, resid)


def _mm_ple(h, w_gate, layer, x, p, w_up):
    m, n = x.shape
    pdim = p.shape[1]
    tm, tn = _tile(m, 2048, SUBLANES), _tile(n, 1024, LANES)

    def epi(acc, extras, o_ref):
        pu =jnp.dot(extras[1][...].astype(BF16), extras[2][0].astype(BF16), preferred_element_type=F32)
        o_ref[...] = extras[0][...] + pu * jax.nn.sigmoid(acc)

    return _mm(h, w_gate, layer=layer, tm=tm, tn=tn, epilogue=epi,
               extras=[(x, _tile_spec(tm, tn)),
                       (p, pl.BlockSpec((tm, pdim), lambda i, j, k: (i, 0))),
                       (w_up, pl.BlockSpec((1, pdim, tn), lambda i, j, k: (layer, 0, j)))], name="ple_gate")[0]


def _group_rms(a):
    return a * lax.rsqrt(jnp.mean(a * a, axis=-1, keepdims=True) + EPS)


def _mm_kv(h, w_kv, g_k_slc, g_k_win):
    m = h.shape[0]
    gd = NSA_KV_HEADS * NSA_HEAD_DIM
    n = 2 * N_BRANCH * gd
    ones = jnp.ones((gd,), F32)
    gain = jnp.concatenate([ones, ones, jnp.tile(g_k_slc.astype(F32), NSA_KV_HEADS), ones,
                            jnp.tile(g_k_win.astype(F32), NSA_KV_HEADS), ones]).reshape(1, n)
    zeros = jnp.zeros((gd,), F32)
    flag = jnp.concatenate([zeros, zeros, ones, zeros, ones, zeros]).reshape(1, n)
    tm = _tile(m, 2048, SUBLANES)
    tn = 2 * gd

    def epi(acc, extras, o_ref):
        parts = [_group_rms(acc[:, c * NSA_HEAD_DIM:(c + 1) * NSA_HEAD_DIM]) for c in range(tn // NSA_HEAD_DIM)]
        normed = jnp.concatenate(parts, axis=-1) * extras[0][...]
        o_ref[...] = jnp.where(extras[1][...] > 0.5, normed, acc)

    row = pl.BlockSpec((1, tn), lambda i, j, k: (0, j))
    return _mm(h, w_kv, tm=tm, tn=tn, tk=1024, epilogue=epi, extras=[(gain, row), (flag, row)], name="kv_proj")[0]


def _mm_q(h, w_in, g_q, b, t):
    m, d_model = h.shape
    g, d = NSA_KV_HEADS, NSA_HEAD_DIM
    hpg = d_model // d // g
    tn = hpg * d
    tm = _tile(t, 2048, SUBLANES) if t >= 256 else _tile(m, 1024, t)
    seqs = max(tm // t, 1)
    rows = min(tm, t)

    def epi(acc, extras, o_ref):
        for sq in range(seqs):
            for hh in range(hpg):
                a = acc[sq * rows:(sq + 1) * rows, hh * d:(hh + 1) * d]
                o_ref[sq, 0, hh] = _group_rms(a) * extras[0][...]

    gq = g_q.reshape(1, d).astype(F32)
    tiles_per_b = t // rows
    return _mm(h, w_in, n_out=g * tn, tm=tm, tn=tn, tk=1024, epilogue=epi,
               extras=[(gq, pl.BlockSpec((1, d), lambda i, j, k: (0, 0)))],
               out_shape=jax.ShapeDtypeStruct((b, g, hpg, t, d), F32),
               out_spec=pl.BlockSpec((seqs, 1, hpg, rows, d),
                                     lambda i, j, k: (i // tiles_per_b, j, 0, i % tiles_per_b, 0)),
               name="nsa_q")[0]


def _ret_kernel(*refs, c, cp, nc, hb, has_s0):
    if has_s0:
        q_ref, k_ref, v_ref, g_ref, cos_ref, sin_ref, mask_ref, qd_ref, kd_ref, cd_ref, s0_ref, o_ref, so_ref, s_ref = refs
    else:
        q_ref, k_ref, v_ref, g_ref, cos_ref, sin_ref, mask_ref, qd_ref, kd_ref, cd_ref, o_ref, so_ref, s_ref = refs
    ci = pl.program_id(2)
    dk, dv = RET_HEAD_DIM, RET_V_DIM

    @pl.when(ci == 0)
    def _():
        if has_s0:
            s_ref[...] = s0_ref[0]
        else:
            s_ref[...] = jnp.zeros_like(s_ref)

    def padded(a):
        if cp == c:
            return a
        return jnp.concatenate([a, jnp.zeros((cp - c, a.shape[1]), a.dtype)], axis=0)

    half = dk // 2
    cos, sin = cos_ref[...], sin_ref[...]

    def rot(a):
        a1, a2 = a[:, :half], a[:, half:]
        return jnp.concatenate([a1 * cos - a2 * sin, a1 * sin + a2 * cos], axis=-1)

    for hh in range(hb):
        q = rot(padded(q_ref[0, :, hh * dk:(hh + 1) * dk]))
        k = rot(padded(k_ref[0, :, hh * dk:(hh + 1) * dk])) * (dk ** -0.5)
        v = padded(v_ref[0, :, hh * dv:(hh + 1) * dv]).astype(BF16)
        s = s_ref[hh]
        att = lax.dot_general(q.astype(BF16), k.astype(BF16), (((1,), (1,)), ((), ())),
                              preferred_element_type=F32) * mask_ref[hh]
        o = (jnp.dot(att.astype(BF16), v, preferred_element_type=F32)
             + jnp.dot((q * qd_ref[hh]).astype(BF16), s.astype(BF16), preferred_element_type=F32))
        kt = jnp.transpose(k * kd_ref[hh]).astype(BF16)
        s_new = s * cd_ref[hh] + jnp.dot(kt, v, preferred_element_type=F32)
        s_ref[hh] = s_new
        o = _group_rms(o[:c])
        gate = g_ref[0, :, hh * dv:(hh + 1) * dv]
        o_ref[0, :, hh * dv:(hh + 1) * dv] = (gate * jax.nn.sigmoid(gate) * o).astype(o_ref.dtype)

    @pl.when(ci == nc - 1)
    def _():
        so_ref[0] = s_ref[...]


def _retention(qkvg, pos, s0):
    b, t, width = qkvg.shape
    dk, dv = RET_HEAD_DIM, RET_V_DIM
    h = width // (2 * dk + 2 * dv)
    c = RET_CHUNK if t % RET_CHUNK == 0 else t
    nc = t // c
    cp = max(c, LANES)
    half = dk // 2
    lg = np.log1p(-(2.0 ** (-5.0 - np.arange(h, dtype=np.float32)))).astype(np.float32)
    idx = np.arange(c, dtype=np.float32)
    diff = idx[:, None] - idx[None, :]
    mask = np.where(diff >= 0, np.exp(np.maximum(diff, 0.0)[None] * lg[:, None, None]), 0.0).astype(np.float32)
    q_dec = np.exp((idx + 1.0)[None, :] * lg[:, None]).astype(np.float32)
    k_dec = np.exp((c - 1.0 - idx)[None, :] * lg[:, None]).astype(np.float32)
    c_dec = np.exp(c * lg).astype(np.float32)
    mask = jnp.asarray(np.pad(mask, ((0, 0), (0, cp - c), (0, cp - c))))
    q_dec = jnp.asarray(np.pad(q_dec, ((0, 0), (0, cp - c)))[..., None])
    k_dec = jnp.asarray(np.pad(k_dec, ((0, 0), (0, cp - c)))[..., None])
    c_dec = jnp.asarray(c_dec.reshape(h, 1, 1))
    inv = ROPE_BASE ** (-jnp.arange(half, dtype=F32) / half)
    ang = pos.astype(F32)[:, None] * inv[None, :]
    cos = jnp.pad(jnp.cos(ang), ((0, nc * cp - t), (0, 0)))
    sin = jnp.pad(jnp.sin(ang), ((0, nc * cp - t), (0, 0)))

    hb = _tile(h, 4, 1)
    ng = h // hb
    vb = (2 * h * dk) // (hb * dv)
    assert (2 * h * dk) % (hb * dv) == 0
    in_specs = [
        pl.BlockSpec((1, c, hb * dk), lambda bi, hi, ci: (bi, ci, hi)),
        pl.BlockSpec((1, c, hb * dk), lambda bi, hi, ci: (bi, ci, ng + hi)),
        pl.BlockSpec((1, c, hb * dv), lambda bi, hi, ci: (bi, ci, vb + hi)),
        pl.BlockSpec((1, c, hb * dv), lambda bi, hi, ci: (bi, ci, vb + ng + hi)),
        pl.BlockSpec((cp, half), lambda bi, hi, ci: (ci, 0)),
        pl.BlockSpec((cp, half), lambda bi, hi, ci: (ci, 0)),
        pl.BlockSpec((hb, cp, cp), lambda bi, hi, ci: (hi, 0, 0)),
        pl.BlockSpec((hb, cp, 1), lambda bi, hi, ci: (hi, 0, 0)),
        pl.BlockSpec((hb, cp, 1), lambda bi, hi, ci: (hi, 0, 0)),
        pl.BlockSpec((hb, 1, 1), lambda bi, hi, ci: (hi, 0, 0)),
    ]
    args = [qkvg, qkvg, qkvg, qkvg, cos, sin, mask, q_dec, k_dec, c_dec]
    if s0 is not None:
        in_specs.append(pl.BlockSpec((1, hb, dk, dv), lambda bi, hi, ci: (bi, hi, 0, 0)))
        args.append(s0)
    o, s_out = pl.pallas_call(
        functools.partial(_ret_kernel, c=c, cp=cp, nc=nc, hb=hb, has_s0=s0 is not None),
        grid=(b, ng, nc),
        in_specs=in_specs,
        out_specs=[pl.BlockSpec((1, c, hb * dv), lambda bi, hi, ci: (bi, ci, hi)),
                   pl.BlockSpec((1, hb, dk, dv), lambda bi, hi, ci: (bi, hi, 0, 0))],
        out_shape=[jax.ShapeDtypeStruct((b, t, h * dv), BF16), jax.ShapeDtypeStruct((b, h, dk, dv), F32)],
        scratch_shapes=[pltpu.VMEM((hb, dk, dv), F32)],
        compiler_params=_params(("parallel", "parallel", "arbitrary")),
        name="retention",
    )(*args)
    return o, s_out


def _router_kernel(h_ref, w_ref, b_ref, ids_ref, gates_ref):
    logits = jnp.dot(h_ref[...], w_ref[...].astype(BF16), preferred_element_type=F32) + b_ref[...]
    lane = lax.broadcasted_iota(jnp.int32, logits.shape, 1)
    big = jnp.int32(LANES)
    ng, ne = MOE_GROUPS, MOE_EXPERTS_PER_GROUP
    gl = jnp.where(lane < ng, logits, NEG_INF)
    gmax = jnp.max(gl, axis=-1, keepdims=True)
    gsum = jnp.sum(jnp.exp(gl - gmax), axis=-1, keepdims=True)
    g_sel = jnp.min(jnp.where(gl == gmax, lane, big), axis=-1, keepdims=True)
    g_w = 1.0 / gsum
    lo = ng + g_sel * ne
    in_group = (lane >= lo) & (lane < lo + ne)
    el = jnp.where(in_group, logits, NEG_INF)
    emax = jnp.max(el, axis=-1, keepdims=True)
    ee = jnp.exp(el - emax)
    ep = ee / jnp.sum(ee, axis=-1, keepdims=True)
    ep = jnp.where(in_group, ep, -1.0)
    p1 = jnp.max(ep, axis=-1, keepdims=True)
    i1 = jnp.min(jnp.where(ep == p1, lane, big), axis=-1, keepdims=True)
    ep2 = jnp.where(lane == i1, -1.0, ep)
    p2 = jnp.max(ep2, axis=-1, keepdims=True)
    i2 = jnp.min(jnp.where(ep2 == p2, lane, big), axis=-1, keepdims=True)
    psum = p1 + p2
    ids_ref[...] = jnp.where(lane == 0, i1 - ng, jnp.where(lane == 1, i2 - ng, 0))
    gates_ref[...] = jnp.where(lane == 0, g_w * p1 / psum, jnp.where(lane == 1, g_w * p2 / psum, 0.0))


def _router(hf, w_rg, b_rg, w_re, b_re):
    n, d = hf.shape
    ng, ne = MOE_GROUPS, MOE_EXPERTS_PER_GROUP
    w = jnp.concatenate([w_rg.astype(F32), jnp.transpose(w_re.astype(F32), (1, 0, 2)).reshape(d, ng * ne)], axis=1)
    w = jnp.pad(w, ((0, 0), (0, LANES - w.shape[1])))
    bias = jnp.pad(jnp.concatenate([b_rg.astype(F32), b_re.astype(F32).reshape(-1)]), (0, LANES - ng - ng * ne)).reshape(1, LANES)
    tm = _tile(n, 256, SUBLANES)
    ids, gates = pl.pallas_call(
        _router_kernel,
        grid=(n // tm,),
        in_specs=[pl.BlockSpec((tm, d), lambda i: (i, 0)), pl.BlockSpec((d, LANES), lambda i: (0, 0)),
                  pl.BlockSpec((1, LANES), lambda i: (0, 0))],
        out_specs=[pl.BlockSpec((tm, LANES), lambda i: (i, 0)), pl.BlockSpec((tm, LANES), lambda i: (i, 0))],
        out_shape=[jax.ShapeDtypeStruct((n, LANES), jnp.int32), jax.ShapeDtypeStruct((n, LANES), F32)],
        compiler_params=_params(("parallel",)),
        name="moe_router",
    )(hf, w, bias)
    return ids[:, :MOE_TOP], gates[:, :MOE_TOP]


def _row_copy(src, s_row, dst, d_row, sem):
    return pltpu.make_async_copy(src.at[pl.ds(s_row, 1)], dst.at[pl.ds(d_row, 1)], sem)


def _row_pitch(n_planes):
    return n_planes + (4 - n_planes) % SUBLANES


def _moe_up_kernel(be_ref, nu_ref, tokc_ref, tokn_ref, x_hbm, w_ref, h_ref, xbuf, sem, *, rows, gather_priority):
    b = pl.program_id(0)
    n_used = nu_ref[0]
    slot = lax.rem(b, 2)
    n_planes = x_hbm.shape[1]
    ROW_PITCH = _row_pitch(n_planes)

    def row_copy(tok_ref, s, r):
        dst = xbuf.at[pl.ds((s * rows + r) * ROW_PITCH, n_planes), :]
        return pltpu.make_async_copy(x_hbm.at[tok_ref[0, 0, r]], dst, sem.at[s])

    def gather(tok_ref, s):
        def body(r, carry):
            row_copy(tok_ref, s, r).start(priority=gather_priority)
            return carry
        lax.fori_loop(0, rows, body, 0, unroll=8)

    @pl.when(b == 0)
    def _():
        gather(tokc_ref, 0)

    @pl.when(b + 1 < n_used)
    def _():
        gather(tokn_ref, 1 - slot)

    @pl.when(b < n_used)
    def _():
        def wait_body(r, carry):
            row_copy(tokc_ref, slot, r).wait()
            return carry
        lax.fori_loop(0, rows, wait_body, 0, unroll=8)
        f = h_ref.shape[1]
        base = slot * (rows * ROW_PITCH)
        acc = jnp.zeros((rows, 2 * f), F32)
        for j in range(0, n_planes, 2):
            x2 = jnp.concatenate([xbuf[pl.ds(base + j, rows, stride=ROW_PITCH), :],
                                  xbuf[pl.ds(base + j + 1, rows, stride=ROW_PITCH), :]], axis=1)
            acc = acc + jnp.dot(x2.astype(BF16), w_ref[0, 0, j * LANES:(j + 2) * LANES, :].astype(BF16),
                                preferred_element_type=F32)
        a, g = acc[:, :f], acc[:, f:]
        h_ref[...] = (a * jax.nn.sigmoid(a) * g).astype(h_ref.dtype)

    @pl.when(b >= n_used)
    def _():
        h_ref[...] = jnp.zeros_like(h_ref)


def _moe_down_kernel(be_ref, nu_ref, nv_ref, dstc_ref, dstp_ref, h_ref, g_ref, w_ref, y_hbm, ybuf, sem, *, nb):
    b = pl.program_id(0)
    n_used = nu_ref[0]
    slot = lax.rem(b, 2)

    def scatter(dst_ref, s, count, wait):
        def body(r, carry):
            cp = _row_copy(ybuf.at[s], r, y_hbm, dst_ref[0, 0, r], sem.at[s])
            if wait:
                cp.wait()
            else:
                cp.start()
            return carry
        lax.fori_loop(0, count, body, 0)

    @pl.when(b < n_used)
    def _():
        y = jnp.dot(h_ref[...], w_ref[0, 0].astype(BF16), preferred_element_type=F32) * g_ref[...]
        ybuf[slot] = y
        scatter(dstc_ref, slot, nv_ref[b], False)

    @pl.when((b >= 1) & (b - 1 < n_used))
    def _():
        scatter(dstp_ref, 1 - slot, nv_ref[jnp.maximum(b - 1, 0)], True)

    @pl.when((b == nb - 1) & (b < n_used))
    def _():
        scatter(dstc_ref, slot, nv_ref[b], True)


def _moe(h16, hf3, layer, w_rg, b_rg, w_re, b_re, w_up, w_down):
    n, d = h16.shape
    e, f2 = w_up.shape[1], w_up.shape[3]
    f = f2 // 2
    rows = MOE_ROWS
    ids, gates = _router(h16, w_rg, b_rg, w_re, b_re)

    a = n * MOE_TOP
    nb = -(-a // rows) + e
    e_flat = ids.reshape(-1)
    a_idx = jnp.arange(a, dtype=jnp.int32)
    tok_flat = a_idx // MOE_TOP
    _, tok_s, dst_s, gate_s = lax.sort(
        (e_flat, tok_flat, (a_idx % MOE_TOP) * n + tok_flat, lax.bitcast_convert_type(gates.reshape(-1), jnp.int32)),
        num_keys=1, is_stable=True)
    packed = jnp.stack([tok_s, dst_s, gate_s, jnp.zeros_like(tok_s)], axis=1)
    counts = jnp.bincount(e_flat, length=e).astype(jnp.int32)
    starts = jnp.cumsum(counts) - counts
    blocks_per = (counts + rows - 1) // rows
    blk_end = jnp.cumsum(blocks_per)
    first_blk = blk_end - blocks_per
    n_used = blk_end[-1].astype(jnp.int32)
    blk_ids = jnp.arange(nb, dtype=jnp.int32)
    owner = jnp.minimum(jnp.searchsorted(blk_end, blk_ids, side="right"), e - 1).astype(jnp.int32)
    in_e0 = (blk_ids - first_blk[owner]) * rows
    n_valid = jnp.where(blk_ids < n_used, jnp.clip(counts[owner] - in_e0, 0, rows), 0).astype(jnp.int32)
    within = jnp.arange(rows, dtype=jnp.int32)[None, :]
    live = within < n_valid[:, None]
    src = jnp.clip((starts[owner] + in_e0)[:, None] + within, 0, a - 1)
    picked = packed[src]
    tok3 = jnp.where(live, picked[..., 0], 0).reshape(nb, 1, rows)
    dst3 = jnp.where(live, picked[..., 1], 0).reshape(nb, 1, rows)
    gate_buf = jnp.where(live, lax.bitcast_convert_type(picked[..., 2], F32), 0.0)
    block_expert = owner[jnp.minimum(blk_ids, n_used - 1)]
    n_used_arr = n_used.reshape(1)
    smem_blk = functools.partial(pl.BlockSpec, (1, 1, rows), memory_space=pltpu.SMEM)
    h_mid = pl.pallas_call(
        functools.partial(_moe_up_kernel, rows=rows, gather_priority=layer % 2),
        grid_spec=pltpu.PrefetchScalarGridSpec(
            num_scalar_prefetch=2,
            grid=(nb,),
            in_specs=[smem_blk(lambda b, be, nu: (b, 0, 0)),
                      smem_blk(lambda b, be, nu: (jnp.minimum(b + 1, nb - 1), 0, 0)),
                      pl.BlockSpec(memory_space=pl.ANY),
                      pl.BlockSpec((1, 1, d, f2), lambda b, be, nu: (layer, be[b], 0, 0))],
            out_specs=pl.BlockSpec((rows, f), lambda b, be, nu: (b, 0)),
            scratch_shapes=[pltpu.VMEM((2 * rows * _row_pitch(d // LANES), LANES), F32),
                            pltpu.SemaphoreType.DMA((2,))]),
        out_shape=jax.ShapeDtypeStruct((nb * rows, f), BF16),
        compiler_params=_params(("arbitrary",)),
        name="moe_up",
    )(block_expert, n_used_arr, tok3, tok3, hf3, w_up)

    y_tok = pl.pallas_call(
        functools.partial(_moe_down_kernel, nb=nb),
        grid_spec=pltpu.PrefetchScalarGridSpec(
            num_scalar_prefetch=3,
            grid=(nb,),
            in_specs=[smem_blk(lambda b, be, nu, nv: (b, 0, 0)),
                      smem_blk(lambda b, be, nu, nv: (jnp.maximum(b - 1, 0), 0, 0)),
                      pl.BlockSpec((rows, f), lambda b, be, nu, nv: (b, 0)),
                      pl.BlockSpec((rows, 1), lambda b, be, nu, nv: (b, 0)),
                      pl.BlockSpec((1, 1, f, d), lambda b, be, nu, nv: (layer, be[b], 0, 0))],
            out_specs=pl.BlockSpec(memory_space=pl.ANY),
            scratch_shapes=[pltpu.VMEM((2, rows, d), F32), pltpu.SemaphoreType.DMA((2,))]),
        out_shape=jax.ShapeDtypeStruct((MOE_TOP * n, d), F32),
        compiler_params=_params(("arbitrary",)),
        name="moe_down",
    )(block_expert, n_used_arr, n_valid, dst3, dst3, h_mid, gate_buf.reshape(nb * rows, 1), w_down)
    return y_tok


def _page_specs(cache_shape, n_pages, pgs):
    return [pl.BlockSpec((1,) + tuple(cache_shape[1:]),
                         functools.partial(lambda bi, j, pt, r: (pt[bi * n_pages + j * pgs + r], 0, 0, 0, 0), r=r))
            for r in range(pgs)]


def _cmp_paged_kernel(pt_ref, *refs, pgs, g, d):
    page_refs = refs[:pgs]
    perm_ref, wk_ref, wv_ref, uk_ref, uv_ref = refs[pgs:]
    cs = CMP_STRIDE
    page = page_refs[0].shape[1]
    cpp = page // cs
    xs = [[[jnp.dot(perm_ref[...], pr[0, :, kv, gi, :].astype(BF16), preferred_element_type=F32)
            for gi in range(g)] for kv in range(2)] for pr in page_refs]
    m = g * pgs * cpp
    accs = [jnp.zeros((m, wk_ref.shape[1]), F32), jnp.zeros((m, wv_ref.shape[1]), F32)]
    for pp in range(0, cs, 2):
        for kv, w_ref in enumerate((wk_ref, wv_ref)):
            halves = []
            for p in (pp, pp + 1):
                pieces = [xs[r][kv][gi][p * cpp:(p + 1) * cpp, :] for gi in range(g) for r in range(pgs)]
                halves.append(jnp.concatenate(pieces, axis=0))
            lhs = jnp.concatenate(halves, axis=1).astype(BF16)
            accs[kv] = accs[kv] + jnp.dot(lhs, w_ref[pp * d:(pp + 2) * d, :], preferred_element_type=F32)
    uk_ref[0] = accs[0].reshape(g, pgs * cpp, wk_ref.shape[1])
    uv_ref[0] = accs[1].reshape(g, pgs * cpp, wv_ref.shape[1])


def _cmp_hidden_paged(cache, page_table, wcat_k, wcat_v):
    page = cache.shape[1]
    g, d = cache.shape[3], cache.shape[4]
    b, n_pages = page_table.shape
    pgs = _tile(n_pages, 8, 1)
    cpp = page // CMP_STRIDE
    nch = n_pages * cpp
    perm = np.zeros((page, page), np.float32)
    for c in range(cpp):
        for p in range(CMP_STRIDE):
            perm[p * cpp + c, c * CMP_STRIDE + p] = 1.0
    hid2 = wcat_k.shape[1]
    const = lambda shape: pl.BlockSpec(shape, lambda bi, j, pt: (0, 0))
    uk, uv = pl.pallas_call(
        functools.partial(_cmp_paged_kernel, pgs=pgs, g=g, d=d),
        grid_spec=pltpu.PrefetchScalarGridSpec(
            num_scalar_prefetch=1,
            grid=(b, n_pages // pgs),
            in_specs=_page_specs(cache.shape, n_pages, pgs)
            + [const((page, page)), const(wcat_k.shape), const(wcat_v.shape)],
            out_specs=[pl.BlockSpec((1, g, pgs * cpp, hid2), lambda bi, j, pt: (bi, 0, j, 0))] * 2),
        out_shape=[jax.ShapeDtypeStruct((b, g, nch, hid2), F32)] * 2,
        compiler_params=_params(("parallel", "parallel")),
        name="cmp_hidden_paged",
    )(page_table.reshape(-1).astype(jnp.int32), *([cache] * pgs), jnp.asarray(perm, BF16),
      wcat_k.astype(BF16), wcat_v.astype(BF16))
    return uk.reshape(b * g, nch, hid2), uv.reshape(b * g, nch, hid2)
def _cmp_post_kernel(u_ref, pe_ref, w2_ref, g_ref, o_ref, *, n_cmp, norm):
    u = u_ref[0]
    nch, hid2 = u.shape
    hid = hid2 // 2
    nxt = pltpu.roll(u[:, hid:], nch - 1, axis=0)
    x = (pe_ref[0:1, :] + u[:, :hid]) + nxt
    y = 0.5 * x * (1.0 + jnp.tanh(0.7978845608028654 * (x + 0.044715 * (x * x * x))))
    z = jnp.dot(y.astype(BF16), w2_ref[...].astype(BF16), preferred_element_type=F32)
    if norm:
        z = _group_rms(z) * g_ref[...]
    row = lax.broadcasted_iota(jnp.int32, z.shape, 0)
    o_ref[0] = jnp.where(row < n_cmp, z, 0.0)


def _cmp_wcat(w1):
    r = CMP_LEN // CMP_STRIDE
    assert r == 2
    w1r = w1.reshape(r, w1.shape[0] // r, w1.shape[1])
    return jnp.concatenate([w1r[0], w1r[1]], axis=1)


def _cmp_hidden_dense(rows, nch, wcat):
    b, _, g, d = rows.shape
    ch = rows[:, :nch * CMP_STRIDE].reshape(b, nch, CMP_STRIDE, g, d)
    ch = jnp.transpose(ch, (0, 3, 1, 2, 4)).reshape(b * g * nch, CMP_STRIDE * d).astype(BF16)
    return _mm(ch, wcat, name="cmp_hidden")[0].reshape(b * g, nch, wcat.shape[1])


def _cmp_finish(u, n_cmp, pe, w1, w2, g_k):
    bg, nch, hid2 = u.shape
    hid = hid2 // 2
    d = w2.shape[1]
    pe_rows = jnp.pad(pe.reshape(1, -1), ((0, SUBLANES - 1), (0, 0)))
    pe_hid = _mm(pe_rows, w1, name="cmp_pe")[0]
    gain = (jnp.ones((d,), F32) if g_k is None else g_k.astype(F32)).reshape(1, d)
    return pl.pallas_call(
        functools.partial(_cmp_post_kernel, n_cmp=n_cmp, norm=g_k is not None),
        grid=(bg,),
        in_specs=[pl.BlockSpec((1, nch, hid2), lambda i: (i, 0, 0)), pl.BlockSpec((SUBLANES, hid), lambda i: (0, 0)),
                  pl.BlockSpec((hid, d), lambda i: (0, 0)), pl.BlockSpec((1, d), lambda i: (0, 0))],
        out_specs=pl.BlockSpec((1, nch, d), lambda i: (i, 0, 0)),
        out_shape=jax.ShapeDtypeStruct((bg, nch, d), F32),
        compiler_params=_params(("parallel",)),
        name="cmp_post",
    )(u, pe_hid, w2, gain)


def _slc_map(n_cmp, n_slc, rows, cols):
    a = SLC_BLOCK // CMP_STRIDE
    bb = CMP_LEN // CMP_STRIDE
    j = np.arange(n_slc)[:, None, None]
    i = j * a + np.arange(a)[None, :, None] + np.arange(bb)[None, None, :] - bb + 1
    i, jj = np.broadcast_arrays(i, j)
    ok = (i >= 0) & (i < n_cmp)
    m = np.zeros((rows, cols), np.float32)
    np.add.at(m, (i[ok], jj[ok]), 1.0)
    return jnp.asarray(m)


def _masked_softmax_rows(s):
    m = jnp.max(s, axis=-1, keepdims=True)
    e = jnp.exp(s - jnp.where(m > NEG_INF, m, 0.0))
    den = jnp.sum(e, axis=-1, keepdims=True)
    return e / jnp.where(den > 0, den, 1.0)


def _store_gated(o, gl_ref, prev_ref, o_ref, branch, hpg, tq):
    d = NSA_HEAD_DIM
    gate = jax.nn.sigmoid(gl_ref[0])
    for hh in range(hpg):
        c = hh * N_BRANCH + branch
        val = gate[:, c:c + 1] * o[hh * tq:(hh + 1) * tq]
        if prev_ref is not None:
            val = prev_ref[0, :, hh * d:(hh + 1) * d] + val
        o_ref[0, :, hh * d:(hh + 1) * d] = val.astype(o_ref.dtype)


def _nsa_cmp_kernel(q_ref, kc_ref, vc_ref, map_ref, gl_ref, o_ref, sel_ref, *, hpg, tq, pos0, n_cmp, n_slc, n_top):
    i = pl.program_id(2)
    rws = hpg * tq
    q = q_ref[0, 0].reshape(rws, NSA_HEAD_DIM).astype(BF16)
    kc = kc_ref[0].astype(BF16)
    s = lax.dot_general(q, kc, (((1,), (1,)), ((), ())), preferred_element_type=F32) * NSA_SCALE
    ncp = s.shape[1]
    tok = lax.broadcasted_iota(jnp.int32, (rws, 1), 0) & (tq - 1)
    pos = pos0 + i * tq + tok
    cidx = lax.broadcasted_iota(jnp.int32, (1, ncp), 1)
    ok = (cidx * CMP_STRIDE + (CMP_LEN - 1) <= pos) & (cidx < n_cmp)
    p = _masked_softmax_rows(jnp.where(ok, s, NEG_INF))
    o = jnp.dot(p.astype(BF16), vc_ref[0].astype(BF16), preferred_element_type=F32)
    _store_gated(o, gl_ref, None, o_ref, 0, hpg, tq)

    psum = jnp.sum(p.astype(BF16).astype(F32).reshape(hpg, tq, ncp), axis=0)
    imp = jnp.dot(psum, map_ref[...], precision=lax.Precision.HIGHEST, preferred_element_type=F32)
    nsp = imp.shape[1]
    posq = pos0 + i * tq + lax.broadcasted_iota(jnp.int32, (tq, 1), 0)
    blk = lax.broadcasted_iota(jnp.int32, (tq, nsp), 1)
    back = (posq >> SLC_SHIFT) - blk
    real = blk < n_slc
    valid = (blk * SLC_BLOCK <= posq) & real
    forced = (blk == 0) | ((back >= 0) & (back < SLC_LOCAL))
    score = jnp.where(valid, imp + jnp.where(forced, FORCE_BONUS, 0.0), NEG_INF)
    if tq % LANES == 0 and n_slc <= LANES:
        nr = -(-n_slc // SUBLANES) * SUBLANES
        st = jnp.transpose(score)[:nr]
        blk_t = lax.broadcasted_iota(jnp.int32, (nr, tq), 0)
        rank = jnp.zeros((nr, tq), jnp.int32)
        for kb in range(n_slc):
            row = st[kb:kb + 1, :]
            ahead = (row > st) | ((row == st) & (blk_t > kb))
            rank = rank + ahead.astype(jnp.int32)
        sel_t = jnp.where((rank < n_top) & (blk_t < n_slc), 1.0, 0.0)
        sel_t = jnp.concatenate([sel_t, jnp.zeros((nsp - nr, tq), F32)], axis=0)
        sel_ref[0, 0] = jnp.transpose(sel_t)
    else:
        rank = jnp.zeros((tq, nsp), jnp.int32)
        for kb in range(n_slc):
            col = score[:, kb:kb + 1]
            ahead = (col > score) | ((col == score) & (blk > kb))
            rank = rank + ahead.astype(jnp.int32)
        sel_ref[0, 0] = jnp.where((rank < n_top) & real, 1.0, 0.0)


def _nsa_cmp(qn, kc, vc, glog, pos0, n_cmp, n_slc):
    b, g, hpg, t, d = qn.shape
    ncp = kc.shape[1]
    nsp = -(-n_slc // LANES) * LANES
    tq = _tile(t, 128, SUBLANES)
    smap = _slc_map(n_cmp, n_slc, ncp, nsp)
    n_top = min(SLC_TOP, n_slc)
    return pl.pallas_call(
        functools.partial(_nsa_cmp_kernel, hpg=hpg, tq=tq, pos0=pos0, n_cmp=n_cmp, n_slc=n_slc, n_top=n_top),
        grid=(b, g, t // tq),
        in_specs=[pl.BlockSpec((1, 1, hpg, tq, d), lambda bi, gi, i: (bi, gi, 0, i, 0)),
                  pl.BlockSpec((1, ncp, d), lambda bi, gi, i: (bi * g + gi, 0, 0)),
                  pl.BlockSpec((1, ncp, d), lambda bi, gi, i: (bi * g + gi, 0, 0)),
                  pl.BlockSpec((ncp, nsp), lambda bi, gi, i: (0, 0)),
                  pl.BlockSpec((1, tq, LANES), lambda bi, gi, i: (bi, i, gi))],
        out_specs=[pl.BlockSpec((1, tq, hpg * d), lambda bi, gi, i: (bi, i, gi)),
                   pl.BlockSpec((1, 1, tq, nsp), lambda bi, gi, i: (bi, gi, i, 0))],
        out_shape=[jax.ShapeDtypeStruct((b, t, g * hpg * d), F32), jax.ShapeDtypeStruct((b, g, t, nsp), F32)],
        compiler_params=_params(("parallel", "parallel", "parallel")),
        name="nsa_cmp",
    )(qn, kc, vc, smap, glog)


def _nsa_attn_kernel(*refs, hpg, tq, tk, n_kt, qpos0, kpos0, window, use_sel, branch):
    refs = list(refs)
    q_ref, k_ref, v_ref = refs[:3]
    refs = refs[3:]
    sel_ref = None
    if use_sel:
        sel_ref = refs[0]
        refs = refs[1:]
    gl_ref, prev_ref, o_ref, qs_ref, s_ref, p_ref, bias_ref, m_ref, a_ref, acc_ref = refs
    i = pl.program_id(2)
    d = NSA_HEAD_DIM
    rws = hpg * tq
    rb = min(tq, 64)
    per_head = tq // rb
    pvb = min(rws, 512)
    qs_ref[...] = (q_ref[0, 0].reshape(rws, d) * NSA_SCALE).astype(BF16)
    posq = qpos0 + i * tq + lax.broadcasted_iota(jnp.int32, (tq, 1), 0)
    sel = sel_ref[0, 0].astype(BF16) if use_sel else None
    m_ref[...] = jnp.full_like(m_ref, NEG_INF)
    acc_ref[...] = jnp.zeros_like(acc_ref)
    ones = jnp.ones((tk, d), BF16)

    def body(kt, carry):
        key0 = kt * tk
        off = pl.multiple_of(key0, tk)
        k = k_ref[0, pl.ds(off, tk), :].astype(BF16)
        v1 = jnp.concatenate([v_ref[0, pl.ds(off, tk), :].astype(BF16), ones], axis=1)
        s_ref[...] = lax.dot_general(qs_ref[...], k, (((1,), (1,)), ((), ())), preferred_element_type=F32)
        kpos = kpos0 + key0 + lax.broadcasted_iota(jnp.int32, (1, tk), 1)
        ok = kpos <= posq
        if window is not None:
            ok = ok & (posq - kpos < window)
        if use_sel:
            nsp = sel.shape[1]
            kblk = (key0 + lax.broadcasted_iota(jnp.int32, (nsp, tk), 1)) >> SLC_SHIFT
            expand = (kblk == lax.broadcasted_iota(jnp.int32, (nsp, tk), 0)).astype(BF16)
            ok = ok & (jnp.dot(sel, expand, preferred_element_type=F32) > 0.5)
        bias_ref[...] = jnp.where(ok, 0.0, NEG_INF)
        for blk in range(rws // rb):
            rows = slice(blk * rb, (blk + 1) * rb)
            part = blk % per_head
            s = s_ref[rows, :] + bias_ref[part * rb:(part + 1) * rb, :]
            m_old = m_ref[rows, :]
            m_new = jnp.maximum(m_old, jnp.max(s, axis=-1, keepdims=True))
            m_safe = jnp.where(m_new > NEG_INF, m_new, 0.0)
            p_ref[rows, :] = jnp.exp(s - jnp.tile(m_safe, (1, tk // LANES))).astype(BF16)
            a_ref[rows, :] = jnp.exp(m_old - m_safe)
            m_ref[rows, :] = m_new
        for r0 in range(0, rws, pvb):
            rows = slice(r0, r0 + pvb)
            pv = jnp.dot(p_ref[rows, :], v1, preferred_element_type=F32)
            acc_ref[rows, :] = jnp.tile(a_ref[rows, :], (1, 2)) * acc_ref[rows, :] + pv
        return carry

    q_lo = qpos0 + i * tq
    q_hi = q_lo + tq - 1
    hi = jnp.clip((q_hi - kpos0) // tk + 1, 0, n_kt)
    if window is None:
        lo = 0
    else:
        lo = jnp.clip((q_lo - (window - 1) - kpos0) // tk, 0, n_kt)
    lax.fori_loop(lo, hi, body, 0)
    l = acc_ref[:, d:]
    o = acc_ref[:, :d] / jnp.where(l > 0, l, 1.0)
    _store_gated(o, gl_ref, prev_ref, o_ref, branch, hpg, tq)


def _nsa_attn(qn, kv, glog, prev, *, qpos0, kpos0, branch, out_dtype, window=None, sel=None, tk=512):
    b, g, hpg, t, d = qn.shape
    tk_total = kv.shape[1]
    tq = _tile(t, 256, SUBLANES)
    tk = _tile(tk_total, tk, LANES)
    n_kt = tk_total // tk
    in_specs = [pl.BlockSpec((1, 1, hpg, tq, d), lambda bi, gi, i: (bi, gi, 0, i, 0)),
                pl.BlockSpec((1, tk_total, d), lambda bi, gi, i: (bi, 0, gi)),
                pl.BlockSpec((1, tk_total, d), lambda bi, gi, i: (bi, 0, g + gi))]
    args = [qn, kv, kv]
    if sel is not None:
        nsp = sel.shape[-1]
        in_specs.append(pl.BlockSpec((1, 1, tq, nsp), lambda bi, gi, i: (bi, gi, i, 0)))
        args.append(sel)
    in_specs += [pl.BlockSpec((1, tq, LANES), lambda bi, gi, i: (bi, i, gi)),
                 pl.BlockSpec((1, tq, hpg * d), lambda bi, gi, i: (bi, i, gi))]
    args += [glog, prev]
    rws = hpg * tq
    return pl.pallas_call(
        functools.partial(_nsa_attn_kernel, hpg=hpg, tq=tq, tk=tk, n_kt=n_kt, qpos0=qpos0, kpos0=kpos0,
                          window=window, use_sel=sel is not None, branch=branch),
        grid=(b, g, t // tq),
        in_specs=in_specs,
        out_specs=pl.BlockSpec((1, tq, hpg * d), lambda bi, gi, i: (bi, i, gi)),
        out_shape=jax.ShapeDtypeStruct((b, t, g * hpg * d), out_dtype),
        scratch_shapes=[pltpu.VMEM((rws, d), BF16), pltpu.VMEM((rws, tk), F32), pltpu.VMEM((rws, tk), BF16),
                        pltpu.VMEM((tq, tk), F32), pltpu.VMEM((rws, LANES), F32), pltpu.VMEM((rws, LANES), F32),
                        pltpu.VMEM((rws, 2 * d), F32)],
        compiler_params=_params(("parallel", "parallel", "parallel")),
        name="nsa_attn_%d" % branch,
    )(*args)


def _nsa_slc_paged_kernel(pt_ref, *refs, pgs, g, hpg, tq, qpos0, past_len, n_steps):
    page_refs = refs[:pgs]
    q_ref, sel_ref, tail_ref, gl_ref, prev_ref, o_ref, m_ref, l_ref, acc_ref = refs[pgs:]
    j = pl.program_id(1)
    d = NSA_HEAD_DIM
    rws = hpg * tq
    page = page_refs[0].shape[1]
    nsp = sel_ref.shape[-1]
    posq = qpos0 + lax.broadcasted_iota(jnp.int32, (tq, 1), 0)
    sel_all = sel_ref[0].reshape(g * tq, nsp).astype(BF16)

    @pl.when(j == 0)
    def _():
        m_ref[...] = jnp.full_like(m_ref, NEG_INF)
        l_ref[...] = jnp.zeros_like(l_ref)
        acc_ref[...] = jnp.zeros_like(acc_ref)

    def update(plane, n, key0):
        ok_pos = key0 + lax.broadcasted_iota(jnp.int32, (1, n), 1) <= posq
        kblk = (key0 + lax.broadcasted_iota(jnp.int32, (nsp, n), 1)) >> SLC_SHIFT
        expand = (kblk == lax.broadcasted_iota(jnp.int32, (nsp, n), 0)).astype(BF16)
        picked = jnp.dot(sel_all, expand, preferred_element_type=F32) > 0.5
        for gi in range(g):
            k = plane(0, gi).astype(BF16)
            v = plane(1, gi).astype(BF16)
            q = q_ref[0, gi].reshape(rws, d).astype(BF16)
            s = lax.dot_general(q, k, (((1,), (1,)), ((), ())), preferred_element_type=F32) * NSA_SCALE
            ok = ok_pos & picked[gi * tq:(gi + 1) * tq]
            s = jnp.where(ok[None], s.reshape(hpg, tq, n), NEG_INF).reshape(rws, n)
            m_old = m_ref[gi]
            m_new = jnp.maximum(m_old, jnp.max(s, axis=-1, keepdims=True))
            m_safe = jnp.where(m_new > NEG_INF, m_new, 0.0)
            p = jnp.exp(s - m_safe)
            alpha = jnp.exp(m_old - m_safe)
            l_ref[gi] = alpha * l_ref[gi] + jnp.sum(p, axis=-1, keepdims=True)
            acc_ref[gi] = alpha * acc_ref[gi] + jnp.dot(p.astype(BF16), v, preferred_element_type=F32)
            m_ref[gi] = m_new

    update(lambda kv, gi: jnp.concatenate([pr[0, :, kv, gi, :] for pr in page_refs], axis=0),
           pgs * page, j * (pgs * page))

    @pl.when(j == n_steps - 1)
    def _():
        update(lambda kv, gi: tail_ref[0, :, (kv * g + gi) * d:(kv * g + gi + 1) * d], tail_ref.shape[1], past_len)
        gate = jax.nn.sigmoid(gl_ref[0])
        for gi in range(g):
            l = l_ref[gi]
            o = acc_ref[gi] / jnp.where(l > 0, l, 1.0)
            for hh in range(hpg):
                c = gi * LANES + hh * N_BRANCH + 1
                col = (gi * hpg + hh) * d
                o_ref[0, :, col:col + d] = prev_ref[0, :, col:col + d] + gate[:, c:c + 1] * o[hh * tq:(hh + 1) * tq]


def _nsa_slc_paged(qn, cache, page_table, sel, tail, glog, prev, *, qpos0):
    b, g, hpg, t, d = qn.shape
    page = cache.shape[1]
    width = 2 * g * d
    n_pages = page_table.shape[1]
    pgs = _tile(n_pages, 16, 1)
    n_steps = n_pages // pgs
    nsp = sel.shape[-1]
    nt = tail.shape[1]
    rws = hpg * t
    return pl.pallas_call(
        functools.partial(_nsa_slc_paged_kernel, pgs=pgs, g=g, hpg=hpg, tq=t, qpos0=qpos0,
                          past_len=n_pages * page, n_steps=n_steps),
        grid_spec=pltpu.PrefetchScalarGridSpec(
            num_scalar_prefetch=1,
            grid=(b, n_steps),
            in_specs=_page_specs(cache.shape, n_pages, pgs)
            + [pl.BlockSpec((1, g, hpg, t, d), lambda bi, j, pt: (bi, 0, 0, 0, 0)),
               pl.BlockSpec((1, g, t, nsp), lambda bi, j, pt: (bi, 0, 0, 0)),
               pl.BlockSpec((1, nt, width), lambda bi, j, pt: (bi, 0, 0)),
               pl.BlockSpec((1, t, g * LANES), lambda bi, j, pt: (bi, 0, 0)),
               pl.BlockSpec((1, t, g * hpg * d), lambda bi, j, pt: (bi, 0, 0))],
            out_specs=pl.BlockSpec((1, t, g * hpg * d), lambda bi, j, pt: (bi, 0, 0)),
            scratch_shapes=[pltpu.VMEM((g, rws, 1), F32), pltpu.VMEM((g, rws, 1), F32), pltpu.VMEM((g, rws, d), F32)]),
        out_shape=jax.ShapeDtypeStruct((b, t, g * hpg * d), F32),
        compiler_params=_params(("parallel", "arbitrary")),
        name="nsa_slc_paged",
    )(page_table.reshape(-1).astype(jnp.int32), *([cache] * pgs), qn, sel, tail, glog, prev)


def _gate_weight(w_in, d_model):
    g = NSA_KV_HEADS
    hpg = d_model // NSA_HEAD_DIM // g
    wg = w_in[:, d_model:].reshape(d_model, g, hpg * N_BRANCH)
    wg = jnp.pad(wg, ((0, 0), (0, 0), (0, LANES - hpg * N_BRANCH)))
    return wg.reshape(d_model, g * LANES)


def kernel(x_prompt, x_sample, state_ret, cache_cmp_kv, cache_slc_kv, cache_win_kv, page_table, p_prompt, p_sample, g_mix, g_ffn, w_ret_in, w_ret_out, w_nsa_in, g_nsa_q, w_nsa_out, g_kv, w_kv, g_k_cmp, g_k_slc, g_k_win, pe_cmp_k, w_cmp_k1, w_cmp_k2, pe_cmp_v, w_cmp_v1, w_cmp_v2, w_rg, b_rg, w_re, b_re, w_moe_up, w_moe_down, w_ple_up, g_ple, w_ple_gate):
    depth = g_mix.shape[0]
    n_a = w_ret_in.shape[0]
    g, d = NSA_KV_HEADS, NSA_HEAD_DIM
    gd = g * d
    d_model = x_prompt.shape[-1]
    page = cache_cmp_kv.shape[1]
    past_len = page_table.shape[1] * page

    groups = [
        dict(x=x_prompt.reshape(-1, d_model), p=p_prompt, b=x_prompt.shape[0], t=x_prompt.shape[1], pos0=0, s0=None),
        dict(x=x_sample.reshape(-1, d_model), p=p_sample, b=x_sample.shape[0], t=x_sample.shape[1], pos0=past_len, s0=state_ret),
    ]
    n_rows = [gr["x"].shape[0] for gr in groups]
    n_tok = sum(n_rows)
    offs = [0, n_rows[0]]
    for gr in groups:
        gr["ret"] = []

    for i in range(depth):
        w_gate_nsa = None if i < n_a else _gate_weight(w_nsa_in[i - n_a], d_model)
        for gr in groups:
            b, t = gr["b"], gr["t"]
            h = _rms([(gr["x"], 0)], g_mix[i], [BF16])[0]
            if i < n_a:
                qkvg = _mm(h, w_ret_in[i], tm=2048, tk=1024, name="ret_in")[0]
                pos = gr["pos0"] + jnp.arange(t)
                s0 = None if gr["s0"] is None else gr["s0"][i]
                o, s_new = _retention(qkvg.reshape(b, t, -1), pos, s0)
                gr["ret"].append(s_new)
                gr["x"] = _mm_resid(o.reshape(b * t, -1), w_ret_out[i], gr["x"], "ret_out")
            else:
                j = i - n_a
                qn = _mm_q(h, w_nsa_in[j], g_nsa_q[j], b, t)
                glog = _mm(h, w_gate_nsa, name="nsa_gate")[0].reshape(b, t, g * LANES)
                ctx = gr["ctx"]
                o1, sel = _nsa_cmp(qn, ctx["k_c"], ctx["v_c"], glog, gr["pos0"], ctx["n_cmp"], ctx["n_slc"])
                if ctx["slc_tail"] is None:
                    o2 = _nsa_attn(qn, ctx["slc"], glog, o1, qpos0=gr["pos0"], kpos0=0, branch=1, out_dtype=F32, sel=sel)
                else:
                    o2 = _nsa_slc_paged(qn, cache_slc_kv, page_table, sel, ctx["slc_tail"], glog, o1, qpos0=gr["pos0"])
                o3 = _nsa_attn(qn, ctx["win"], glog, o2, qpos0=gr["pos0"], kpos0=ctx["win_pos0"], branch=2,
                               out_dtype=BF16, window=WINDOW)
                gr["x"] = _mm_resid(o3.reshape(b * t, -1), w_nsa_out[j], gr["x"], "nsa_out")

        normed = [_rms([(gr["x"], 0)], g_ffn[i], [BF16], planes=True) for gr in groups]
        h16 = jnp.concatenate([nm[0] for nm in normed], axis=0)
        hf3 = jnp.concatenate([nm[1] for nm in normed], axis=0)
        y_tok = _moe(h16, hf3, i, w_rg[i], b_rg[i], w_re[i], b_re[i], w_moe_up, w_moe_down)

        for gi, gr in enumerate(groups):
            rows = n_rows[gi]
            x_new, hp = _rms([(gr["x"], 0), (y_tok, offs[gi]), (y_tok, n_tok + offs[gi])], g_ple[i], [BF16],
                             want_sum=True, rows=rows, tm=64)
            gr["x"] = _mm_ple(hp, w_ple_gate, i, x_new, gr["p"][i].reshape(rows, -1), w_ple_up)

        if i == n_a - 1:
            for gi, gr in enumerate(groups):
                b, t = gr["b"], gr["t"]
                hk = _rms([(gr["x"], 0)], g_kv, [BF16])[0]
                kv = _mm_kv(hk, w_kv, g_k_slc, g_k_win).reshape(b, t, 2 * N_BRANCH * gd)
                cmp_new, slc_new, win_new = kv[..., :2 * gd], kv[..., 2 * gd:4 * gd], kv[..., 4 * gd:]
                gr["cmp_new"], gr["slc_new"] = cmp_new, slc_new
                wcat_k, wcat_v = _cmp_wcat(w_cmp_k1), _cmp_wcat(w_cmp_v1)
                n_keys = gr["pos0"] + t
                n_cmp = (n_keys - CMP_LEN) // CMP_STRIDE + 1
                nch = n_cmp + CMP_LEN // CMP_STRIDE - 1
                if gi == 0:
                    slc_tail = None
                    win_keys, win_pos0 = win_new, 0
                    gr["win_state"] = win_new[:, t - min(WINDOW, t):]
                    cmp_rows = cmp_new.reshape(b, t, 2, g, d)
                    u_k = _cmp_hidden_dense(cmp_rows[:, :, 0], nch, wcat_k)
                    u_v = _cmp_hidden_dense(cmp_rows[:, :, 1], nch, wcat_v)
                else:
                    slc_tail = jnp.pad(slc_new, ((0, 0), (0, LANES - t), (0, 0)))
                    w_buf = cache_win_kv.shape[1]
                    win_all = jnp.concatenate([cache_win_kv.reshape(b, w_buf, 2 * gd), win_new], axis=1)
                    n_all = w_buf + t
                    gr["win_state"] = win_all[:, n_all - min(WINDOW, past_len + t):]
                    win_keys = jnp.pad(win_all, ((0, 0), (0, -n_all % LANES), (0, 0)))
                    win_pos0 = past_len - w_buf
                    assert nch * CMP_STRIDE == past_len
                    u_k, u_v = _cmp_hidden_paged(cache_cmp_kv, page_table, wcat_k, wcat_v)
                k_c = _cmp_finish(u_k, n_cmp, pe_cmp_k, w_cmp_k1, w_cmp_k2, g_k_cmp)
                v_c = _cmp_finish(u_v, n_cmp, pe_cmp_v, w_cmp_v1, w_cmp_v2, None)
                gr["ctx"] = dict(k_c=k_c, v_c=v_c, n_cmp=n_cmp, n_slc=-(-n_keys // SLC_BLOCK), slc=slc_new,
                                 slc_tail=slc_tail, win=win_keys, win_pos0=win_pos0)

    outs = []
    for gr in groups:
        outs.append(gr["x"].reshape(gr["b"], gr["t"], d_model))
    rets = [jnp.stack(gr["ret"]) for gr in groups]
    kvs = []
    for name in ("cmp_new", "slc_new", "win_state"):
        for gr in groups:
            a = gr[name]
            kvs.append(a.reshape(a.shape[0], a.shape[1], 2, g, d))
    return (outs[0], outs[1], rets[0], rets[1], kvs[0], kvs[1], kvs[2], kvs[3], kvs[4], kvs[5])
```

```python
import functools

import numpy as np
import jax
import jax.numpy as jnp
from jax import lax
from jax.experimental import pallas as pl
from jax.experimental.pallas import tpu as pltpu

F32 = jnp.float32
BF16 = jnp.bfloat16

RET_HEAD_DIM = 256
RET_V_DIM = 2 * RET_HEAD_DIM
RET_CHUNK = 128
ROPE_BASE = 10000.0
NSA_HEAD_DIM = 128
NSA_KV_HEADS = 4
NSA_SCALE = NSA_HEAD_DIM ** -0.5
CMP_LEN = 32
CMP_STRIDE = 16
SLC_BLOCK = 64
SLC_SHIFT = 6
SLC_TOP = 16
SLC_LOCAL = 2
FORCE_BONUS = 1e4
WINDOW = 512
N_BRANCH = 3
MOE_GROUPS = 8
MOE_EXPERTS_PER_GROUP = 8
MOE_TOP = 2
EPS = 1e-6

LANES = 128
SUBLANES = 8
VMEM_LIMIT_BYTES = 56 * 1024 * 1024
MOE_ROWS = 256
NEG_INF = float("-inf")


def _tile(n, pref, align):
    best = None
    for t in range(align, min(n, pref) + 1, align):
        if n % t == 0:
            best = t
    return n if best is None else best


def _params(semantics):
    return pltpu.CompilerParams(dimension_semantics=semantics, vmem_limit_bytes=VMEM_LIMIT_BYTES)


def _rms_kernel(*refs, n_add, want_sum, planes):
    adds, g_ref, outs = refs[:n_add], refs[n_add], refs[n_add + 1:]
    x = adds[0][...]
    for r in adds[1:]:
        x = x + r[...]
    y = (x * lax.rsqrt(jnp.mean(x * x, axis=-1, keepdims=True) + EPS)) * g_ref[...]
    if want_sum:
        outs[0][...] = x
        outs = outs[1:]
    if planes:
        for j in range(y.shape[1] // LANES):
            outs[-1][:, j, :] = y[:, j * LANES:(j + 1) * LANES]
        outs = outs[:-1]
    for o in outs:
        o[...] = y.astype(o.dtype)


def _rms(addends, g, out_dtypes, want_sum=False, rows=None, tm=128, planes=False):
    d = addends[0][0].shape[1]
    rows = addends[0][0].shape[0] if rows is None else rows
    tm = _tile(rows, tm, SUBLANES)
    for _, off in addends:
        assert off % tm == 0
    in_specs = [pl.BlockSpec((tm, d), functools.partial(lambda i, o: (i + o, 0), o=off // tm)) for _, off in addends]
    in_specs.append(pl.BlockSpec((1, d), lambda i: (0, 0)))
    dts = ([F32] if want_sum else []) + list(out_dtypes)
    out_specs = [pl.BlockSpec((tm, d), lambda i: (i, 0)) for _ in dts]
    out_shape = [jax.ShapeDtypeStruct((rows, d), dt) for dt in dts]
    if planes:
        out_specs.append(pl.BlockSpec((tm, d // LANES, LANES), lambda i: (i, 0, 0)))
        out_shape.append(jax.ShapeDtypeStruct((rows, d // LANES, LANES), F32))
    outs = pl.pallas_call(
        functools.partial(_rms_kernel, n_add=len(addends), want_sum=want_sum, planes=planes),
        grid=(rows // tm,),
        in_specs=in_specs,
        out_specs=out_specs,
        out_shape=out_shape,
        compiler_params=_params(("parallel",)),
        name="rms",
    )(*[a for a, _ in addends], g.reshape(1, d).astype(F32))
    return outs


def _mm_kernel(x_ref, w_ref, *rest, n_extra, epilogue, nk, in_place):
    extras, o_ref = rest[:n_extra], rest[n_extra]
    acc_ref = o_ref if in_place else rest[n_extra + 1]
    k = pl.program_id(2)

    def part():
        w = w_ref[0] if len(w_ref.shape) == 3 else w_ref[...]
        return jnp.dot(x_ref[...].astype(BF16), w.astype(BF16), preferred_element_type=F32)

    @pl.when(k == 0)
    def _():
        acc_ref[...] = part()

    @pl.when(k > 0)
    def _():
        acc_ref[...] += part()

    if epilogue is not None or not in_place:
        @pl.when(k == nk - 1)
        def _():
            acc = acc_ref[...]
            if epilogue is None:
                o_ref[...] = acc.astype(o_ref.dtype)
            else:
                epilogue(acc, extras, o_ref)


def _mm(x, w, *, layer=None, n_out=None, out_dtype=F32, tm=1024, tn=1024, tk=512, epilogue=None, extras=(),
        out_shape=None, out_spec=None, name="mm"):
    m, kdim = x.shape
    n = w.shape[-1] if n_out is None else n_out
    if m <= 256:
        tk = 2048
    tm, tn, tk = _tile(m, tm, SUBLANES), _tile(n, tn, LANES), _tile(kdim, tk, LANES)
    nk = kdim // tk
    in_place = out_shape is None and out_dtype == F32
    if out_shape is None:
        out_shape = jax.ShapeDtypeStruct((m, n), out_dtype)
        out_spec = pl.BlockSpec((tm, tn), lambda i, j, k: (i, j))
    return pl.pallas_call(
        functools.partial(_mm_kernel, n_extra=len(extras), epilogue=epilogue, nk=nk, in_place=in_place),
        grid=(m // tm, n // tn, nk),
        in_specs=[pl.BlockSpec((tm, tk), lambda i, j, k: (i, k)),
                  pl.BlockSpec((tk, tn), lambda i, j, k: (k, j)) if layer is None
                  else pl.BlockSpec((1, tk, tn), lambda i, j, k: (layer, k, j))]
        + [s for _, s in extras],
        out_specs=out_spec,
        out_shape=out_shape,
        scratch_shapes=[] if in_place else [pltpu.VMEM((tm, tn), F32)],
        compiler_params=_params(("parallel", "parallel", "arbitrary")),
        name=name,
    )(x, w, *[a for a, _ in extras]), (tm, tn)


def _tile_spec(tm, tn):
    return pl.BlockSpec((tm, tn), lambda i, j, k: (i, j))


def _mm_resid_kernel(x_ref, w_ref, r_ref, o_ref):
    def part():
        return jnp.dot(x_ref[...].astype(BF16), w_ref[...].astype(BF16), preferred_element_type=F32)

    @pl.when(pl.program_id(2) == 0)
    def _():
        o_ref[...] = r_ref[...] + part()

    @pl.when(pl.program_id(2) > 0)
    def _():
        o_ref[...] += part()


def _mm_resid(x, w, resid, name):
    m, n = resid.shape
    kdim = x.shape[1]
    tm, tn = _tile(m, 2048, SUBLANES), _tile(n, 1024, LANES)
    tk = _tile(kdim, 2048 if m <= 256 else 1024, LANES)
    return pl.pallas_call(
        _mm_resid_kernel,
        grid=(m // tm, n // tn, kdim // tk),
        in_specs=[pl.BlockSpec((tm, tk), lambda i, j, k: (i, k)), pl.BlockSpec((tk, tn), lambda i, j, k: (k, j)),
                  _tile_spec(tm, tn)],
        out_specs=_tile_spec(tm, tn),
        out_shape=jax.ShapeDtypeStruct((m, n), F32),
        compiler_params=_params(("parallel", "parallel", "arbitrary")),
        name=name,
    )(x, w, resid)


def _mm_ple(h, w_gate, layer, x, p, w_up):
    m, n = x.shape
    pdim = p.shape[1]
    tm, tn = _tile(m, 2048, SUBLANES), _tile(n, 1024, LANES)

    def epi(acc, extras, o_ref):
        pu =jnp.dot(extras[1][...].astype(BF16), extras[2][0].astype(BF16), preferred_element_type=F32)
        o_ref[...] = extras[0][...] + pu * jax.nn.sigmoid(acc)

    return _mm(h, w_gate, layer=layer, tm=tm, tn=tn, epilogue=epi,
               extras=[(x, _tile_spec(tm, tn)),
                       (p, pl.BlockSpec((tm, pdim), lambda i, j, k: (i, 0))),
                       (w_up, pl.BlockSpec((1, pdim, tn), lambda i, j, k: (layer, 0, j)))], name="ple_gate")[0]


def _group_rms(a):
    return a * lax.rsqrt(jnp.mean(a * a, axis=-1, keepdims=True) + EPS)


def _mm_kv(h, w_kv, g_k_slc, g_k_win):
    m = h.shape[0]
    gd = NSA_KV_HEADS * NSA_HEAD_DIM
    n = 2 * N_BRANCH * gd
    ones = jnp.ones((gd,), F32)
    gain = jnp.concatenate([ones, ones, jnp.tile(g_k_slc.astype(F32), NSA_KV_HEADS), ones,
                            jnp.tile(g_k_win.astype(F32), NSA_KV_HEADS), ones]).reshape(1, n)
    zeros = jnp.zeros((gd,), F32)
    flag = jnp.concatenate([zeros, zeros, ones, zeros, ones, zeros]).reshape(1, n)
    tm = _tile(m, 2048, SUBLANES)
    tn = 2 * gd

    def epi(acc, extras, o_ref):
        parts = [_group_rms(acc[:, c * NSA_HEAD_DIM:(c + 1) * NSA_HEAD_DIM]) for c in range(tn // NSA_HEAD_DIM)]
        normed = jnp.concatenate(parts, axis=-1) * extras[0][...]
        o_ref[...] = jnp.where(extras[1][...] > 0.5, normed, acc)

    row = pl.BlockSpec((1, tn), lambda i, j, k: (0, j))
    return _mm(h, w_kv, tm=tm, tn=tn, tk=1024, epilogue=epi, extras=[(gain, row), (flag, row)], name="kv_proj")[0]


def _mm_q(h, w_in, g_q, b, t):
    m, d_model = h.shape
    g, d = NSA_KV_HEADS, NSA_HEAD_DIM
    hpg = d_model // d // g
    tn = hpg * d
    tm = _tile(t, 2048, SUBLANES) if t >= 256 else _tile(m, 1024, t)
    seqs = max(tm // t, 1)
    rows = min(tm, t)

    def epi(acc, extras, o_ref):
        for sq in range(seqs):
            for hh in range(hpg):
                a = acc[sq * rows:(sq + 1) * rows, hh * d:(hh + 1) * d]
                o_ref[sq, 0, hh] = _group_rms(a) * extras[0][...]

    gq = g_q.reshape(1, d).astype(F32)
    tiles_per_b = t // rows
    return _mm(h, w_in, n_out=g * tn, tm=tm, tn=tn, tk=1024, epilogue=epi,
               extras=[(gq, pl.BlockSpec((1, d), lambda i, j, k: (0, 0)))],
               out_shape=jax.ShapeDtypeStruct((b, g, hpg, t, d), F32),
               out_spec=pl.BlockSpec((seqs, 1, hpg, rows, d),
                                     lambda i, j, k: (i // tiles_per_b, j, 0, i % tiles_per_b, 0)),
               name="nsa_q")[0]


def _ret_kernel(*refs, c, cp, nc, hb, has_s0):
    if has_s0:
        q_ref, k_ref, v_ref, g_ref, cos_ref, sin_ref, mask_ref, qd_ref, kd_ref, cd_ref, s0_ref, o_ref, so_ref, s_ref = refs
    else:
        q_ref, k_ref, v_ref, g_ref, cos_ref, sin_ref, mask_ref, qd_ref, kd_ref, cd_ref, o_ref, so_ref, s_ref = refs
    ci = pl.program_id(2)
    dk, dv = RET_HEAD_DIM, RET_V_DIM

    @pl.when(ci == 0)
    def _():
        if has_s0:
            s_ref[...] = s0_ref[0]
        else:
            s_ref[...] = jnp.zeros_like(s_ref)

    def padded(a):
        if cp == c:
            return a
        return jnp.concatenate([a, jnp.zeros((cp - c, a.shape[1]), a.dtype)], axis=0)

    half = dk // 2
    cos, sin = cos_ref[...], sin_ref[...]

    def rot(a):
        a1, a2 = a[:, :half], a[:, half:]
        return jnp.concatenate([a1 * cos - a2 * sin, a1 * sin + a2 * cos], axis=-1)

    for hh in range(hb):
        q = rot(padded(q_ref[0, :, hh * dk:(hh + 1) * dk]))
        k = rot(padded(k_ref[0, :, hh * dk:(hh + 1) * dk])) * (dk ** -0.5)
        v = padded(v_ref[0, :, hh * dv:(hh + 1) * dv]).astype(BF16)
        s = s_ref[hh]
        att = lax.dot_general(q.astype(BF16), k.astype(BF16), (((1,), (1,)), ((), ())),
                              preferred_element_type=F32) * mask_ref[hh]
        o = (jnp.dot(att.astype(BF16), v, preferred_element_type=F32)
             + jnp.dot((q * qd_ref[hh]).astype(BF16), s.astype(BF16), preferred_element_type=F32))
        kt = jnp.transpose(k * kd_ref[hh]).astype(BF16)
        s_new = s * cd_ref[hh] + jnp.dot(kt, v, preferred_element_type=F32)
        s_ref[hh] = s_new
        o = _group_rms(o[:c])
        gate = g_ref[0, :, hh * dv:(hh + 1) * dv]
        o_ref[0, :, hh * dv:(hh + 1) * dv] = (gate * jax.nn.sigmoid(gate) * o).astype(o_ref.dtype)

    @pl.when(ci == nc - 1)
    def _():
        so_ref[0] = s_ref[...]


def _retention(qkvg, pos, s0):
    b, t, width = qkvg.shape
    dk, dv = RET_HEAD_DIM, RET_V_DIM
    h = width // (2 * dk + 2 * dv)
    c = RET_CHUNK if t % RET_CHUNK == 0 else t
    nc = t // c
    cp = max(c, LANES)
    half = dk // 2
    lg = np.log1p(-(2.0 ** (-5.0 - np.arange(h, dtype=np.float32)))).astype(np.float32)
    idx = np.arange(c, dtype=np.float32)
    diff = idx[:, None] - idx[None, :]
    mask = np.where(diff >= 0, np.exp(np.maximum(diff, 0.0)[None] * lg[:, None, None]), 0.0).astype(np.float32)
    q_dec = np.exp((idx + 1.0)[None, :] * lg[:, None]).astype(np.float32)
    k_dec = np.exp((c - 1.0 - idx)[None, :] * lg[:, None]).astype(np.float32)
    c_dec = np.exp(c * lg).astype(np.float32)
    mask = jnp.asarray(np.pad(mask, ((0, 0), (0, cp - c), (0, cp - c))))
    q_dec = jnp.asarray(np.pad(q_dec, ((0, 0), (0, cp - c)))[..., None])
    k_dec = jnp.asarray(np.pad(k_dec, ((0, 0), (0, cp - c)))[..., None])
    c_dec = jnp.asarray(c_dec.reshape(h, 1, 1))
    inv = ROPE_BASE ** (-jnp.arange(half, dtype=F32) / half)
    ang = pos.astype(F32)[:, None] * inv[None, :]
    cos = jnp.pad(jnp.cos(ang), ((0, nc * cp - t), (0, 0)))
    sin = jnp.pad(jnp.sin(ang), ((0, nc * cp - t), (0, 0)))

    hb = _tile(h, 4, 1)
    ng = h // hb
    vb = (2 * h * dk) // (hb * dv)
    assert (2 * h * dk) % (hb * dv) == 0
    in_specs = [
        pl.BlockSpec((1, c, hb * dk), lambda bi, hi, ci: (bi, ci, hi)),
        pl.BlockSpec((1, c, hb * dk), lambda bi, hi, ci: (bi, ci, ng + hi)),
        pl.BlockSpec((1, c, hb * dv), lambda bi, hi, ci: (bi, ci, vb + hi)),
        pl.BlockSpec((1, c, hb * dv), lambda bi, hi, ci: (bi, ci, vb + ng + hi)),
        pl.BlockSpec((cp, half), lambda bi, hi, ci: (ci, 0)),
        pl.BlockSpec((cp, half), lambda bi, hi, ci: (ci, 0)),
        pl.BlockSpec((hb, cp, cp), lambda bi, hi, ci: (hi, 0, 0)),
        pl.BlockSpec((hb, cp, 1), lambda bi, hi, ci: (hi, 0, 0)),
        pl.BlockSpec((hb, cp, 1), lambda bi, hi, ci: (hi, 0, 0)),
        pl.BlockSpec((hb, 1, 1), lambda bi, hi, ci: (hi, 0, 0)),
    ]
    args = [qkvg, qkvg, qkvg, qkvg, cos, sin, mask, q_dec, k_dec, c_dec]
    if s0 is not None:
        in_specs.append(pl.BlockSpec((1, hb, dk, dv), lambda bi, hi, ci: (bi, hi, 0, 0)))
        args.append(s0)
    o, s_out = pl.pallas_call(
        functools.partial(_ret_kernel, c=c, cp=cp, nc=nc, hb=hb, has_s0=s0 is not None),
        grid=(b, ng, nc),
        in_specs=in_specs,
        out_specs=[pl.BlockSpec((1, c, hb * dv), lambda bi, hi, ci: (bi, ci, hi)),
                   pl.BlockSpec((1, hb, dk, dv), lambda bi, hi, ci: (bi, hi, 0, 0))],
        out_shape=[jax.ShapeDtypeStruct((b, t, h * dv), BF16), jax.ShapeDtypeStruct((b, h, dk, dv), F32)],
        scratch_shapes=[pltpu.VMEM((hb, dk, dv), F32)],
        compiler_params=_params(("parallel", "parallel", "arbitrary")),
        name="retention",
    )(*args)
    return o, s_out


def _router_kernel(h_ref, w_ref, b_ref, ids_ref, gates_ref):
    logits = jnp.dot(h_ref[...], w_ref[...].astype(BF16), preferred_element_type=F32) + b_ref[...]
    lane = lax.broadcasted_iota(jnp.int32, logits.shape, 1)
    big = jnp.int32(LANES)
    ng, ne = MOE_GROUPS, MOE_EXPERTS_PER_GROUP
    gl = jnp.where(lane < ng, logits, NEG_INF)
    gmax = jnp.max(gl, axis=-1, keepdims=True)
    gsum = jnp.sum(jnp.exp(gl - gmax), axis=-1, keepdims=True)
    g_sel = jnp.min(jnp.where(gl == gmax, lane, big), axis=-1, keepdims=True)
    g_w = 1.0 / gsum
    lo = ng + g_sel * ne
    in_group = (lane >= lo) & (lane < lo + ne)
    el = jnp.where(in_group, logits, NEG_INF)
    emax = jnp.max(el, axis=-1, keepdims=True)
    ee = jnp.exp(el - emax)
    ep = ee / jnp.sum(ee, axis=-1, keepdims=True)
    ep = jnp.where(in_group, ep, -1.0)
    p1 = jnp.max(ep, axis=-1, keepdims=True)
    i1 = jnp.min(jnp.where(ep == p1, lane, big), axis=-1, keepdims=True)
    ep2 = jnp.where(lane == i1, -1.0, ep)
    p2 = jnp.max(ep2, axis=-1, keepdims=True)
    i2 = jnp.min(jnp.where(ep2 == p2, lane, big), axis=-1, keepdims=True)
    psum = p1 + p2
    ids_ref[...] = jnp.where(lane == 0, i1 - ng, jnp.where(lane == 1, i2 - ng, 0))
    gates_ref[...] = jnp.where(lane == 0, g_w * p1 / psum, jnp.where(lane == 1, g_w * p2 / psum, 0.0))


def _router(hf, w_rg, b_rg, w_re, b_re):
    n, d = hf.shape
    ng, ne = MOE_GROUPS, MOE_EXPERTS_PER_GROUP
    w = jnp.concatenate([w_rg.astype(F32), jnp.transpose(w_re.astype(F32), (1, 0, 2)).reshape(d, ng * ne)], axis=1)
    w = jnp.pad(w, ((0, 0), (0, LANES - w.shape[1])))
    bias = jnp.pad(jnp.concatenate([b_rg.astype(F32), b_re.astype(F32).reshape(-1)]), (0, LANES - ng - ng * ne)).reshape(1, LANES)
    tm = _tile(n, 256, SUBLANES)
    ids, gates = pl.pallas_call(
        _router_kernel,
        grid=(n // tm,),
        in_specs=[pl.BlockSpec((tm, d), lambda i: (i, 0)), pl.BlockSpec((d, LANES), lambda i: (0, 0)),
                  pl.BlockSpec((1, LANES), lambda i: (0, 0))],
        out_specs=[pl.BlockSpec((tm, LANES), lambda i: (i, 0)), pl.BlockSpec((tm, LANES), lambda i: (i, 0))],
        out_shape=[jax.ShapeDtypeStruct((n, LANES), jnp.int32), jax.ShapeDtypeStruct((n, LANES), F32)],
        compiler_params=_params(("parallel",)),
        name="moe_router",
    )(hf, w, bias)
    return ids[:, :MOE_TOP], gates[:, :MOE_TOP]


def _row_copy(src, s_row, dst, d_row, sem):
    return pltpu.make_async_copy(src.at[pl.ds(s_row, 1)], dst.at[pl.ds(d_row, 1)], sem)


def _row_pitch(n_planes):
    return n_planes + (4 - n_planes) % SUBLANES


def _moe_up_kernel(be_ref, nu_ref, tokc_ref, tokn_ref, x_hbm, w_ref, h_ref, xbuf, sem, *, rows):
    b = pl.program_id(0)
    n_used = nu_ref[0]
    slot = lax.rem(b, 2)
    n_planes = x_hbm.shape[1]
    ROW_PITCH = _row_pitch(n_planes)

    def row_copy(tok_ref, s, r):
        dst = xbuf.at[pl.ds((s * rows + r) * ROW_PITCH, n_planes), :]
        return pltpu.make_async_copy(x_hbm.at[tok_ref[0, 0, r]], dst, sem.at[s])

    def gather(tok_ref, s):
        def body(r, carry):
            row_copy(tok_ref, s, r).start()
            return carry
        lax.fori_loop(0, rows, body, 0, unroll=8)

    @pl.when(b == 0)
    def _():
        gather(tokc_ref, 0)

    @pl.when(b + 1 < n_used)
    def _():
        gather(tokn_ref, 1 - slot)

    @pl.when(b < n_used)
    def _():
        def wait_body(r, carry):
            row_copy(tokc_ref, slot, r).wait()
            return carry
        lax.fori_loop(0, rows, wait_body, 0, unroll=8)
        f = h_ref.shape[1]
        base = slot * (rows * ROW_PITCH)
        acc = jnp.zeros((rows, 2 * f), F32)
        for j in range(0, n_planes, 2):
            x2 = jnp.concatenate([xbuf[pl.ds(base + j, rows, stride=ROW_PITCH), :],
                                  xbuf[pl.ds(base + j + 1, rows, stride=ROW_PITCH), :]], axis=1)
            acc = acc + jnp.dot(x2.astype(BF16), w_ref[0, 0, j * LANES:(j + 2) * LANES, :].astype(BF16),
                                preferred_element_type=F32)
        a, g = acc[:, :f], acc[:, f:]
        h_ref[...] = (a * jax.nn.sigmoid(a) * g).astype(h_ref.dtype)

    @pl.when(b >= n_used)
    def _():
        h_ref[...] = jnp.zeros_like(h_ref)


def _moe_down_kernel(be_ref, nu_ref, nv_ref, dstc_ref, dstp_ref, h_ref, g_ref, w_ref, y_hbm, ybuf, sem, *, nb):
    b = pl.program_id(0)
    n_used = nu_ref[0]
    slot = lax.rem(b, 2)

    def scatter(dst_ref, s, count, wait):
        def one(r):
            cp = _row_copy(ybuf.at[s], r, y_hbm, dst_ref[0, 0, r], sem.at[s])
            if wait:
                cp.wait()
            else:
                cp.start()

        def group(i, carry):
            for u in range(SUBLANES):
                one(i * SUBLANES + u)
            return carry

        def single(r, carry):
            one(r)
            return carry

        full = count // SUBLANES
        lax.fori_loop(0, full, group, 0)
        lax.fori_loop(full * SUBLANES, count, single, 0)

    @pl.when(b < n_used)
    def _():
        y = jnp.dot(h_ref[...], w_ref[0, 0].astype(BF16), preferred_element_type=F32) * g_ref[...]
        ybuf[slot] = y
        scatter(dstc_ref, slot, nv_ref[b], False)

    @pl.when((b >= 1) & (b - 1 < n_used))
    def _():
        scatter(dstp_ref, 1 - slot, nv_ref[jnp.maximum(b - 1, 0)], True)

    @pl.when((b == nb - 1) & (b < n_used))
    def _():
        scatter(dstc_ref, slot, nv_ref[b], True)


def _moe(h16, hf3, layer, w_rg, b_rg, w_re, b_re, w_up, w_down):
    n, d = h16.shape
    e, f2 = w_up.shape[1], w_up.shape[3]
    f = f2 // 2
    rows = MOE_ROWS
    ids, gates = _router(h16, w_rg, b_rg, w_re, b_re)

    a = n * MOE_TOP
    nb = -(-a // rows) + e
    e_flat = ids.reshape(-1)
    a_idx = jnp.arange(a, dtype=jnp.int32)
    tok_flat = a_idx // MOE_TOP
    _, tok_s, dst_s, gate_s = lax.sort(
        (e_flat, tok_flat, (a_idx % MOE_TOP) * n + tok_flat, lax.bitcast_convert_type(gates.reshape(-1), jnp.int32)),
        num_keys=1, is_stable=True)
    packed = jnp.stack([tok_s, dst_s, gate_s, jnp.zeros_like(tok_s)], axis=1)
    counts = jnp.bincount(e_flat, length=e).astype(jnp.int32)
    starts = jnp.cumsum(counts) - counts
    blocks_per = (counts + rows - 1) // rows
    blk_end = jnp.cumsum(blocks_per)
    first_blk = blk_end - blocks_per
    n_used = blk_end[-1].astype(jnp.int32)
    blk_ids = jnp.arange(nb, dtype=jnp.int32)
    owner = jnp.minimum(jnp.searchsorted(blk_end, blk_ids, side="right"), e - 1).astype(jnp.int32)
    in_e0 = (blk_ids - first_blk[owner]) * rows
    n_valid = jnp.where(blk_ids < n_used, jnp.clip(counts[owner] - in_e0, 0, rows), 0).astype(jnp.int32)
    within = jnp.arange(rows, dtype=jnp.int32)[None, :]
    live = within < n_valid[:, None]
    src = jnp.clip((starts[owner] + in_e0)[:, None] + within, 0, a - 1)
    picked = packed[src]
    tok3 = jnp.where(live, picked[..., 0], 0).reshape(nb, 1, rows)
    dst3 = jnp.where(live, picked[..., 1], 0).reshape(nb, 1, rows)
    gate_buf = jnp.where(live, lax.bitcast_convert_type(picked[..., 2], F32), 0.0)
    block_expert = owner[jnp.minimum(blk_ids, n_used - 1)]
    n_used_arr = n_used.reshape(1)
    smem_blk = functools.partial(pl.BlockSpec, (1, 1, rows), memory_space=pltpu.SMEM)
    h_mid = pl.pallas_call(
        functools.partial(_moe_up_kernel, rows=rows),
        grid_spec=pltpu.PrefetchScalarGridSpec(
            num_scalar_prefetch=2,
            grid=(nb,),
            in_specs=[smem_blk(lambda b, be, nu: (b, 0, 0)),
                      smem_blk(lambda b, be, nu: (jnp.minimum(b + 1, nb - 1), 0, 0)),
                      pl.BlockSpec(memory_space=pl.ANY),
                      pl.BlockSpec((1, 1, d, f2), lambda b, be, nu: (layer, be[b], 0, 0))],
            out_specs=pl.BlockSpec((rows, f), lambda b, be, nu: (b, 0)),
            scratch_shapes=[pltpu.VMEM((2 * rows * _row_pitch(d // LANES), LANES), F32),
                            pltpu.SemaphoreType.DMA((2,))]),
        out_shape=jax.ShapeDtypeStruct((nb * rows, f), BF16),
        compiler_params=_params(("arbitrary",)),
        name="moe_up",
    )(block_expert, n_used_arr, tok3, tok3, hf3, w_up)

    y_tok = pl.pallas_call(
        functools.partial(_moe_down_kernel, nb=nb),
        grid_spec=pltpu.PrefetchScalarGridSpec(
            num_scalar_prefetch=3,
            grid=(nb,),
            in_specs=[smem_blk(lambda b, be, nu, nv: (b, 0, 0)),
                      smem_blk(lambda b, be, nu, nv: (jnp.maximum(b - 1, 0), 0, 0)),
                      pl.BlockSpec((rows, f), lambda b, be, nu, nv: (b, 0)),
                      pl.BlockSpec((rows, 1), lambda b, be, nu, nv: (b, 0)),
                      pl.BlockSpec((1, 1, f, d), lambda b, be, nu, nv: (layer, be[b], 0, 0))],
            out_specs=pl.BlockSpec(memory_space=pl.ANY),
            scratch_shapes=[pltpu.VMEM((2, rows, d), F32), pltpu.SemaphoreType.DMA((2,))]),
        out_shape=jax.ShapeDtypeStruct((MOE_TOP * n, d), F32),
        compiler_params=_params(("arbitrary",)),
        name="moe_down",
    )(block_expert, n_used_arr, n_valid, dst3, dst3, h_mid, gate_buf.reshape(nb * rows, 1), w_down)
    return y_tok


def _page_specs(cache_shape, n_pages, pgs):
    return [pl.BlockSpec((1,) + tuple(cache_shape[1:]),
                         functools.partial(lambda bi, j, pt, r: (pt[bi * n_pages + j * pgs + r], 0, 0, 0, 0), r=r))
            for r in range(pgs)]


def _cmp_paged_kernel(pt_ref, *refs, pgs, g, d):
    page_refs = refs[:pgs]
    perm_ref, wk_ref, wv_ref, uk_ref, uv_ref = refs[pgs:]
    cs = CMP_STRIDE
    page = page_refs[0].shape[1]
    cpp = page // cs
    xs = [[[jnp.dot(perm_ref[...], pr[0, :, kv, gi, :].astype(BF16), preferred_element_type=F32)
            for gi in range(g)] for kv in range(2)] for pr in page_refs]
    m = g * pgs * cpp
    accs = [jnp.zeros((m, wk_ref.shape[1]), F32), jnp.zeros((m, wv_ref.shape[1]), F32)]
    for pp in range(0, cs, 2):
        for kv, w_ref in enumerate((wk_ref, wv_ref)):
            halves = []
            for p in (pp, pp + 1):
                pieces = [xs[r][kv][gi][p * cpp:(p + 1) * cpp, :] for gi in range(g) for r in range(pgs)]
                halves.append(jnp.concatenate(pieces, axis=0))
            lhs = jnp.concatenate(halves, axis=1).astype(BF16)
            accs[kv] = accs[kv] + jnp.dot(lhs, w_ref[pp * d:(pp + 2) * d, :], preferred_element_type=F32)
    uk_ref[0] = accs[0].reshape(g, pgs * cpp, wk_ref.shape[1])
    uv_ref[0] = accs[1].reshape(g, pgs * cpp, wv_ref.shape[1])


def _cmp_hidden_paged(cache, page_table, wcat_k, wcat_v):
    page = cache.shape[1]
    g, d = cache.shape[3], cache.shape[4]
    b, n_pages = page_table.shape
    pgs = _tile(n_pages, 8, 1)
    cpp = page // CMP_STRIDE
    nch = n_pages * cpp
    perm = np.zeros((page, page), np.float32)
    for c in range(cpp):
        for p in range(CMP_STRIDE):
            perm[p * cpp + c, c * CMP_STRIDE + p] = 1.0
    hid2 = wcat_k.shape[1]
    const = lambda shape: pl.BlockSpec(shape, lambda bi, j, pt: (0, 0))
    uk, uv = pl.pallas_call(
        functools.partial(_cmp_paged_kernel, pgs=pgs, g=g, d=d),
        grid_spec=pltpu.PrefetchScalarGridSpec(
            num_scalar_prefetch=1,
            grid=(b, n_pages // pgs),
            in_specs=_page_specs(cache.shape, n_pages, pgs)
            + [const((page, page)), const(wcat_k.shape), const(wcat_v.shape)],
            out_specs=[pl.BlockSpec((1, g, pgs * cpp, hid2), lambda bi, j, pt: (bi, 0, j, 0))] * 2),
        out_shape=[jax.ShapeDtypeStruct((b, g, nch, hid2), F32)] * 2,
        compiler_params=_params(("parallel", "parallel")),
        name="cmp_hidden_paged",
    )(page_table.reshape(-1).astype(jnp.int32), *([cache] * pgs), jnp.asarray(perm, BF16),
      wcat_k.astype(BF16), wcat_v.astype(BF16))
    return uk.reshape(b * g, nch, hid2), uv.reshape(b * g, nch, hid2)
def _cmp_post_kernel(u_ref, pe_ref, w2_ref, g_ref, o_ref, *, n_cmp, norm):
    u = u_ref[0]
    nch, hid2 = u.shape
    hid = hid2 // 2
    nxt = pltpu.roll(u[:, hid:], nch - 1, axis=0)
    x = (pe_ref[0:1, :] + u[:, :hid]) + nxt
    y = 0.5 * x * (1.0 + jnp.tanh(0.7978845608028654 * (x + 0.044715 * (x * x * x))))
    z = jnp.dot(y.astype(BF16), w2_ref[...].astype(BF16), preferred_element_type=F32)
    if norm:
        z = _group_rms(z) * g_ref[...]
    row = lax.broadcasted_iota(jnp.int32, z.shape, 0)
    o_ref[0] = jnp.where(row < n_cmp, z, 0.0)


def _cmp_wcat(w1):
    r = CMP_LEN // CMP_STRIDE
    assert r == 2
    w1r = w1.reshape(r, w1.shape[0] // r, w1.shape[1])
    return jnp.concatenate([w1r[0], w1r[1]], axis=1)


def _cmp_hidden_dense(rows, nch, wcat):
    b, _, g, d = rows.shape
    ch = rows[:, :nch * CMP_STRIDE].reshape(b, nch, CMP_STRIDE, g, d)
    ch = jnp.transpose(ch, (0, 3, 1, 2, 4)).reshape(b * g * nch, CMP_STRIDE * d).astype(BF16)
    return _mm(ch, wcat, name="cmp_hidden")[0].reshape(b * g, nch, wcat.shape[1])


def _cmp_finish(u, n_cmp, pe, w1, w2, g_k):
    bg, nch, hid2 = u.shape
    hid = hid2 // 2
    d = w2.shape[1]
    pe_rows = jnp.pad(pe.reshape(1, -1), ((0, SUBLANES - 1), (0, 0)))
    pe_hid = _mm(pe_rows, w1, name="cmp_pe")[0]
    gain = (jnp.ones((d,), F32) if g_k is None else g_k.astype(F32)).reshape(1, d)
    return pl.pallas_call(
        functools.partial(_cmp_post_kernel, n_cmp=n_cmp, norm=g_k is not None),
        grid=(bg,),
        in_specs=[pl.BlockSpec((1, nch, hid2), lambda i: (i, 0, 0)), pl.BlockSpec((SUBLANES, hid), lambda i: (0, 0)),
                  pl.BlockSpec((hid, d), lambda i: (0, 0)), pl.BlockSpec((1, d), lambda i: (0, 0))],
        out_specs=pl.BlockSpec((1, nch, d), lambda i: (i, 0, 0)),
        out_shape=jax.ShapeDtypeStruct((bg, nch, d), F32),
        compiler_params=_params(("parallel",)),
        name="cmp_post",
    )(u, pe_hid, w2, gain)


def _slc_map(n_cmp, n_slc, rows, cols):
    a = SLC_BLOCK // CMP_STRIDE
    bb = CMP_LEN // CMP_STRIDE
    j = np.arange(n_slc)[:, None, None]
    i = j * a + np.arange(a)[None, :, None] + np.arange(bb)[None, None, :] - bb + 1
    i, jj = np.broadcast_arrays(i, j)
    ok = (i >= 0) & (i < n_cmp)
    m = np.zeros((rows, cols), np.float32)
    np.add.at(m, (i[ok], jj[ok]), 1.0)
    return jnp.asarray(m)


def _masked_softmax_rows(s):
    m = jnp.max(s, axis=-1, keepdims=True)
    e = jnp.exp(s - jnp.where(m > NEG_INF, m, 0.0))
    den = jnp.sum(e, axis=-1, keepdims=True)
    return e / jnp.where(den > 0, den, 1.0)


def _store_gated(o, gl_ref, prev_ref, o_ref, branch, hpg, tq):
    d = NSA_HEAD_DIM
    gate = jax.nn.sigmoid(gl_ref[0])
    for hh in range(hpg):
        c = hh * N_BRANCH + branch
        val = gate[:, c:c + 1] * o[hh * tq:(hh + 1) * tq]
        if prev_ref is not None:
            val = prev_ref[0, :, hh * d:(hh + 1) * d] + val
        o_ref[0, :, hh * d:(hh + 1) * d] = val.astype(o_ref.dtype)


def _nsa_cmp_kernel(q_ref, kc_ref, vc_ref, map_ref, gl_ref, o_ref, sel_ref, *, hpg, tq, pos0, n_cmp, n_slc, n_top):
    i = pl.program_id(2)
    rws = hpg * tq
    q = q_ref[0, 0].reshape(rws, NSA_HEAD_DIM).astype(BF16)
    kc = kc_ref[0].astype(BF16)
    s = lax.dot_general(q, kc, (((1,), (1,)), ((), ())), preferred_element_type=F32) * NSA_SCALE
    ncp = s.shape[1]
    tok = lax.broadcasted_iota(jnp.int32, (rws, 1), 0) & (tq - 1)
    pos = pos0 + i * tq + tok
    cidx = lax.broadcasted_iota(jnp.int32, (1, ncp), 1)
    ok = (cidx * CMP_STRIDE + (CMP_LEN - 1) <= pos) & (cidx < n_cmp)
    p = _masked_softmax_rows(jnp.where(ok, s, NEG_INF))
    o = jnp.dot(p.astype(BF16), vc_ref[0].astype(BF16), preferred_element_type=F32)
    _store_gated(o, gl_ref, None, o_ref, 0, hpg, tq)

    psum = jnp.sum(p.astype(BF16).astype(F32).reshape(hpg, tq, ncp), axis=0)
    imp = jnp.dot(psum, map_ref[...], precision=lax.Precision.HIGHEST, preferred_element_type=F32)
    nsp = imp.shape[1]
    posq = pos0 + i * tq + lax.broadcasted_iota(jnp.int32, (tq, 1), 0)
    blk = lax.broadcasted_iota(jnp.int32, (tq, nsp), 1)
    back = (posq >> SLC_SHIFT) - blk
    real = blk < n_slc
    valid = (blk * SLC_BLOCK <= posq) & real
    forced = (blk == 0) | ((back >= 0) & (back < SLC_LOCAL))
    score = jnp.where(valid, imp + jnp.where(forced, FORCE_BONUS, 0.0), NEG_INF)
    if tq % LANES == 0 and n_slc <= LANES:
        nr = -(-n_slc // SUBLANES) * SUBLANES
        st = jnp.transpose(score)[:nr]
        blk_t = lax.broadcasted_iota(jnp.int32, (nr, tq), 0)
        rank = jnp.zeros((nr, tq), jnp.int32)
        for kb in range(n_slc):
            row = st[kb:kb + 1, :]
            ahead = (row > st) | ((row == st) & (blk_t > kb))
            rank = rank + ahead.astype(jnp.int32)
        sel_t = jnp.where((rank < n_top) & (blk_t < n_slc), 1.0, 0.0)
        sel_t = jnp.concatenate([sel_t, jnp.zeros((nsp - nr, tq), F32)], axis=0)
        sel_ref[0, 0] = jnp.transpose(sel_t)
    else:
        rank = jnp.zeros((tq, nsp), jnp.int32)
        for kb in range(n_slc):
            col = score[:, kb:kb + 1]
            ahead = (col > score) | ((col == score) & (blk > kb))
            rank = rank + ahead.astype(jnp.int32)
        sel_ref[0, 0] = jnp.where((rank < n_top) & real, 1.0, 0.0)


def _nsa_cmp(qn, kc, vc, glog, pos0, n_cmp, n_slc):
    b, g, hpg, t, d = qn.shape
    ncp = kc.shape[1]
    nsp = -(-n_slc // LANES) * LANES
    tq = _tile(t, 128, SUBLANES)
    smap = _slc_map(n_cmp, n_slc, ncp, nsp)
    n_top = min(SLC_TOP, n_slc)
    return pl.pallas_call(
        functools.partial(_nsa_cmp_kernel, hpg=hpg, tq=tq, pos0=pos0, n_cmp=n_cmp, n_slc=n_slc, n_top=n_top),
        grid=(b, g, t // tq),
        in_specs=[pl.BlockSpec((1, 1, hpg, tq, d), lambda bi, gi, i: (bi, gi, 0, i, 0)),
                  pl.BlockSpec((1, ncp, d), lambda bi, gi, i: (bi * g + gi, 0, 0)),
                  pl.BlockSpec((1, ncp, d), lambda bi, gi, i: (bi * g + gi, 0, 0)),
                  pl.BlockSpec((ncp, nsp), lambda bi, gi, i: (0, 0)),
                  pl.BlockSpec((1, tq, LANES), lambda bi, gi, i: (bi, i, gi))],
        out_specs=[pl.BlockSpec((1, tq, hpg * d), lambda bi, gi, i: (bi, i, gi)),
                   pl.BlockSpec((1, 1, tq, nsp), lambda bi, gi, i: (bi, gi, i, 0))],
        out_shape=[jax.ShapeDtypeStruct((b, t, g * hpg * d), F32), jax.ShapeDtypeStruct((b, g, t, nsp), F32)],
        compiler_params=_params(("parallel", "parallel", "parallel")),
        name="nsa_cmp",
    )(qn, kc, vc, smap, glog)


def _nsa_attn_kernel(*refs, hpg, tq, tk, n_kt, qpos0, kpos0, window, use_sel, branch):
    refs = list(refs)
    q_ref, k_ref, v_ref = refs[:3]
    refs = refs[3:]
    sel_ref = None
    if use_sel:
        sel_ref = refs[0]
        refs = refs[1:]
    gl_ref, prev_ref, o_ref, qs_ref, s_ref, p_ref, bias_ref, m_ref, a_ref, acc_ref = refs
    i = pl.program_id(2)
    d = NSA_HEAD_DIM
    rws = hpg * tq
    rb = min(tq, 64)
    per_head = tq // rb
    pvb = min(rws, 512)
    qs_ref[...] = (q_ref[0, 0].reshape(rws, d) * NSA_SCALE).astype(BF16)
    posq = qpos0 + i * tq + lax.broadcasted_iota(jnp.int32, (tq, 1), 0)
    sel = sel_ref[0, 0].astype(BF16) if use_sel else None
    m_ref[...] = jnp.full_like(m_ref, NEG_INF)
    acc_ref[...] = jnp.zeros_like(acc_ref)
    ones = jnp.ones((tk, d), BF16)

    def body(kt, carry):
        key0 = kt * tk
        off = pl.multiple_of(key0, tk)
        k = k_ref[0, pl.ds(off, tk), :].astype(BF16)
        v1 = jnp.concatenate([v_ref[0, pl.ds(off, tk), :].astype(BF16), ones], axis=1)
        s_ref[...] = lax.dot_general(qs_ref[...], k, (((1,), (1,)), ((), ())), preferred_element_type=F32)
        kpos = kpos0 + key0 + lax.broadcasted_iota(jnp.int32, (1, tk), 1)
        ok = kpos <= posq
        if window is not None:
            ok = ok & (posq - kpos < window)
        if use_sel:
            nsp = sel.shape[1]
            kblk = (key0 + lax.broadcasted_iota(jnp.int32, (nsp, tk), 1)) >> SLC_SHIFT
            expand = (kblk == lax.broadcasted_iota(jnp.int32, (nsp, tk), 0)).astype(BF16)
            ok = ok & (jnp.dot(sel, expand, preferred_element_type=F32) > 0.5)
        bias_ref[...] = jnp.where(ok, 0.0, NEG_INF)
        for blk in range(rws // rb):
            rows = slice(blk * rb, (blk + 1) * rb)
            part = blk % per_head
            s = s_ref[rows, :] + bias_ref[part * rb:(part + 1) * rb, :]
            m_old = m_ref[rows, :]
            m_new = jnp.maximum(m_old, jnp.max(s, axis=-1, keepdims=True))
            m_safe = jnp.where(m_new > NEG_INF, m_new, 0.0)
            p_ref[rows, :] = jnp.exp(s - jnp.tile(m_safe, (1, tk // LANES))).astype(BF16)
            a_ref[rows, :] = jnp.exp(m_old - m_safe)
            m_ref[rows, :] = m_new
        for r0 in range(0, rws, pvb):
            rows = slice(r0, r0 + pvb)
            pv = jnp.dot(p_ref[rows, :], v1, preferred_element_type=F32)
            acc_ref[rows, :] = jnp.tile(a_ref[rows, :], (1, 2)) * acc_ref[rows, :] + pv
        return carry

    q_lo = qpos0 + i * tq
    q_hi = q_lo + tq - 1
    hi = jnp.clip((q_hi - kpos0) // tk + 1, 0, n_kt)
    if window is None:
        lo = 0
    else:
        lo = jnp.clip((q_lo - (window - 1) - kpos0) // tk, 0, n_kt)
    lax.fori_loop(lo, hi, body, 0)
    l = acc_ref[:, d:]
    o = acc_ref[:, :d] / jnp.where(l > 0, l, 1.0)
    _store_gated(o, gl_ref, prev_ref, o_ref, branch, hpg, tq)


def _nsa_attn(qn, kv, glog, prev, *, qpos0, kpos0, branch, out_dtype, window=None, sel=None, tk=512):
    b, g, hpg, t, d = qn.shape
    tk_total = kv.shape[1]
    tq = _tile(t, 256, SUBLANES)
    tk = _tile(tk_total, tk, LANES)
    n_kt = tk_total // tk
    in_specs = [pl.BlockSpec((1, 1, hpg, tq, d), lambda bi, gi, i: (bi, gi, 0, i, 0)),
                pl.BlockSpec((1, tk_total, d), lambda bi, gi, i: (bi, 0, gi)),
                pl.BlockSpec((1, tk_total, d), lambda bi, gi, i: (bi, 0, g + gi))]
    args = [qn, kv, kv]
    if sel is not None:
        nsp = sel.shape[-1]
        in_specs.append(pl.BlockSpec((1, 1, tq, nsp), lambda bi, gi, i: (bi, gi, i, 0)))
        args.append(sel)
    in_specs += [pl.BlockSpec((1, tq, LANES), lambda bi, gi, i: (bi, i, gi)),
                 pl.BlockSpec((1, tq, hpg * d), lambda bi, gi, i: (bi, i, gi))]
    args += [glog, prev]
    rws = hpg * tq
    return pl.pallas_call(
        functools.partial(_nsa_attn_kernel, hpg=hpg, tq=tq, tk=tk, n_kt=n_kt, qpos0=qpos0, kpos0=kpos0,
                          window=window, use_sel=sel is not None, branch=branch),
        grid=(b, g, t // tq),
        in_specs=in_specs,
        out_specs=pl.BlockSpec((1, tq, hpg * d), lambda bi, gi, i: (bi, i, gi)),
        out_shape=jax.ShapeDtypeStruct((b, t, g * hpg * d), out_dtype),
        scratch_shapes=[pltpu.VMEM((rws, d), BF16), pltpu.VMEM((rws, tk), F32), pltpu.VMEM((rws, tk), BF16),
                        pltpu.VMEM((tq, tk), F32), pltpu.VMEM((rws, LANES), F32), pltpu.VMEM((rws, LANES), F32),
                        pltpu.VMEM((rws, 2 * d), F32)],
        compiler_params=_params(("parallel", "parallel", "parallel")),
        name="nsa_attn_%d" % branch,
    )(*args)


def _nsa_slc_paged_kernel(pt_ref, *refs, pgs, g, hpg, tq, qpos0, past_len, n_steps):
    page_refs = refs[:pgs]
    q_ref, sel_ref, tail_ref, gl_ref, prev_ref, o_ref, m_ref, l_ref, acc_ref = refs[pgs:]
    j = pl.program_id(1)
    d = NSA_HEAD_DIM
    rws = hpg * tq
    page = page_refs[0].shape[1]
    nsp = sel_ref.shape[-1]
    posq = qpos0 + lax.broadcasted_iota(jnp.int32, (tq, 1), 0)
    sel_all = sel_ref[0].reshape(g * tq, nsp).astype(BF16)

    @pl.when(j == 0)
    def _():
        m_ref[...] = jnp.full_like(m_ref, NEG_INF)
        l_ref[...] = jnp.zeros_like(l_ref)
        acc_ref[...] = jnp.zeros_like(acc_ref)

    def update(plane, n, key0):
        ok_pos = key0 + lax.broadcasted_iota(jnp.int32, (1, n), 1) <= posq
        kblk = (key0 + lax.broadcasted_iota(jnp.int32, (nsp, n), 1)) >> SLC_SHIFT
        expand = (kblk == lax.broadcasted_iota(jnp.int32, (nsp, n), 0)).astype(BF16)
        picked = jnp.dot(sel_all, expand, preferred_element_type=F32) > 0.5
        for gi in range(g):
            k = plane(0, gi).astype(BF16)
            v = plane(1, gi).astype(BF16)
            q = q_ref[0, gi].reshape(rws, d).astype(BF16)
            s = lax.dot_general(q, k, (((1,), (1,)), ((), ())), preferred_element_type=F32) * NSA_SCALE
            ok = ok_pos & picked[gi * tq:(gi + 1) * tq]
            s = jnp.where(ok[None], s.reshape(hpg, tq, n), NEG_INF).reshape(rws, n)
            m_old = m_ref[gi]
            m_new = jnp.maximum(m_old, jnp.max(s, axis=-1, keepdims=True))
            m_safe = jnp.where(m_new > NEG_INF, m_new, 0.0)
            p = jnp.exp(s - m_safe)
            alpha = jnp.exp(m_old - m_safe)
            l_ref[gi] = alpha * l_ref[gi] + jnp.sum(p, axis=-1, keepdims=True)
            acc_ref[gi] = alpha * acc_ref[gi] + jnp.dot(p.astype(BF16), v, preferred_element_type=F32)
            m_ref[gi] = m_new

    update(lambda kv, gi: jnp.concatenate([pr[0, :, kv, gi, :] for pr in page_refs], axis=0),
           pgs * page, j * (pgs * page))

    @pl.when(j == n_steps - 1)
    def _():
        update(lambda kv, gi: tail_ref[0, :, (kv * g + gi) * d:(kv * g + gi + 1) * d], tail_ref.shape[1], past_len)
        gate = jax.nn.sigmoid(gl_ref[0])
        for gi in range(g):
            l = l_ref[gi]
            o = acc_ref[gi] / jnp.where(l > 0, l, 1.0)
            for hh in range(hpg):
                c = gi * LANES + hh * N_BRANCH + 1
                col = (gi * hpg + hh) * d
                o_ref[0, :, col:col + d] = prev_ref[0, :, col:col + d] + gate[:, c:c + 1] * o[hh * tq:(hh + 1) * tq]


def _nsa_slc_paged(qn, cache, page_table, sel, tail, glog, prev, *, qpos0):
    b, g, hpg, t, d = qn.shape
    page = cache.shape[1]
    width = 2 * g * d
    n_pages = page_table.shape[1]
    pgs = _tile(n_pages, 16, 1)
    n_steps = n_pages // pgs
    nsp = sel.shape[-1]
    nt = tail.shape[1]
    rws = hpg * t
    return pl.pallas_call(
        functools.partial(_nsa_slc_paged_kernel, pgs=pgs, g=g, hpg=hpg, tq=t, qpos0=qpos0,
                          past_len=n_pages * page, n_steps=n_steps),
        grid_spec=pltpu.PrefetchScalarGridSpec(
            num_scalar_prefetch=1,
            grid=(b, n_steps),
            in_specs=_page_specs(cache.shape, n_pages, pgs)
            + [pl.BlockSpec((1, g, hpg, t, d), lambda bi, j, pt: (bi, 0, 0, 0, 0)),
               pl.BlockSpec((1, g, t, nsp), lambda bi, j, pt: (bi, 0, 0, 0)),
               pl.BlockSpec((1, nt, width), lambda bi, j, pt: (bi, 0, 0)),
               pl.BlockSpec((1, t, g * LANES), lambda bi, j, pt: (bi, 0, 0)),
               pl.BlockSpec((1, t, g * hpg * d), lambda bi, j, pt: (bi, 0, 0))],
            out_specs=pl.BlockSpec((1, t, g * hpg * d), lambda bi, j, pt: (bi, 0, 0)),
            scratch_shapes=[pltpu.VMEM((g, rws, 1), F32), pltpu.VMEM((g, rws, 1), F32), pltpu.VMEM((g, rws, d), F32)]),
        out_shape=jax.ShapeDtypeStruct((b, t, g * hpg * d), F32),
        compiler_params=_params(("parallel", "arbitrary")),
        name="nsa_slc_paged",
    )(page_table.reshape(-1).astype(jnp.int32), *([cache] * pgs), qn, sel, tail, glog, prev)


def _gate_weight(w_in, d_model):
    g = NSA_KV_HEADS
    hpg = d_model // NSA_HEAD_DIM // g
    wg = w_in[:, d_model:].reshape(d_model, g, hpg * N_BRANCH)
    wg = jnp.pad(wg, ((0, 0), (0, 0), (0, LANES - hpg * N_BRANCH)))
    return wg.reshape(d_model, g * LANES)


def kernel(x_prompt, x_sample, state_ret, cache_cmp_kv, cache_slc_kv, cache_win_kv, page_table, p_prompt, p_sample, g_mix, g_ffn, w_ret_in, w_ret_out, w_nsa_in, g_nsa_q, w_nsa_out, g_kv, w_kv, g_k_cmp, g_k_slc, g_k_win, pe_cmp_k, w_cmp_k1, w_cmp_k2, pe_cmp_v, w_cmp_v1, w_cmp_v2, w_rg, b_rg, w_re, b_re, w_moe_up, w_moe_down, w_ple_up, g_ple, w_ple_gate):
    depth = g_mix.shape[0]
    n_a = w_ret_in.shape[0]
    g, d = NSA_KV_HEADS, NSA_HEAD_DIM
    gd = g * d
    d_model = x_prompt.shape[-1]
    page = cache_cmp_kv.shape[1]
    past_len = page_table.shape[1] * page

    groups = [
        dict(x=x_prompt.reshape(-1, d_model), p=p_prompt, b=x_prompt.shape[0], t=x_prompt.shape[1], pos0=0, s0=None),
        dict(x=x_sample.reshape(-1, d_model), p=p_sample, b=x_sample.shape[0], t=x_sample.shape[1], pos0=past_len, s0=state_ret),
    ]
    n_rows = [gr["x"].shape[0] for gr in groups]
    n_tok = sum(n_rows)
    offs = [0, n_rows[0]]
    for gr in groups:
        gr["ret"] = []

    for i in range(depth):
        w_gate_nsa = None if i < n_a else _gate_weight(w_nsa_in[i - n_a], d_model)
        for gr in groups:
            b, t = gr["b"], gr["t"]
            h = _rms([(gr["x"], 0)], g_mix[i], [BF16])[0]
            if i < n_a:
                qkvg = _mm(h, w_ret_in[i], tm=2048, tk=1024, name="ret_in")[0]
                pos = gr["pos0"] + jnp.arange(t)
                s0 = None if gr["s0"] is None else gr["s0"][i]
                o, s_new = _retention(qkvg.reshape(b, t, -1), pos, s0)
                gr["ret"].append(s_new)
                gr["x"] = _mm_resid(o.reshape(b * t, -1), w_ret_out[i], gr["x"], "ret_out")
            else:
                j = i - n_a
                qn = _mm_q(h, w_nsa_in[j], g_nsa_q[j], b, t)
                glog = _mm(h, w_gate_nsa, name="nsa_gate")[0].reshape(b, t, g * LANES)
                ctx = gr["ctx"]
                o1, sel = _nsa_cmp(qn, ctx["k_c"], ctx["v_c"], glog, gr["pos0"], ctx["n_cmp"], ctx["n_slc"])
                if ctx["slc_tail"] is None:
                    o2 = _nsa_attn(qn, ctx["slc"], glog, o1, qpos0=gr["pos0"], kpos0=0, branch=1, out_dtype=F32, sel=sel)
                else:
                    o2 = _nsa_slc_paged(qn, cache_slc_kv, page_table, sel, ctx["slc_tail"], glog, o1, qpos0=gr["pos0"])
                o3 = _nsa_attn(qn, ctx["win"], glog, o2, qpos0=gr["pos0"], kpos0=ctx["win_pos0"], branch=2,
                               out_dtype=BF16, window=WINDOW)
                gr["x"] = _mm_resid(o3.reshape(b * t, -1), w_nsa_out[j], gr["x"], "nsa_out")

        normed = [_rms([(gr["x"], 0)], g_ffn[i], [BF16], planes=True) for gr in groups]
        h16 = jnp.concatenate([nm[0] for nm in normed], axis=0)
        hf3 = jnp.concatenate([nm[1] for nm in normed], axis=0)
        y_tok = _moe(h16, hf3, i, w_rg[i], b_rg[i], w_re[i], b_re[i], w_moe_up, w_moe_down)

        for gi, gr in enumerate(groups):
            rows = n_rows[gi]
            x_new, hp = _rms([(gr["x"], 0), (y_tok, offs[gi]), (y_tok, n_tok + offs[gi])], g_ple[i], [BF16],
                             want_sum=True, rows=rows, tm=64)
            gr["x"] = _mm_ple(hp, w_ple_gate, i, x_new, gr["p"][i].reshape(rows, -1), w_ple_up)

        if i == n_a - 1:
            for gi, gr in enumerate(groups):
                b, t = gr["b"], gr["t"]
                hk = _rms([(gr["x"], 0)], g_kv, [BF16])[0]
                kv = _mm_kv(hk, w_kv, g_k_slc, g_k_win).reshape(b, t, 2 * N_BRANCH * gd)
                cmp_new, slc_new, win_new = kv[..., :2 * gd], kv[..., 2 * gd:4 * gd], kv[..., 4 * gd:]
                gr["cmp_new"], gr["slc_new"] = cmp_new, slc_new
                wcat_k, wcat_v = _cmp_wcat(w_cmp_k1), _cmp_wcat(w_cmp_v1)
                n_keys = gr["pos0"] + t
                n_cmp = (n_keys - CMP_LEN) // CMP_STRIDE + 1
                nch = n_cmp + CMP_LEN // CMP_STRIDE - 1
                if gi == 0:
                    slc_tail = None
                    win_keys, win_pos0 = win_new, 0
                    gr["win_state"] = win_new[:, t - min(WINDOW, t):]
                    cmp_rows = cmp_new.reshape(b, t, 2, g, d)
                    u_k = _cmp_hidden_dense(cmp_rows[:, :, 0], nch, wcat_k)
                    u_v = _cmp_hidden_dense(cmp_rows[:, :, 1], nch, wcat_v)
                else:
                    slc_tail = jnp.pad(slc_new, ((0, 0), (0, LANES - t), (0, 0)))
                    w_buf = cache_win_kv.shape[1]
                    win_all = jnp.concatenate([cache_win_kv.reshape(b, w_buf, 2 * gd), win_new], axis=1)
                    n_all = w_buf + t
                    gr["win_state"] = win_all[:, n_all - min(WINDOW, past_len + t):]
                    win_keys = jnp.pad(win_all, ((0, 0), (0, -n_all % LANES), (0, 0)))
                    win_pos0 = past_len - w_buf
                    assert nch * CMP_STRIDE == past_len
                    u_k, u_v = _cmp_hidden_paged(cache_cmp_kv, page_table, wcat_k, wcat_v)
                k_c = _cmp_finish(u_k, n_cmp, pe_cmp_k, w_cmp_k1, w_cmp_k2, g_k_cmp)
                v_c = _cmp_finish(u_v, n_cmp, pe_cmp_v, w_cmp_v1, w_cmp_v2, None)
                gr["ctx"] = dict(k_c=k_c, v_c=v_c, n_cmp=n_cmp, n_slc=-(-n_keys // SLC_BLOCK), slc=slc_new,
                                 slc_tail=slc_tail, win=win_keys, win_pos0=win_pos0)

    outs = []
    for gr in groups:
        outs.append(gr["x"].reshape(gr["b"], gr["t"], d_model))
    rets = [jnp.stack(gr["ret"]) for gr in groups]
    kvs = []
    for name in ("cmp_new", "slc_new", "win_state"):
        for gr in groups:
            a = gr[name]
            kvs.append(a.reshape(a.shape[0], a.shape[1], 2, g, d))
    return (outs[0], outs[1], rets[0], rets[1], kvs[0], kvs[1], kvs[2], kvs[3], kvs[4], kvs[5])
```

```python
import functools

import numpy as np
import jax
import jax.numpy as jnp
from jax import lax
from jax.experimental import pallas as pl
from jax.experimental.pallas import tpu as pltpu

F32 = jnp.float32
BF16 = jnp.bfloat16

RET_HEAD_DIM = 256
RET_V_DIM = 2 * RET_HEAD_DIM
RET_CHUNK = 128
ROPE_BASE = 10000.0
NSA_HEAD_DIM = 128
NSA_KV_HEADS = 4
NSA_SCALE = NSA_HEAD_DIM ** -0.5
CMP_LEN = 32
CMP_STRIDE = 16
SLC_BLOCK = 64
SLC_SHIFT = 6
SLC_TOP = 16
SLC_LOCAL = 2
FORCE_BONUS = 1e4
WINDOW = 512
N_BRANCH = 3
MOE_GROUPS = 8
MOE_EXPERTS_PER_GROUP = 8
MOE_TOP = 2
EPS = 1e-6

LANES = 128
SUBLANES = 8
VMEM_LIMIT_BYTES = 56 * 1024 * 1024
MOE_ROWS = 256
NEG_INF = float("-inf")


def _tile(n, pref, align):
    best = None
    for t in range(align, min(n, pref) + 1, align):
        if n % t == 0:
            best = t
    return n if best is None else best


def _params(semantics):
    return pltpu.CompilerParams(dimension_semantics=semantics, vmem_limit_bytes=VMEM_LIMIT_BYTES)


def _rms_kernel(*refs, n_add, want_sum, planes):
    adds, g_ref, outs = refs[:n_add], refs[n_add], refs[n_add + 1:]
    x = adds[0][...]
    for r in adds[1:]:
        x = x + r[...]
    y = (x * lax.rsqrt(jnp.mean(x * x, axis=-1, keepdims=True) + EPS)) * g_ref[...]
    if want_sum:
        outs[0][...] = x
        outs = outs[1:]
    if planes:
        for j in range(y.shape[1] // LANES):
            outs[-1][:, j, :] = y[:, j * LANES:(j + 1) * LANES]
        outs = outs[:-1]
    for o in outs:
        o[...] = y.astype(o.dtype)


def _rms(addends, g, out_dtypes, want_sum=False, rows=None, tm=128, planes=False):
    d = addends[0][0].shape[1]
    rows = addends[0][0].shape[0] if rows is None else rows
    tm = _tile(rows, tm, SUBLANES)
    for _, off in addends:
        assert off % tm == 0
    in_specs = [pl.BlockSpec((tm, d), functools.partial(lambda i, o: (i + o, 0), o=off // tm)) for _, off in addends]
    in_specs.append(pl.BlockSpec((1, d), lambda i: (0, 0)))
    dts = ([F32] if want_sum else []) + list(out_dtypes)
    out_specs = [pl.BlockSpec((tm, d), lambda i: (i, 0)) for _ in dts]
    out_shape = [jax.ShapeDtypeStruct((rows, d), dt) for dt in dts]
    if planes:
        out_specs.append(pl.BlockSpec((tm, d // LANES, LANES), lambda i: (i, 0, 0)))
        out_shape.append(jax.ShapeDtypeStruct((rows, d // LANES, LANES), F32))
    outs = pl.pallas_call(
        functools.partial(_rms_kernel, n_add=len(addends), want_sum=want_sum, planes=planes),
        grid=(rows // tm,),
        in_specs=in_specs,
        out_specs=out_specs,
        out_shape=out_shape,
        compiler_params=_params(("parallel",)),
        name="rms",
    )(*[a for a, _ in addends], g.reshape(1, d).astype(F32))
    return outs


def _rms_stack_kernel(*refs, starts):
    n = len(starts)
    xs, g_ref, o16_ref, o3_ref = refs[:n], refs[n], refs[n + 1], refs[n + 2]
    i = pl.program_id(0)
    x = xs[0][...]
    for k in range(1, n):
        x = jnp.where(i >= starts[k], xs[k][...], x)
    y = (x * lax.rsqrt(jnp.mean(x * x, axis=-1, keepdims=True) + EPS)) * g_ref[...]
    o16_ref[...] = y.astype(o16_ref.dtype)
    for j in range(y.shape[1] // LANES):
        o3_ref[:, j, :] = y[:, j * LANES:(j + 1) * LANES]


def _rms_stack(xs, g, tm=64):
    d = xs[0].shape[1]
    nblk = [x.shape[0] // tm for x in xs]
    assert all(x.shape[0] % tm == 0 for x in xs)
    starts = [sum(nblk[:k]) for k in range(len(xs))]
    rows = sum(nblk) * tm
    in_specs = [pl.BlockSpec((tm, d), functools.partial(lambda i, s, n: (jnp.clip(i - s, 0, n - 1), 0), s=s, n=n))
                for s, n in zip(starts, nblk)]
    in_specs.append(pl.BlockSpec((1, d), lambda i: (0, 0)))
    return pl.pallas_call(
        functools.partial(_rms_stack_kernel, starts=starts),
        grid=(sum(nblk),),
        in_specs=in_specs,
        out_specs=[pl.BlockSpec((tm, d), lambda i: (i, 0)), pl.BlockSpec((tm, d // LANES, LANES), lambda i: (i, 0, 0))],
        out_shape=[jax.ShapeDtypeStruct((rows, d), BF16), jax.ShapeDtypeStruct((rows, d // LANES, LANES), F32)],
        compiler_params=_params(("parallel",)),
        name="rms_stack",
    )(*xs, g.reshape(1, d).astype(F32))


def _mm_kernel(x_ref, w_ref, *rest, n_extra, epilogue, nk, in_place):
    extras, o_ref = rest[:n_extra], rest[n_extra]
    acc_ref = o_ref if in_place else rest[n_extra + 1]
    k = pl.program_id(2)

    def part():
        w = w_ref[0] if len(w_ref.shape) == 3 else w_ref[...]
        return jnp.dot(x_ref[...].astype(BF16), w.astype(BF16), preferred_element_type=F32)

    @pl.when(k == 0)
    def _():
        acc_ref[...] = part()

    @pl.when(k > 0)
    def _():
        acc_ref[...] += part()

    if epilogue is not None or not in_place:
        @pl.when(k == nk - 1)
        def _():
            acc = acc_ref[...]
            if epilogue is None:
                o_ref[...] = acc.astype(o_ref.dtype)
            else:
                epilogue(acc, extras, o_ref)


def _mm(x, w, *, layer=None, n_out=None, out_dtype=F32, tm=1024, tn=1024, tk=512, epilogue=None, extras=(),
        out_shape=None, out_spec=None, name="mm"):
    m, kdim = x.shape
    n = w.shape[-1] if n_out is None else n_out
    if m <= 256:
        tk = 2048
    tm, tn, tk = _tile(m, tm, SUBLANES), _tile(n, tn, LANES), _tile(kdim, tk, LANES)
    nk = kdim // tk
    in_place = out_shape is None and out_dtype == F32
    if out_shape is None:
        out_shape = jax.ShapeDtypeStruct((m, n), out_dtype)
        out_spec = pl.BlockSpec((tm, tn), lambda i, j, k: (i, j))
    return pl.pallas_call(
        functools.partial(_mm_kernel, n_extra=len(extras), epilogue=epilogue, nk=nk, in_place=in_place),
        grid=(m // tm, n // tn, nk),
        in_specs=[pl.BlockSpec((tm, tk), lambda i, j, k: (i, k)),
                  pl.BlockSpec((tk, tn), lambda i, j, k: (k, j)) if layer is None
                  else pl.BlockSpec((1, tk, tn), lambda i, j, k: (layer, k, j))]
        + [s for _, s in extras],
        out_specs=out_spec,
        out_shape=out_shape,
        scratch_shapes=[] if in_place else [pltpu.VMEM((tm, tn), F32)],
        compiler_params=_params(("parallel", "parallel", "arbitrary")),
        name=name,
    )(x, w, *[a for a, _ in extras]), (tm, tn)


def _tile_spec(tm, tn):
    return pl.BlockSpec((tm, tn), lambda i, j, k: (i, j))


def _mm_resid_kernel(x_ref, w_ref, r_ref, o_ref):
    def part():
        return jnp.dot(x_ref[...].astype(BF16), w_ref[...].astype(BF16), preferred_element_type=F32)

    @pl.when(pl.program_id(2) == 0)
    def _():
        o_ref[...] = r_ref[...] + part()

    @pl.when(pl.program_id(2) > 0)
    def _():
        o_ref[...] += part()


def _mm_resid(x, w, resid, name):
    m, n = resid.shape
    kdim = x.shape[1]
    tm, tn = _tile(m, 2048, SUBLANES), _tile(n, 1024, LANES)
    tk = _tile(kdim, 2048 if m <= 256 else 1024, LANES)
    return pl.pallas_call(
        _mm_resid_kernel,
        grid=(m // tm, n // tn, kdim // tk),
        in_specs=[pl.BlockSpec((tm, tk), lambda i, j, k: (i, k)), pl.BlockSpec((tk, tn), lambda i, j, k: (k, j)),
                  _tile_spec(tm, tn)],
        out_specs=_tile_spec(tm, tn),
        out_shape=jax.ShapeDtypeStruct((m, n), F32),
        compiler_params=_params(("parallel", "parallel", "arbitrary")),
        name=name,
    )(x, w, resid)


def _mm_ple(h, w_gate, layer, x, p, w_up):
    m, n = x.shape
    pdim = p.shape[1]
    tm, tn = _tile(m, 2048, SUBLANES), _tile(n, 1024, LANES)

    def epi(acc, extras, o_ref):
        pu =jnp.dot(extras[1][...].astype(BF16), extras[2][0].astype(BF16), preferred_element_type=F32)
        o_ref[...] = extras[0][...] + pu * jax.nn.sigmoid(acc)

    return _mm(h, w_gate, layer=layer, tm=tm, tn=tn, epilogue=epi,
               extras=[(x, _tile_spec(tm, tn)),
                       (p, pl.BlockSpec((tm, pdim), lambda i, j, k: (i, 0))),
                       (w_up, pl.BlockSpec((1, pdim, tn), lambda i, j, k: (layer, 0, j)))], name="ple_gate")[0]


def _group_rms(a):
    return a * lax.rsqrt(jnp.mean(a * a, axis=-1, keepdims=True) + EPS)


def _mm_kv(h, w_kv, g_k_slc, g_k_win):
    m = h.shape[0]
    gd = NSA_KV_HEADS * NSA_HEAD_DIM
    n = 2 * N_BRANCH * gd
    ones = jnp.ones((gd,), F32)
    gain = jnp.concatenate([ones, ones, jnp.tile(g_k_slc.astype(F32), NSA_KV_HEADS), ones,
                            jnp.tile(g_k_win.astype(F32), NSA_KV_HEADS), ones]).reshape(1, n)
    zeros = jnp.zeros((gd,), F32)
    flag = jnp.concatenate([zeros, zeros, ones, zeros, ones, zeros]).reshape(1, n)
    tm = _tile(m, 2048, SUBLANES)
    tn = 2 * gd

    def epi(acc, extras, o_ref):
        parts = [_group_rms(acc[:, c * NSA_HEAD_DIM:(c + 1) * NSA_HEAD_DIM]) for c in range(tn // NSA_HEAD_DIM)]
        normed = jnp.concatenate(parts, axis=-1) * extras[0][...]
        o_ref[...] = jnp.where(extras[1][...] > 0.5, normed, acc)

    row = pl.BlockSpec((1, tn), lambda i, j, k: (0, j))
    return _mm(h, w_kv, tm=tm, tn=tn, tk=1024, epilogue=epi, extras=[(gain, row), (flag, row)], name="kv_proj")[0]


def _mm_q(h, w_in, g_q, b, t):
    m, d_model = h.shape
    g, d = NSA_KV_HEADS, NSA_HEAD_DIM
    hpg = d_model // d // g
    tn = hpg * d
    tm = _tile(t, 2048, SUBLANES) if t >= 256 else _tile(m, 1024, t)
    seqs = max(tm // t, 1)
    rows = min(tm, t)

    def epi(acc, extras, o_ref):
        for sq in range(seqs):
            for hh in range(hpg):
                a = acc[sq * rows:(sq + 1) * rows, hh * d:(hh + 1) * d]
                o_ref[sq, 0, hh] = _group_rms(a) * extras[0][...]

    gq = g_q.reshape(1, d).astype(F32)
    tiles_per_b = t // rows
    return _mm(h, w_in, n_out=g * tn, tm=tm, tn=tn, tk=1024, epilogue=epi,
               extras=[(gq, pl.BlockSpec((1, d), lambda i, j, k: (0, 0)))],
               out_shape=jax.ShapeDtypeStruct((b, g, hpg, t, d), F32),
               out_spec=pl.BlockSpec((seqs, 1, hpg, rows, d),
                                     lambda i, j, k: (i // tiles_per_b, j, 0, i % tiles_per_b, 0)),
               name="nsa_q")[0]


def _ret_kernel(*refs, c, cp, nc, hb, has_s0):
    if has_s0:
        q_ref, k_ref, v_ref, g_ref, cos_ref, sin_ref, mask_ref, qd_ref, kd_ref, cd_ref, s0_ref, o_ref, so_ref, s_ref = refs
    else:
        q_ref, k_ref, v_ref, g_ref, cos_ref, sin_ref, mask_ref, qd_ref, kd_ref, cd_ref, o_ref, so_ref, s_ref = refs
    ci = pl.program_id(2)
    dk, dv = RET_HEAD_DIM, RET_V_DIM

    @pl.when(ci == 0)
    def _():
        if has_s0:
            s_ref[...] = s0_ref[0]
        else:
            s_ref[...] = jnp.zeros_like(s_ref)

    def padded(a):
        if cp == c:
            return a
        return jnp.concatenate([a, jnp.zeros((cp - c, a.shape[1]), a.dtype)], axis=0)

    half = dk // 2
    cos, sin = cos_ref[...], sin_ref[...]

    def rot(a):
        a1, a2 = a[:, :half], a[:, half:]
        return jnp.concatenate([a1 * cos - a2 * sin, a1 * sin + a2 * cos], axis=-1)

    for hh in range(hb):
        q = rot(padded(q_ref[0, :, hh * dk:(hh + 1) * dk]))
        k = rot(padded(k_ref[0, :, hh * dk:(hh + 1) * dk])) * (dk ** -0.5)
        v = padded(v_ref[0, :, hh * dv:(hh + 1) * dv]).astype(BF16)
        s = s_ref[hh]
        att = lax.dot_general(q.astype(BF16), k.astype(BF16), (((1,), (1,)), ((), ())),
                              preferred_element_type=F32) * mask_ref[hh]
        o = (jnp.dot(att.astype(BF16), v, preferred_element_type=F32)
             + jnp.dot((q * qd_ref[hh]).astype(BF16), s.astype(BF16), preferred_element_type=F32))
        kt = jnp.transpose(k * kd_ref[hh]).astype(BF16)
        s_new = s * cd_ref[hh] + jnp.dot(kt, v, preferred_element_type=F32)
        s_ref[hh] = s_new
        o = _group_rms(o[:c])
        gate = g_ref[0, :, hh * dv:(hh + 1) * dv]
        o_ref[0, :, hh * dv:(hh + 1) * dv] = (gate * jax.nn.sigmoid(gate) * o).astype(o_ref.dtype)

    @pl.when(ci == nc - 1)
    def _():
        so_ref[0] = s_ref[...]


def _retention(qkvg, pos, s0):
    b, t, width = qkvg.shape
    dk, dv = RET_HEAD_DIM, RET_V_DIM
    h = width // (2 * dk + 2 * dv)
    c = RET_CHUNK if t % RET_CHUNK == 0 else t
    nc = t // c
    cp = max(c, LANES)
    half = dk // 2
    lg = np.log1p(-(2.0 ** (-5.0 - np.arange(h, dtype=np.float32)))).astype(np.float32)
    idx = np.arange(c, dtype=np.float32)
    diff = idx[:, None] - idx[None, :]
    mask = np.where(diff >= 0, np.exp(np.maximum(diff, 0.0)[None] * lg[:, None, None]), 0.0).astype(np.float32)
    q_dec = np.exp((idx + 1.0)[None, :] * lg[:, None]).astype(np.float32)
    k_dec = np.exp((c - 1.0 - idx)[None, :] * lg[:, None]).astype(np.float32)
    c_dec = np.exp(c * lg).astype(np.float32)
    mask = jnp.asarray(np.pad(mask, ((0, 0), (0, cp - c), (0, cp - c))))
    q_dec = jnp.asarray(np.pad(q_dec, ((0, 0), (0, cp - c)))[..., None])
    k_dec = jnp.asarray(np.pad(k_dec, ((0, 0), (0, cp - c)))[..., None])
    c_dec = jnp.asarray(c_dec.reshape(h, 1, 1))
    inv = ROPE_BASE ** (-jnp.arange(half, dtype=F32) / half)
    ang = pos.astype(F32)[:, None] * inv[None, :]
    cos = jnp.pad(jnp.cos(ang), ((0, nc * cp - t), (0, 0)))
    sin = jnp.pad(jnp.sin(ang), ((0, nc * cp - t), (0, 0)))

    hb = _tile(h, 4, 1)
    ng = h // hb
    vb = (2 * h * dk) // (hb * dv)
    assert (2 * h * dk) % (hb * dv) == 0
    in_specs = [
        pl.BlockSpec((1, c, hb * dk), lambda bi, hi, ci: (bi, ci, hi)),
        pl.BlockSpec((1, c, hb * dk), lambda bi, hi, ci: (bi, ci, ng + hi)),
        pl.BlockSpec((1, c, hb * dv), lambda bi, hi, ci: (bi, ci, vb + hi)),
        pl.BlockSpec((1, c, hb * dv), lambda bi, hi, ci: (bi, ci, vb + ng + hi)),
        pl.BlockSpec((cp, half), lambda bi, hi, ci: (ci, 0)),
        pl.BlockSpec((cp, half), lambda bi, hi, ci: (ci, 0)),
        pl.BlockSpec((hb, cp, cp), lambda bi, hi, ci: (hi, 0, 0)),
        pl.BlockSpec((hb, cp, 1), lambda bi, hi, ci: (hi, 0, 0)),
        pl.BlockSpec((hb, cp, 1), lambda bi, hi, ci: (hi, 0, 0)),
        pl.BlockSpec((hb, 1, 1), lambda bi, hi, ci: (hi, 0, 0)),
    ]
    args = [qkvg, qkvg, qkvg, qkvg, cos, sin, mask, q_dec, k_dec, c_dec]
    if s0 is not None:
        in_specs.append(pl.BlockSpec((1, hb, dk, dv), lambda bi, hi, ci: (bi, hi, 0, 0)))
        args.append(s0)
    o, s_out = pl.pallas_call(
        functools.partial(_ret_kernel, c=c, cp=cp, nc=nc, hb=hb, has_s0=s0 is not None),
        grid=(b, ng, nc),
        in_specs=in_specs,
        out_specs=[pl.BlockSpec((1, c, hb * dv), lambda bi, hi, ci: (bi, ci, hi)),
                   pl.BlockSpec((1, hb, dk, dv), lambda bi, hi, ci: (bi, hi, 0, 0))],
        out_shape=[jax.ShapeDtypeStruct((b, t, h * dv), BF16), jax.ShapeDtypeStruct((b, h, dk, dv), F32)],
        scratch_shapes=[pltpu.VMEM((hb, dk, dv), F32)],
        compiler_params=_params(("parallel", "parallel", "arbitrary")),
        name="retention",
    )(*args)
    return o, s_out


def _router_kernel(h_ref, w_ref, b_ref, ids_ref, gates_ref):
    logits = jnp.dot(h_ref[...], w_ref[...].astype(BF16), preferred_element_type=F32) + b_ref[...]
    lane = lax.broadcasted_iota(jnp.int32, logits.shape, 1)
    big = jnp.int32(LANES)
    ng, ne = MOE_GROUPS, MOE_EXPERTS_PER_GROUP
    gl = jnp.where(lane < ng, logits, NEG_INF)
    gmax = jnp.max(gl, axis=-1, keepdims=True)
    gsum = jnp.sum(jnp.exp(gl - gmax), axis=-1, keepdims=True)
    g_sel = jnp.min(jnp.where(gl == gmax, lane, big), axis=-1, keepdims=True)
    g_w = 1.0 / gsum
    lo = ng + g_sel * ne
    in_group = (lane >= lo) & (lane < lo + ne)
    el = jnp.where(in_group, logits, NEG_INF)
    emax = jnp.max(el, axis=-1, keepdims=True)
    ee = jnp.exp(el - emax)
    ep = ee / jnp.sum(ee, axis=-1, keepdims=True)
    ep = jnp.where(in_group, ep, -1.0)
    p1 = jnp.max(ep, axis=-1, keepdims=True)
    i1 = jnp.min(jnp.where(ep == p1, lane, big), axis=-1, keepdims=True)
    ep2 = jnp.where(lane == i1, -1.0, ep)
    p2 = jnp.max(ep2, axis=-1, keepdims=True)
    i2 = jnp.min(jnp.where(ep2 == p2, lane, big), axis=-1, keepdims=True)
    psum = p1 + p2
    ids_ref[...] = jnp.where(lane == 0, i1 - ng, jnp.where(lane == 1, i2 - ng, 0))
    gates_ref[...] = jnp.where(lane == 0, g_w * p1 / psum, jnp.where(lane == 1, g_w * p2 / psum, 0.0))


def _router(hf, w_rg, b_rg, w_re, b_re):
    n, d = hf.shape
    ng, ne = MOE_GROUPS, MOE_EXPERTS_PER_GROUP
    w = jnp.concatenate([w_rg.astype(F32), jnp.transpose(w_re.astype(F32), (1, 0, 2)).reshape(d, ng * ne)], axis=1)
    w = jnp.pad(w, ((0, 0), (0, LANES - w.shape[1])))
    bias = jnp.pad(jnp.concatenate([b_rg.astype(F32), b_re.astype(F32).reshape(-1)]), (0, LANES - ng - ng * ne)).reshape(1, LANES)
    tm = _tile(n, 256, SUBLANES)
    ids, gates = pl.pallas_call(
        _router_kernel,
        grid=(n // tm,),
        in_specs=[pl.BlockSpec((tm, d), lambda i: (i, 0)), pl.BlockSpec((d, LANES), lambda i: (0, 0)),
                  pl.BlockSpec((1, LANES), lambda i: (0, 0))],
        out_specs=[pl.BlockSpec((tm, LANES), lambda i: (i, 0)), pl.BlockSpec((tm, LANES), lambda i: (i, 0))],
        out_shape=[jax.ShapeDtypeStruct((n, LANES), jnp.int32), jax.ShapeDtypeStruct((n, LANES), F32)],
        compiler_params=_params(("parallel",)),
        name="moe_router",
    )(hf, w, bias)
    return ids[:, :MOE_TOP], gates[:, :MOE_TOP]


def _row_copy(src, s_row, dst, d_row, sem):
    return pltpu.make_async_copy(src.at[pl.ds(s_row, 1)], dst.at[pl.ds(d_row, 1)], sem)


def _row_pitch(n_planes):
    return n_planes + (4 - n_planes) % SUBLANES


def _moe_up_kernel(be_ref, nu_ref, tokc_ref, tokn_ref, x_hbm, w_ref, h_ref, xbuf, sem, *, rows):
    b = pl.program_id(0)
    n_used = nu_ref[0]
    slot = lax.rem(b, 2)
    n_planes = x_hbm.shape[1]
    ROW_PITCH = _row_pitch(n_planes)

    def row_copy(tok_ref, s, r):
        dst = xbuf.at[pl.ds((s * rows + r) * ROW_PITCH, n_planes), :]
        return pltpu.make_async_copy(x_hbm.at[tok_ref[0, 0, r]], dst, sem.at[s])

    def gather(tok_ref, s):
        def body(r, carry):
            row_copy(tok_ref, s, r).start()
            return carry
        lax.fori_loop(0, rows, body, 0, unroll=8)

    @pl.when(b == 0)
    def _():
        gather(tokc_ref, 0)

    @pl.when(b + 1 < n_used)
    def _():
        gather(tokn_ref, 1 - slot)

    @pl.when(b < n_used)
    def _():
        def wait_body(r, carry):
            row_copy(tokc_ref, slot, r).wait()
            return carry
        lax.fori_loop(0, rows, wait_body, 0, unroll=8)
        f = h_ref.shape[1]
        base = slot * (rows * ROW_PITCH)
        acc = jnp.zeros((rows, 2 * f), F32)
        for j in range(0, n_planes, 2):
            x2 = jnp.concatenate([xbuf[pl.ds(base + j, rows, stride=ROW_PITCH), :],
                                  xbuf[pl.ds(base + j + 1, rows, stride=ROW_PITCH), :]], axis=1)
            acc = acc + jnp.dot(x2.astype(BF16), w_ref[0, 0, j * LANES:(j + 2) * LANES, :].astype(BF16),
                                preferred_element_type=F32)
        a, g = acc[:, :f], acc[:, f:]
        h_ref[...] = (a * jax.nn.sigmoid(a) * g).astype(h_ref.dtype)

    @pl.when(b >= n_used)
    def _():
        h_ref[...] = jnp.zeros_like(h_ref)


def _moe_down_kernel(be_ref, nu_ref, nv_ref, dstc_ref, dstp_ref, h_ref, g_ref, w_ref, y_hbm, ybuf, sem, *, nb):
    b = pl.program_id(0)
    n_used = nu_ref[0]
    slot = lax.rem(b, 2)

    def scatter(dst_ref, s, count, wait):
        def one(r):
            cp = _row_copy(ybuf.at[s], r, y_hbm, dst_ref[0, 0, r], sem.at[s])
            if wait:
                cp.wait()
            else:
                cp.start()

        def group(i, carry):
            for u in range(SUBLANES):
                one(i * SUBLANES + u)
            return carry

        def single(r, carry):
            one(r)
            return carry

        full = count // SUBLANES
        lax.fori_loop(0, full, group, 0)
        lax.fori_loop(full * SUBLANES, count, single, 0)

    @pl.when(b < n_used)
    def _():
        y = jnp.dot(h_ref[...], w_ref[0, 0].astype(BF16), preferred_element_type=F32) * g_ref[...]
        ybuf[slot] = y
        scatter(dstc_ref, slot, nv_ref[b], False)

    @pl.when((b >= 1) & (b - 1 < n_used))
    def _():
        scatter(dstp_ref, 1 - slot, nv_ref[jnp.maximum(b - 1, 0)], True)

    @pl.when((b == nb - 1) & (b < n_used))
    def _():
        scatter(dstc_ref, slot, nv_ref[b], True)


def _moe(h16, hf3, layer, w_rg, b_rg, w_re, b_re, w_up, w_down):
    n, d = h16.shape
    e, f2 = w_up.shape[1], w_up.shape[3]
    f = f2 // 2
    rows = MOE_ROWS
    ids, gates = _router(h16, w_rg, b_rg, w_re, b_re)

    a = n * MOE_TOP
    nb = -(-a // rows) + e
    e_flat = ids.reshape(-1)
    a_idx = jnp.arange(a, dtype=jnp.int32)
    tok_flat = a_idx // MOE_TOP
    _, tok_s, dst_s, gate_s = lax.sort(
        (e_flat, tok_flat, (a_idx % MOE_TOP) * n + tok_flat, lax.bitcast_convert_type(gates.reshape(-1), jnp.int32)),
        num_keys=1, is_stable=True)
    packed = jnp.stack([tok_s, dst_s, gate_s, jnp.zeros_like(tok_s)], axis=1)
    counts = jnp.bincount(e_flat, length=e).astype(jnp.int32)
    starts = jnp.cumsum(counts) - counts
    blocks_per = (counts + rows - 1) // rows
    blk_end = jnp.cumsum(blocks_per)
    first_blk = blk_end - blocks_per
    n_used = blk_end[-1].astype(jnp.int32)
    blk_ids = jnp.arange(nb, dtype=jnp.int32)
    owner = jnp.minimum(jnp.searchsorted(blk_end, blk_ids, side="right"), e - 1).astype(jnp.int32)
    in_e0 = (blk_ids - first_blk[owner]) * rows
    n_valid = jnp.where(blk_ids < n_used, jnp.clip(counts[owner] - in_e0, 0, rows), 0).astype(jnp.int32)
    within = jnp.arange(rows, dtype=jnp.int32)[None, :]
    live = within < n_valid[:, None]
    src = jnp.clip((starts[owner] + in_e0)[:, None] + within, 0, a - 1)
    picked = packed[src]
    tok3 = jnp.where(live, picked[..., 0], 0).reshape(nb, 1, rows)
    dst3 = jnp.where(live, picked[..., 1], 0).reshape(nb, 1, rows)
    gate_buf = jnp.where(live, lax.bitcast_convert_type(picked[..., 2], F32), 0.0)
    block_expert = owner[jnp.minimum(blk_ids, n_used - 1)]
    n_used_arr = n_used.reshape(1)
    smem_blk = functools.partial(pl.BlockSpec, (1, 1, rows), memory_space=pltpu.SMEM)
    h_mid = pl.pallas_call(
        functools.partial(_moe_up_kernel, rows=rows),
        grid_spec=pltpu.PrefetchScalarGridSpec(
            num_scalar_prefetch=2,
            grid=(nb,),
            in_specs=[smem_blk(lambda b, be, nu: (b, 0, 0)),
                      smem_blk(lambda b, be, nu: (jnp.minimum(b + 1, nb - 1), 0, 0)),
                      pl.BlockSpec(memory_space=pl.ANY),
                      pl.BlockSpec((1, 1, d, f2), lambda b, be, nu: (layer, be[b], 0, 0))],
            out_specs=pl.BlockSpec((rows, f), lambda b, be, nu: (b, 0)),
            scratch_shapes=[pltpu.VMEM((2 * rows * _row_pitch(d // LANES), LANES), F32),
                            pltpu.SemaphoreType.DMA((2,))]),
        out_shape=jax.ShapeDtypeStruct((nb * rows, f), BF16),
        compiler_params=_params(("arbitrary",)),
        name="moe_up",
    )(block_expert, n_used_arr, tok3, tok3, hf3, w_up)

    y_tok = pl.pallas_call(
        functools.partial(_moe_down_kernel, nb=nb),
        grid_spec=pltpu.PrefetchScalarGridSpec(
            num_scalar_prefetch=3,
            grid=(nb,),
            in_specs=[smem_blk(lambda b, be, nu, nv: (b, 0, 0)),
                      smem_blk(lambda b, be, nu, nv: (jnp.maximum(b - 1, 0), 0, 0)),
                      pl.BlockSpec((rows, f), lambda b, be, nu, nv: (b, 0)),
                      pl.BlockSpec((rows, 1), lambda b, be, nu, nv: (b, 0)),
                      pl.BlockSpec((1, 1, f, d), lambda b, be, nu, nv: (layer, be[b], 0, 0))],
            out_specs=pl.BlockSpec(memory_space=pl.ANY),
            scratch_shapes=[pltpu.VMEM((2, rows, d), F32), pltpu.SemaphoreType.DMA((2,))]),
        out_shape=jax.ShapeDtypeStruct((MOE_TOP * n, d), F32),
        compiler_params=_params(("arbitrary",)),
        name="moe_down",
    )(block_expert, n_used_arr, n_valid, dst3, dst3, h_mid, gate_buf.reshape(nb * rows, 1), w_down)
    return y_tok


def _page_specs(cache_shape, n_pages, pgs):
    return [pl.BlockSpec((1,) + tuple(cache_shape[1:]),
                         functools.partial(lambda bi, j, pt, r: (pt[bi * n_pages + j * pgs + r], 0, 0, 0, 0), r=r))
            for r in range(pgs)]


def _cmp_paged_kernel(pt_ref, *refs, pgs, g, d):
    page_refs = refs[:pgs]
    perm_ref, wk_ref, wv_ref, uk_ref, uv_ref = refs[pgs:]
    cs = CMP_STRIDE
    page = page_refs[0].shape[1]
    cpp = page // cs
    xs = [[[jnp.dot(perm_ref[...], pr[0, :, kv, gi, :].astype(BF16), preferred_element_type=F32)
            for gi in range(g)] for kv in range(2)] for pr in page_refs]
    m = g * pgs * cpp
    accs = [jnp.zeros((m, wk_ref.shape[1]), F32), jnp.zeros((m, wv_ref.shape[1]), F32)]
    for pp in range(0, cs, 2):
        for kv, w_ref in enumerate((wk_ref, wv_ref)):
            halves = []
            for p in (pp, pp + 1):
                pieces = [xs[r][kv][gi][p * cpp:(p + 1) * cpp, :] for gi in range(g) for r in range(pgs)]
                halves.append(jnp.concatenate(pieces, axis=0))
            lhs = jnp.concatenate(halves, axis=1).astype(BF16)
            accs[kv] = accs[kv] + jnp.dot(lhs, w_ref[pp * d:(pp + 2) * d, :], preferred_element_type=F32)
    uk_ref[0] = accs[0].reshape(g, pgs * cpp, wk_ref.shape[1])
    uv_ref[0] = accs[1].reshape(g, pgs * cpp, wv_ref.shape[1])


def _cmp_hidden_paged(cache, page_table, wcat_k, wcat_v):
    page = cache.shape[1]
    g, d = cache.shape[3], cache.shape[4]
    b, n_pages = page_table.shape
    pgs = _tile(n_pages, 8, 1)
    cpp = page // CMP_STRIDE
    nch = n_pages * cpp
    perm = np.zeros((page, page), np.float32)
    for c in range(cpp):
        for p in range(CMP_STRIDE):
            perm[p * cpp + c, c * CMP_STRIDE + p] = 1.0
    hid2 = wcat_k.shape[1]
    const = lambda shape: pl.BlockSpec(shape, lambda bi, j, pt: (0, 0))
    uk, uv = pl.pallas_call(
        functools.partial(_cmp_paged_kernel, pgs=pgs, g=g, d=d),
        grid_spec=pltpu.PrefetchScalarGridSpec(
            num_scalar_prefetch=1,
            grid=(b, n_pages // pgs),
            in_specs=_page_specs(cache.shape, n_pages, pgs)
            + [const((page, page)), const(wcat_k.shape), const(wcat_v.shape)],
            out_specs=[pl.BlockSpec((1, g, pgs * cpp, hid2), lambda bi, j, pt: (bi, 0, j, 0))] * 2),
        out_shape=[jax.ShapeDtypeStruct((b, g, nch, hid2), F32)] * 2,
        compiler_params=_params(("parallel", "parallel")),
        name="cmp_hidden_paged",
    )(page_table.reshape(-1).astype(jnp.int32), *([cache] * pgs), jnp.asarray(perm, BF16),
      wcat_k.astype(BF16), wcat_v.astype(BF16))
    return uk.reshape(b * g, nch, hid2), uv.reshape(b * g, nch, hid2)
def _cmp_post_kernel(u_ref, pe_ref, w2_ref, g_ref, o_ref, *, n_cmp, norm):
    u = u_ref[0]
    nch, hid2 = u.shape
    hid = hid2 // 2
    nxt = pltpu.roll(u[:, hid:], nch - 1, axis=0)
    x = (pe_ref[0:1, :] + u[:, :hid]) + nxt
    y = 0.5 * x * (1.0 + jnp.tanh(0.7978845608028654 * (x + 0.044715 * (x * x * x))))
    z = jnp.dot(y.astype(BF16), w2_ref[...].astype(BF16), preferred_element_type=F32)
    if norm:
        z = _group_rms(z) * g_ref[...]
    row = lax.broadcasted_iota(jnp.int32, z.shape, 0)
    o_ref[0] = jnp.where(row < n_cmp, z, 0.0)


def _cmp_wcat(w1):
    r = CMP_LEN // CMP_STRIDE
    assert r == 2
    w1r = w1.reshape(r, w1.shape[0] // r, w1.shape[1])
    return jnp.concatenate([w1r[0], w1r[1]], axis=1)


def _cmp_hidden_dense(rows, nch, wcat):
    b, _, g, d = rows.shape
    ch = rows[:, :nch * CMP_STRIDE].reshape(b, nch, CMP_STRIDE, g, d)
    ch = jnp.transpose(ch, (0, 3, 1, 2, 4)).reshape(b * g * nch, CMP_STRIDE * d).astype(BF16)
    return _mm(ch, wcat, name="cmp_hidden")[0].reshape(b * g, nch, wcat.shape[1])


def _cmp_finish(u, n_cmp, pe, w1, w2, g_k):
    bg, nch, hid2 = u.shape
    hid = hid2 // 2
    d = w2.shape[1]
    pe_rows = jnp.pad(pe.reshape(1, -1), ((0, SUBLANES - 1), (0, 0)))
    pe_hid = _mm(pe_rows, w1, name="cmp_pe")[0]
    gain = (jnp.ones((d,), F32) if g_k is None else g_k.astype(F32)).reshape(1, d)
    return pl.pallas_call(
        functools.partial(_cmp_post_kernel, n_cmp=n_cmp, norm=g_k is not None),
        grid=(bg,),
        in_specs=[pl.BlockSpec((1, nch, hid2), lambda i: (i, 0, 0)), pl.BlockSpec((SUBLANES, hid), lambda i: (0, 0)),
                  pl.BlockSpec((hid, d), lambda i: (0, 0)), pl.BlockSpec((1, d), lambda i: (0, 0))],
        out_specs=pl.BlockSpec((1, nch, d), lambda i: (i, 0, 0)),
        out_shape=jax.ShapeDtypeStruct((bg, nch, d), F32),
        compiler_params=_params(("parallel",)),
        name="cmp_post",
    )(u, pe_hid, w2, gain)


def _slc_map(n_cmp, n_slc, rows, cols):
    a = SLC_BLOCK // CMP_STRIDE
    bb = CMP_LEN // CMP_STRIDE
    j = np.arange(n_slc)[:, None, None]
    i = j * a + np.arange(a)[None, :, None] + np.arange(bb)[None, None, :] - bb + 1
    i, jj = np.broadcast_arrays(i, j)
    ok = (i >= 0) & (i < n_cmp)
    m = np.zeros((rows, cols), np.float32)
    np.add.at(m, (i[ok], jj[ok]), 1.0)
    return jnp.asarray(m)


def _masked_softmax_rows(s):
    m = jnp.max(s, axis=-1, keepdims=True)
    e = jnp.exp(s - jnp.where(m > NEG_INF, m, 0.0))
    den = jnp.sum(e, axis=-1, keepdims=True)
    return e / jnp.where(den > 0, den, 1.0)


def _store_gated(o, gl_ref, prev_ref, o_ref, branch, hpg, tq):
    d = NSA_HEAD_DIM
    gate = jax.nn.sigmoid(gl_ref[0])
    for hh in range(hpg):
        c = hh * N_BRANCH + branch
        val = gate[:, c:c + 1] * o[hh * tq:(hh + 1) * tq]
        if prev_ref is not None:
            val = prev_ref[0, :, hh * d:(hh + 1) * d] + val
        o_ref[0, :, hh * d:(hh + 1) * d] = val.astype(o_ref.dtype)


def _nsa_cmp_kernel(q_ref, kc_ref, vc_ref, map_ref, gl_ref, o_ref, sel_ref, *, hpg, tq, pos0, n_cmp, n_slc, n_top):
    i = pl.program_id(2)
    rws = hpg * tq
    q = q_ref[0, 0].reshape(rws, NSA_HEAD_DIM).astype(BF16)
    kc = kc_ref[0].astype(BF16)
    s = lax.dot_general(q, kc, (((1,), (1,)), ((), ())), preferred_element_type=F32) * NSA_SCALE
    ncp = s.shape[1]
    tok = lax.broadcasted_iota(jnp.int32, (rws, 1), 0) & (tq - 1)
    pos = pos0 + i * tq + tok
    cidx = lax.broadcasted_iota(jnp.int32, (1, ncp), 1)
    ok = (cidx * CMP_STRIDE + (CMP_LEN - 1) <= pos) & (cidx < n_cmp)
    p = _masked_softmax_rows(jnp.where(ok, s, NEG_INF))
    o = jnp.dot(p.astype(BF16), vc_ref[0].astype(BF16), preferred_element_type=F32)
    _store_gated(o, gl_ref, None, o_ref, 0, hpg, tq)

    psum = jnp.sum(p.astype(BF16).astype(F32).reshape(hpg, tq, ncp), axis=0)
    imp = jnp.dot(psum, map_ref[...], precision=lax.Precision.HIGHEST, preferred_element_type=F32)
    nsp = imp.shape[1]
    posq = pos0 + i * tq + lax.broadcasted_iota(jnp.int32, (tq, 1), 0)
    blk = lax.broadcasted_iota(jnp.int32, (tq, nsp), 1)
    back = (posq >> SLC_SHIFT) - blk
    real = blk < n_slc
    valid = (blk * SLC_BLOCK <= posq) & real
    forced = (blk == 0) | ((back >= 0) & (back < SLC_LOCAL))
    score = jnp.where(valid, imp + jnp.where(forced, FORCE_BONUS, 0.0), NEG_INF)
    if tq % LANES == 0 and n_slc <= LANES:
        nr = -(-n_slc // SUBLANES) * SUBLANES
        st = jnp.transpose(score)[:nr]
        blk_t = lax.broadcasted_iota(jnp.int32, (nr, tq), 0)
        rank = jnp.zeros((nr, tq), jnp.int32)
        for kb in range(n_slc):
            row = st[kb:kb + 1, :]
            ahead = (row > st) | ((row == st) & (blk_t > kb))
            rank = rank + ahead.astype(jnp.int32)
        sel_t = jnp.where((rank < n_top) & (blk_t < n_slc), 1.0, 0.0)
        sel_t = jnp.concatenate([sel_t, jnp.zeros((nsp - nr, tq), F32)], axis=0)
        sel_ref[0, 0] = jnp.transpose(sel_t)
    else:
        rank = jnp.zeros((tq, nsp), jnp.int32)
        for kb in range(n_slc):
            col = score[:, kb:kb + 1]
            ahead = (col > score) | ((col == score) & (blk > kb))
            rank = rank + ahead.astype(jnp.int32)
        sel_ref[0, 0] = jnp.where((rank < n_top) & real, 1.0, 0.0)


def _nsa_cmp(qn, kc, vc, glog, pos0, n_cmp, n_slc):
    b, g, hpg, t, d = qn.shape
    ncp = kc.shape[1]
    nsp = -(-n_slc // LANES) * LANES
    tq = _tile(t, 128, SUBLANES)
    smap = _slc_map(n_cmp, n_slc, ncp, nsp)
    n_top = min(SLC_TOP, n_slc)
    return pl.pallas_call(
        functools.partial(_nsa_cmp_kernel, hpg=hpg, tq=tq, pos0=pos0, n_cmp=n_cmp, n_slc=n_slc, n_top=n_top),
        grid=(b, g, t // tq),
        in_specs=[pl.BlockSpec((1, 1, hpg, tq, d), lambda bi, gi, i: (bi, gi, 0, i, 0)),
                  pl.BlockSpec((1, ncp, d), lambda bi, gi, i: (bi * g + gi, 0, 0)),
                  pl.BlockSpec((1, ncp, d), lambda bi, gi, i: (bi * g + gi, 0, 0)),
                  pl.BlockSpec((ncp, nsp), lambda bi, gi, i: (0, 0)),
                  pl.BlockSpec((1, tq, LANES), lambda bi, gi, i: (bi, i, gi))],
        out_specs=[pl.BlockSpec((1, tq, hpg * d), lambda bi, gi, i: (bi, i, gi)),
                   pl.BlockSpec((1, 1, tq, nsp), lambda bi, gi, i: (bi, gi, i, 0))],
        out_shape=[jax.ShapeDtypeStruct((b, t, g * hpg * d), F32), jax.ShapeDtypeStruct((b, g, t, nsp), F32)],
        compiler_params=_params(("parallel", "parallel", "parallel")),
        name="nsa_cmp",
    )(qn, kc, vc, smap, glog)


def _nsa_attn_kernel(*refs, hpg, tq, tk, n_kt, qpos0, kpos0, window, use_sel, branch):
    refs = list(refs)
    q_ref, k_ref, v_ref = refs[:3]
    refs = refs[3:]
    sel_ref = None
    if use_sel:
        sel_ref = refs[0]
        refs = refs[1:]
    gl_ref, prev_ref, o_ref, qs_ref, s_ref, p_ref, bias_ref, m_ref, a_ref, acc_ref = refs
    i = pl.program_id(2)
    d = NSA_HEAD_DIM
    rws = hpg * tq
    rb = min(tq, 64)
    per_head = tq // rb
    pvb = min(rws, 512)
    qs_ref[...] = (q_ref[0, 0].reshape(rws, d) * NSA_SCALE).astype(BF16)
    posq = qpos0 + i * tq + lax.broadcasted_iota(jnp.int32, (tq, 1), 0)
    sel = sel_ref[0, 0].astype(BF16) if use_sel else None
    m_ref[...] = jnp.full_like(m_ref, NEG_INF)
    acc_ref[...] = jnp.zeros_like(acc_ref)
    ones = jnp.ones((tk, d), BF16)

    def body(kt, carry):
        key0 = kt * tk
        off = pl.multiple_of(key0, tk)
        k = k_ref[0, pl.ds(off, tk), :].astype(BF16)
        v1 = jnp.concatenate([v_ref[0, pl.ds(off, tk), :].astype(BF16), ones], axis=1)
        s_ref[...] = lax.dot_general(qs_ref[...], k, (((1,), (1,)), ((), ())), preferred_element_type=F32)
        kpos = kpos0 + key0 + lax.broadcasted_iota(jnp.int32, (1, tk), 1)
        ok = kpos <= posq
        if window is not None:
            ok = ok & (posq - kpos < window)
        if use_sel:
            nsp = sel.shape[1]
            kblk = (key0 + lax.broadcasted_iota(jnp.int32, (nsp, tk), 1)) >> SLC_SHIFT
            expand = (kblk == lax.broadcasted_iota(jnp.int32, (nsp, tk), 0)).astype(BF16)
            ok = ok & (jnp.dot(sel, expand, preferred_element_type=F32) > 0.5)
        bias_ref[...] = jnp.where(ok, 0.0, NEG_INF)
        for blk in range(rws // rb):
            rows = slice(blk * rb, (blk + 1) * rb)
            part = blk % per_head
            s = s_ref[rows, :] + bias_ref[part * rb:(part + 1) * rb, :]
            m_old = m_ref[rows, :]
            m_new = jnp.maximum(m_old, jnp.max(s, axis=-1, keepdims=True))
            m_safe = jnp.where(m_new > NEG_INF, m_new, 0.0)
            p_ref[rows, :] = jnp.exp(s - jnp.tile(m_safe, (1, tk // LANES))).astype(BF16)
            a_ref[rows, :] = jnp.exp(m_old - m_safe)
            m_ref[rows, :] = m_new
        for r0 in range(0, rws, pvb):
            rows = slice(r0, r0 + pvb)
            pv = jnp.dot(p_ref[rows, :], v1, preferred_element_type=F32)
            acc_ref[rows, :] = jnp.tile(a_ref[rows, :], (1, 2)) * acc_ref[rows, :] + pv
        return carry

    q_lo = qpos0 + i * tq
    q_hi = q_lo + tq - 1
    hi = jnp.clip((q_hi - kpos0) // tk + 1, 0, n_kt)
    if window is None:
        lo = 0
    else:
        lo = jnp.clip((q_lo - (window - 1) - kpos0) // tk, 0, n_kt)
    lax.fori_loop(lo, hi, body, 0)
    l = acc_ref[:, d:]
    o = acc_ref[:, :d] / jnp.where(l > 0, l, 1.0)
    _store_gated(o, gl_ref, prev_ref, o_ref, branch, hpg, tq)


def _nsa_attn(qn, kv, glog, prev, *, qpos0, kpos0, branch, out_dtype, window=None, sel=None, tk=512):
    b, g, hpg, t, d = qn.shape
    tk_total = kv.shape[1]
    tq = _tile(t, 256, SUBLANES)
    tk = _tile(tk_total, tk, LANES)
    n_kt = tk_total // tk
    in_specs = [pl.BlockSpec((1, 1, hpg, tq, d), lambda bi, gi, i: (bi, gi, 0, i, 0)),
                pl.BlockSpec((1, tk_total, d), lambda bi, gi, i: (bi, 0, gi)),
                pl.BlockSpec((1, tk_total, d), lambda bi, gi, i: (bi, 0, g + gi))]
    args = [qn, kv, kv]
    if sel is not None:
        nsp = sel.shape[-1]
        in_specs.append(pl.BlockSpec((1, 1, tq, nsp), lambda bi, gi, i: (bi, gi, i, 0)))
        args.append(sel)
    in_specs += [pl.BlockSpec((1, tq, LANES), lambda bi, gi, i: (bi, i, gi)),
                 pl.BlockSpec((1, tq, hpg * d), lambda bi, gi, i: (bi, i, gi))]
    args += [glog, prev]
    rws = hpg * tq
    return pl.pallas_call(
        functools.partial(_nsa_attn_kernel, hpg=hpg, tq=tq, tk=tk, n_kt=n_kt, qpos0=qpos0, kpos0=kpos0,
                          window=window, use_sel=sel is not None, branch=branch),
        grid=(b, g, t // tq),
        in_specs=in_specs,
        out_specs=pl.BlockSpec((1, tq, hpg * d), lambda bi, gi, i: (bi, i, gi)),
        out_shape=jax.ShapeDtypeStruct((b, t, g * hpg * d), out_dtype),
        scratch_shapes=[pltpu.VMEM((rws, d), BF16), pltpu.VMEM((rws, tk), F32), pltpu.VMEM((rws, tk), BF16),
                        pltpu.VMEM((tq, tk), F32), pltpu.VMEM((rws, LANES), F32), pltpu.VMEM((rws, LANES), F32),
                        pltpu.VMEM((rws, 2 * d), F32)],
        compiler_params=_params(("parallel", "parallel", "parallel")),
        name="nsa_attn_%d" % branch,
    )(*args)


def _nsa_slc_paged_kernel(pt_ref, *refs, pgs, g, hpg, tq, qpos0, past_len, n_steps):
    page_refs = refs[:pgs]
    q_ref, sel_ref, tail_ref, gl_ref, prev_ref, o_ref, m_ref, l_ref, acc_ref = refs[pgs:]
    j = pl.program_id(1)
    d = NSA_HEAD_DIM
    rws = hpg * tq
    page = page_refs[0].shape[1]
    nsp = sel_ref.shape[-1]
    posq = qpos0 + lax.broadcasted_iota(jnp.int32, (tq, 1), 0)
    sel_all = sel_ref[0].reshape(g * tq, nsp).astype(BF16)

    @pl.when(j == 0)
    def _():
        m_ref[...] = jnp.full_like(m_ref, NEG_INF)
        l_ref[...] = jnp.zeros_like(l_ref)
        acc_ref[...] = jnp.zeros_like(acc_ref)

    def update(plane, n, key0):
        ok_pos = key0 + lax.broadcasted_iota(jnp.int32, (1, n), 1) <= posq
        kblk = (key0 + lax.broadcasted_iota(jnp.int32, (nsp, n), 1)) >> SLC_SHIFT
        expand = (kblk == lax.broadcasted_iota(jnp.int32, (nsp, n), 0)).astype(BF16)
        picked = jnp.dot(sel_all, expand, preferred_element_type=F32) > 0.5
        for gi in range(g):
            k = plane(0, gi).astype(BF16)
            v = plane(1, gi).astype(BF16)
            q = q_ref[0, gi].reshape(rws, d).astype(BF16)
            s = lax.dot_general(q, k, (((1,), (1,)), ((), ())), preferred_element_type=F32) * NSA_SCALE
            ok = ok_pos & picked[gi * tq:(gi + 1) * tq]
            s = jnp.where(ok[None], s.reshape(hpg, tq, n), NEG_INF).reshape(rws, n)
            m_old = m_ref[gi]
            m_new = jnp.maximum(m_old, jnp.max(s, axis=-1, keepdims=True))
            m_safe = jnp.where(m_new > NEG_INF, m_new, 0.0)
            p = jnp.exp(s - m_safe)
            alpha = jnp.exp(m_old - m_safe)
            l_ref[gi] = alpha * l_ref[gi] + jnp.sum(p, axis=-1, keepdims=True)
            acc_ref[gi] = alpha * acc_ref[gi] + jnp.dot(p.astype(BF16), v, preferred_element_type=F32)
            m_ref[gi] = m_new

    update(lambda kv, gi: jnp.concatenate([pr[0, :, kv, gi, :] for pr in page_refs], axis=0),
           pgs * page, j * (pgs * page))

    @pl.when(j == n_steps - 1)
    def _():
        update(lambda kv, gi: tail_ref[0, :, (kv * g + gi) * d:(kv * g + gi + 1) * d], tail_ref.shape[1], past_len)
        gate = jax.nn.sigmoid(gl_ref[0])
        for gi in range(g):
            l = l_ref[gi]
            o = acc_ref[gi] / jnp.where(l > 0, l, 1.0)
            for hh in range(hpg):
                c = gi * LANES + hh * N_BRANCH + 1
                col = (gi * hpg + hh) * d
                o_ref[0, :, col:col + d] = prev_ref[0, :, col:col + d] + gate[:, c:c + 1] * o[hh * tq:(hh + 1) * tq]


def _nsa_slc_paged(qn, cache, page_table, sel, tail, glog, prev, *, qpos0):
    b, g, hpg, t, d = qn.shape
    page = cache.shape[1]
    width = 2 * g * d
    n_pages = page_table.shape[1]
    pgs = _tile(n_pages, 16, 1)
    n_steps = n_pages // pgs
    nsp = sel.shape[-1]
    nt = tail.shape[1]
    rws = hpg * t
    return pl.pallas_call(
        functools.partial(_nsa_slc_paged_kernel, pgs=pgs, g=g, hpg=hpg, tq=t, qpos0=qpos0,
                          past_len=n_pages * page, n_steps=n_steps),
        grid_spec=pltpu.PrefetchScalarGridSpec(
            num_scalar_prefetch=1,
            grid=(b, n_steps),
            in_specs=_page_specs(cache.shape, n_pages, pgs)
            + [pl.BlockSpec((1, g, hpg, t, d), lambda bi, j, pt: (bi, 0, 0, 0, 0)),
               pl.BlockSpec((1, g, t, nsp), lambda bi, j, pt: (bi, 0, 0, 0)),
               pl.BlockSpec((1, nt, width), lambda bi, j, pt: (bi, 0, 0)),
               pl.BlockSpec((1, t, g * LANES), lambda bi, j, pt: (bi, 0, 0)),
               pl.BlockSpec((1, t, g * hpg * d), lambda bi, j, pt: (bi, 0, 0))],
            out_specs=pl.BlockSpec((1, t, g * hpg * d), lambda bi, j, pt: (bi, 0, 0)),
            scratch_shapes=[pltpu.VMEM((g, rws, 1), F32), pltpu.VMEM((g, rws, 1), F32), pltpu.VMEM((g, rws, d), F32)]),
        out_shape=jax.ShapeDtypeStruct((b, t, g * hpg * d), F32),
        compiler_params=_params(("parallel", "arbitrary")),
        name="nsa_slc_paged",
    )(page_table.reshape(-1).astype(jnp.int32), *([cache] * pgs), qn, sel, tail, glog, prev)


def _gate_weight(w_in, d_model):
    g = NSA_KV_HEADS
    hpg = d_model // NSA_HEAD_DIM // g
    wg = w_in[:, d_model:].reshape(d_model, g, hpg * N_BRANCH)
    wg = jnp.pad(wg, ((0, 0), (0, 0), (0, LANES - hpg * N_BRANCH)))
    return wg.reshape(d_model, g * LANES)


def kernel(x_prompt, x_sample, state_ret, cache_cmp_kv, cache_slc_kv, cache_win_kv, page_table, p_prompt, p_sample, g_mix, g_ffn, w_ret_in, w_ret_out, w_nsa_in, g_nsa_q, w_nsa_out, g_kv, w_kv, g_k_cmp, g_k_slc, g_k_win, pe_cmp_k, w_cmp_k1, w_cmp_k2, pe_cmp_v, w_cmp_v1, w_cmp_v2, w_rg, b_rg, w_re, b_re, w_moe_up, w_moe_down, w_ple_up, g_ple, w_ple_gate):
    depth = g_mix.shape[0]
    n_a = w_ret_in.shape[0]
    g, d = NSA_KV_HEADS, NSA_HEAD_DIM
    gd = g * d
    d_model = x_prompt.shape[-1]
    page = cache_cmp_kv.shape[1]
    past_len = page_table.shape[1] * page

    groups = [
        dict(x=x_prompt.reshape(-1, d_model), p=p_prompt, b=x_prompt.shape[0], t=x_prompt.shape[1], pos0=0, s0=None),
        dict(x=x_sample.reshape(-1, d_model), p=p_sample, b=x_sample.shape[0], t=x_sample.shape[1], pos0=past_len, s0=state_ret),
    ]
    n_rows = [gr["x"].shape[0] for gr in groups]
    n_tok = sum(n_rows)
    offs = [0, n_rows[0]]
    for gr in groups:
        gr["ret"] = []

    for i in range(depth):
        w_gate_nsa = None if i < n_a else _gate_weight(w_nsa_in[i - n_a], d_model)
        for gr in groups:
            b, t = gr["b"], gr["t"]
            h = _rms([(gr["x"], 0)], g_mix[i], [BF16])[0]
            if i < n_a:
                qkvg = _mm(h, w_ret_in[i], tm=2048, tk=1024, name="ret_in")[0]
                pos = gr["pos0"] + jnp.arange(t)
                s0 = None if gr["s0"] is None else gr["s0"][i]
                o, s_new = _retention(qkvg.reshape(b, t, -1), pos, s0)
                gr["ret"].append(s_new)
                gr["x"] = _mm_resid(o.reshape(b * t, -1), w_ret_out[i], gr["x"], "ret_out")
            else:
                j = i - n_a
                qn = _mm_q(h, w_nsa_in[j], g_nsa_q[j], b, t)
                glog = _mm(h, w_gate_nsa, name="nsa_gate")[0].reshape(b, t, g * LANES)
                ctx = gr["ctx"]
                o1, sel = _nsa_cmp(qn, ctx["k_c"], ctx["v_c"], glog, gr["pos0"], ctx["n_cmp"], ctx["n_slc"])
                if ctx["slc_tail"] is None:
                    o2 = _nsa_attn(qn, ctx["slc"], glog, o1, qpos0=gr["pos0"], kpos0=0, branch=1, out_dtype=F32, sel=sel)
                else:
                    o2 = _nsa_slc_paged(qn, cache_slc_kv, page_table, sel, ctx["slc_tail"], glog, o1, qpos0=gr["pos0"])
                o3 = _nsa_attn(qn, ctx["win"], glog, o2, qpos0=gr["pos0"], kpos0=ctx["win_pos0"], branch=2,
                               out_dtype=BF16, window=WINDOW)
                gr["x"] = _mm_resid(o3.reshape(b * t, -1), w_nsa_out[j], gr["x"], "nsa_out")

        h16, hf3 = _rms_stack([gr["x"] for gr in groups], g_ffn[i])
        y_tok = _moe(h16, hf3, i, w_rg[i], b_rg[i], w_re[i], b_re[i], w_moe_up, w_moe_down)

        for gi, gr in enumerate(groups):
            rows = n_rows[gi]
            x_new, hp = _rms([(gr["x"], 0), (y_tok, offs[gi]), (y_tok, n_tok + offs[gi])], g_ple[i], [BF16],
                             want_sum=True, rows=rows, tm=64)
            gr["x"] = _mm_ple(hp, w_ple_gate, i, x_new, gr["p"][i].reshape(rows, -1), w_ple_up)

        if i == n_a - 1:
            for gi, gr in enumerate(groups):
                b, t = gr["b"], gr["t"]
                hk = _rms([(gr["x"], 0)], g_kv, [BF16])[0]
                kv = _mm_kv(hk, w_kv, g_k_slc, g_k_win).reshape(b, t, 2 * N_BRANCH * gd)
                cmp_new, slc_new, win_new = kv[..., :2 * gd], kv[..., 2 * gd:4 * gd], kv[..., 4 * gd:]
                gr["cmp_new"], gr["slc_new"] = cmp_new, slc_new
                wcat_k, wcat_v = _cmp_wcat(w_cmp_k1), _cmp_wcat(w_cmp_v1)
                n_keys = gr["pos0"] + t
                n_cmp = (n_keys - CMP_LEN) // CMP_STRIDE + 1
                nch = n_cmp + CMP_LEN // CMP_STRIDE - 1
                if gi == 0:
                    slc_tail = None
                    win_keys, win_pos0 = win_new, 0
                    gr["win_state"] = win_new[:, t - min(WINDOW, t):]
                    cmp_rows = cmp_new.reshape(b, t, 2, g, d)
                    u_k = _cmp_hidden_dense(cmp_rows[:, :, 0], nch, wcat_k)
                    u_v = _cmp_hidden_dense(cmp_rows[:, :, 1], nch, wcat_v)
                else:
                    slc_tail = jnp.pad(slc_new, ((0, 0), (0, LANES - t), (0, 0)))
                    w_buf = cache_win_kv.shape[1]
                    win_all = jnp.concatenate([cache_win_kv.reshape(b, w_buf, 2 * gd), win_new], axis=1)
                    n_all = w_buf + t
                    gr["win_state"] = win_all[:, n_all - min(WINDOW, past_len + t):]
                    win_keys = jnp.pad(win_all, ((0, 0), (0, -n_all % LANES), (0, 0)))
                    win_pos0 = past_len - w_buf
                    assert nch * CMP_STRIDE == past_len
                    u_k, u_v = _cmp_hidden_paged(cache_cmp_kv, page_table, wcat_k, wcat_v)
                k_c = _cmp_finish(u_k, n_cmp, pe_cmp_k, w_cmp_k1, w_cmp_k2, g_k_cmp)
                v_c = _cmp_finish(u_v, n_cmp, pe_cmp_v, w_cmp_v1, w_cmp_v2, None)
                gr["ctx"] = dict(k_c=k_c, v_c=v_c, n_cmp=n_cmp, n_slc=-(-n_keys // SLC_BLOCK), slc=slc_new,
                                 slc_tail=slc_tail, win=win_keys, win_pos0=win_pos0)

    outs = []
    for gr in groups:
        outs.append(gr["x"].reshape(gr["b"], gr["t"], d_model))
    rets = [jnp.stack(gr["ret"]) for gr in groups]
    kvs = []
    for name in ("cmp_new", "slc_new", "win_state"):
        for gr in groups:
            a = gr[name]
            kvs.append(a.reshape(a.shape[0], a.shape[1], 2, g, d))
    return (outs[0], outs[1], rets[0], rets[1], kvs[0], kvs[1], kvs[2], kvs[3], kvs[4], kvs[5])
```
